```python
import jax, jax.numpy as jnp
from jax import lax
import numpy as np

D_MODEL = 1024
BATCH = 32
SEQ = 2048
DEPTH = 2

HG_HEADS = 4
HG_KDIM = 128
HG_VDIM = 128
RET_HEADS = 4
RET_KDIM = 128
RET_VDIM = 256
MLA_HEADS = 4
MLA_Q_RANK = 256
MLA_KV_RANK = 128
MLA_NOPE = 128
MLA_ROPE = 64
MLA_VDIM = 128
N_BRANCH = 3
D_FF = -(-8 * D_MODEL // (3 * 256)) * 256
CHUNK = 64
Q_BLOCK = 128
ROPE_BASE = 10000.0
EPS = 1e-6
MASK_NEG = -1e30
EXP_CLIP = 60.0
IN_WIDTHS = (
    HG_HEADS * HG_KDIM, HG_HEADS * HG_KDIM, HG_HEADS * HG_VDIM, HG_HEADS * HG_VDIM,
    RET_HEADS * RET_KDIM, RET_HEADS * RET_KDIM, RET_HEADS * RET_VDIM, RET_HEADS * RET_VDIM,
    MLA_Q_RANK, MLA_KV_RANK, MLA_ROPE,
    N_BRANCH * D_MODEL,
)
N_IN = sum(IN_WIDTHS)

kernel_name = 'hybrid_hgrn2_retention_mla_gated_block'


def _rms_norm(x, w):
    xf = x.astype(jnp.float32)
    y = xf * lax.rsqrt(jnp.mean(xf * xf, axis=-1, keepdims=True) + EPS)
    return (y * w.astype(jnp.float32)).astype(x.dtype)


def _head_rms_norm(o, w):
    b, s, h, d = o.shape
    o = o * lax.rsqrt(jnp.mean(o * o, axis=-1, keepdims=True) + EPS)
    return o.reshape(b, s, h * d) * w.astype(jnp.float32)


def _heads(t, n):
    return t.reshape(t.shape[0], t.shape[1], n, -1)


def _rope(t, positions):
    half = t.shape[-1] // 2
    inv = ROPE_BASE ** (-jnp.arange(half, dtype=jnp.float32) / half)
    ang = positions.astype(jnp.float32)[..., None] * inv
    cos = jnp.cos(ang)[:, :, None, :]
    sin = jnp.sin(ang)[:, :, None, :]
    t1, t2 = t[..., :half], t[..., half:]
    return jnp.concatenate([t1 * cos - t2 * sin, t1 * sin + t2 * cos], axis=-1)


def _to_chunks(t):
    b, s, h, d = t.shape
    return t.reshape(b, s // CHUNK, CHUNK, h, d).transpose(1, 0, 3, 2, 4)


def _from_chunks(t):
    n, b, h, c, d = t.shape
    return t.transpose(1, 0, 3, 2, 4).reshape(b, n * c, h, d)


def _hgrn2_mix(q, k, v, log_f):
    b_, _, h_, kd = q.shape
    vd = v.shape[-1]
    qc, kc, vc = _to_chunks(q), _to_chunks(k), _to_chunks(v)
    cum = jnp.cumsum(_to_chunks(log_f), axis=3)
    causal = jnp.tril(jnp.ones((CHUNK, CHUNK), dtype=bool))[:, :, None]

    def step(state, inp):
        q_, k_, v_, c_ = inp
        o_inter = jnp.einsum('bhtk,bhkv->bhtv', q_ * jnp.exp(c_), state)
        diff = c_[:, :, :, None, :] - c_[:, :, None, :, :]
        decay = jnp.where(causal, jnp.exp(jnp.where(causal, diff, 0.0)), 0.0)
        attn = jnp.einsum('bhtsk,bhsk->bhts', decay * q_[:, :, :, None, :], k_)
        o = o_inter + jnp.einsum('bhts,bhsv->bhtv', attn, v_)
        c_last = c_[:, :, -1:, :]
        k_dec = k_ * jnp.exp(c_last - c_)
        state = state * jnp.exp(c_last[:, :, 0, :, None]) + jnp.einsum('bhsk,bhsv->bhkv', k_dec, v_)
        return state, o

    s0 = jnp.zeros((b_, h_, kd, vd), jnp.float32)
    _, o = lax.scan(step, s0, (qc, kc, vc, cum))
    return _from_chunks(o)


def _retention_mix(q, k, v, log_gamma):
    b_, _, h_, kd = q.shape
    vd = v.shape[-1]
    idx = jnp.arange(CHUNK, dtype=jnp.float32)
    causal = jnp.tril(jnp.ones((CHUNK, CHUNK), dtype=bool))[None]
    lg = log_gamma[:, None, None]
    dist = jnp.where(causal, (idx[:, None] - idx[None, :])[None], 0.0)
    intra = jnp.where(causal, jnp.exp(dist * lg), 0.0)
    q_decay = jnp.exp((idx + 1.0)[None, :] * log_gamma[:, None])[..., None]
    k_decay = jnp.exp((CHUNK - 1.0 - idx)[None, :] * log_gamma[:, None])[..., None]
    chunk_decay = jnp.exp(CHUNK * log_gamma)[:, None, None]

    def step(state, inp):
        q_, k_, v_ = inp
        attn = jnp.einsum('bhtd,bhsd->bhts', q_, k_) * intra
        o = jnp.einsum('bhts,bhsv->bhtv', attn, v_) + jnp.einsum('bhtd,bhdv->bhtv', q_ * q_decay, state)
        state = state * chunk_decay + jnp.einsum('bhsd,bhsv->bhdv', k_ * k_decay, v_)
        return state, o

    s0 = jnp.zeros((b_, h_, kd, vd), jnp.float32)
    _, o = lax.scan(step, s0, (_to_chunks(q), _to_chunks(k), _to_chunks(v)))
    return _from_chunks(o)


def _mla_attend(q, k, v):
    b_, s_, h_, dqk = q.shape
    nb = s_ // Q_BLOCK
    qb = q.reshape(b_, nb, Q_BLOCK, h_, dqk).transpose(1, 0, 2, 3, 4)
    kpos = jnp.arange(s_)
    scale = dqk ** -0.5

    def block(args):
        i, q_ = args
        qpos = i * Q_BLOCK + jnp.arange(Q_BLOCK)
        sc = jnp.einsum('bqhd,bkhd->bhqk', q_, k).astype(jnp.float32) * scale
        sc = jnp.where(kpos[None, :] <= qpos[:, None], sc, MASK_NEG)
        p = jax.nn.softmax(sc, axis=-1).astype(v.dtype)
        return jnp.einsum('bhqk,bkhv->bqhv', p, v)

    o = lax.map(block, (jnp.arange(nb), qb))
    return o.transpose(1, 0, 2, 3, 4).reshape(b_, s_, h_, v.shape[-1])


def setup_inputs(seed: int = 0) -> dict:
    key = jax.random.key(seed)
    ks = jax.random.split(key, 20)
    f32 = jnp.float32

    def nrm(k, shape, scale):
        return jax.random.normal(k, shape, f32) * scale

    def gain(k, shape):
        return 1.0 + 0.02 * jax.random.normal(k, shape, f32)

    x = jax.random.normal(ks[0], (BATCH, SEQ, D_MODEL), f32)
    positions = jnp.broadcast_to(jnp.arange(SEQ, dtype=jnp.int32), (BATCH, SEQ))
    return {
        'x': x,
        'positions': positions,
        'norm_mix_w': gain(ks[1], (DEPTH, D_MODEL)),
        'w_in': nrm(ks[2], (DEPTH, D_MODEL, N_IN), D_MODEL ** -0.5),
        'hg_lower_bounds': nrm(ks[3], (DEPTH, HG_HEADS * HG_KDIM), 0.5),
        'hg_norm_w': gain(ks[4], (DEPTH, HG_HEADS * HG_VDIM)),
        'ret_norm_w': gain(ks[5], (DEPTH, RET_HEADS * RET_VDIM)),
        'mla_q_norm_w': gain(ks[6], (DEPTH, MLA_Q_RANK)),
        'mla_w_uq': nrm(ks[7], (DEPTH, MLA_Q_RANK, MLA_HEADS * (MLA_NOPE + MLA_ROPE)), MLA_Q_RANK ** -0.5),
        'mla_kv_norm_w': gain(ks[8], (DEPTH, MLA_KV_RANK)),
        'mla_w_ukv': nrm(ks[9], (DEPTH, MLA_KV_RANK, MLA_HEADS * (MLA_NOPE + MLA_VDIM)), MLA_KV_RANK ** -0.5),
        'w_br_a': nrm(ks[10], (DEPTH, HG_HEADS * HG_VDIM, D_MODEL), (HG_HEADS * HG_VDIM) ** -0.5),
        'w_br_b': nrm(ks[11], (DEPTH, RET_HEADS * RET_VDIM, D_MODEL), (RET_HEADS * RET_VDIM) ** -0.5),
        'w_br_c': nrm(ks[12], (DEPTH, MLA_HEADS * MLA_VDIM, D_MODEL), (MLA_HEADS * MLA_VDIM) ** -0.5),
        'w_out': nrm(ks[13], (DEPTH, D_MODEL, D_MODEL), 0.5 * D_MODEL ** -0.5),
        'norm_ffn_w': gain(ks[14], (DEPTH, D_MODEL)),
        'w_ffn_in': nrm(ks[15], (DEPTH, D_MODEL, 2 * D_FF), D_MODEL ** -0.5),
        'w_ffn_out': nrm(ks[16], (DEPTH, D_FF, D_MODEL), 0.5 * D_FF ** -0.5),
        'final_norm_w': gain(ks[17], (D_MODEL,)),
    }


def reference(x, positions, norm_mix_w, w_in, hg_lower_bounds, hg_norm_w, ret_norm_w,
              mla_q_norm_w, mla_w_uq, mla_kv_norm_w, mla_w_ukv, w_br_a, w_br_b, w_br_c,
              w_out, norm_ffn_w, w_ffn_in, w_ffn_out, final_norm_w):
    f32 = jnp.float32
    b_, s_, d_ = x.shape
    split_points = [int(p) for p in np.cumsum(IN_WIDTHS)[:-1]]
    lb_p = jax.nn.softmax(hg_lower_bounds.astype(f32), axis=0)
    lb_all = jnp.cumsum(lb_p, axis=0) - lb_p[0]
    log_gamma = jnp.log1p(-jnp.exp2(-5.0 - jnp.arange(RET_HEADS, dtype=f32)))

    for l in range(DEPTH):
        h = _rms_norm(x, norm_mix_w[l])
        proj = (h @ w_in[l]).astype(f32)
        hq, hf, hi, hgate, rq, rk, rv, rgate, cq, ckv, kr, mg = jnp.split(proj, split_points, axis=-1)

        lb = lb_all[l]
        log_f = jax.nn.log_sigmoid(hf) + jnp.log1p(lb * jnp.exp(jnp.minimum(-hf, EXP_CLIP)))
        k_hg = (1.0 - lb) * jax.nn.sigmoid(-hf)
        o_hg = _hgrn2_mix(_heads(jax.nn.silu(hq), HG_HEADS), _heads(k_hg, HG_HEADS),
                          _heads(hi, HG_HEADS), _heads(log_f, HG_HEADS))
        y_a = _head_rms_norm(o_hg, hg_norm_w[l]) * jax.nn.silu(hgate)

        q_r = _rope(_heads(rq, RET_HEADS), positions)
        k_r = _rope(_heads(rk, RET_HEADS), positions) * (RET_KDIM ** -0.5)
        o_r = _retention_mix(q_r, k_r, _heads(rv, RET_HEADS), log_gamma)
        y_b = _head_rms_norm(o_r, ret_norm_w[l]) * jax.nn.silu(rgate)

        q_m = (_rms_norm(cq, mla_q_norm_w[l]) @ mla_w_uq[l]).astype(f32)
        q_m = q_m.reshape(b_, s_, MLA_HEADS, MLA_NOPE + MLA_ROPE)
        q_nope, q_pe = q_m[..., :MLA_NOPE], _rope(q_m[..., MLA_NOPE:], positions)
        kv = (_rms_norm(ckv, mla_kv_norm_w[l]) @ mla_w_ukv[l]).astype(f32)
        kv = kv.reshape(b_, s_, MLA_HEADS, MLA_NOPE + MLA_VDIM)
        k_nope, v_m = kv[..., :MLA_NOPE], kv[..., MLA_NOPE:]
        k_pe = _rope(kr[:, :, None, :], positions)
        q_full = jnp.concatenate([q_nope, q_pe], axis=-1)
        k_full = jnp.concatenate([k_nope, jnp.broadcast_to(k_pe, (b_, s_, MLA_HEADS, MLA_ROPE))], axis=-1)
        y_c = _mla_attend(q_full, k_full, v_m).reshape(b_, s_, MLA_HEADS * MLA_VDIM)

        gates = jax.nn.sigmoid(mg).reshape(b_, s_, N_BRANCH, d_)
        merged = (gates[:, :, 0, :] * (y_a @ w_br_a[l])
                  + gates[:, :, 1, :] * (y_b @ w_br_b[l])
                  + gates[:, :, 2, :] * (y_c @ w_br_c[l]))
        x = x + (merged.astype(x.dtype) @ w_out[l])

        h2 = _rms_norm(x, norm_ffn_w[l])
        gu = h2 @ w_ffn_in[l]
        x = x + (jax.nn.silu(gu[..., :D_FF]) * gu[..., D_FF:]) @ w_ffn_out[l]

    return _rms_norm(x, final_norm_w)
```

```python
import functools

import numpy as np
import jax
import jax.numpy as jnp
from jax import lax
from jax.experimental import pallas as pl
from jax.experimental.pallas import tpu as pltpu

F32 = jnp.float32
BF16 = jnp.bfloat16

D_MODEL = 1024
HEADS = 4
HG_K = 128
HG_V = 128
RET_K = 128
RET_V = 256
MLA_Q_RANK = 256
MLA_KV_RANK = 128
MLA_NOPE = 128
MLA_ROPE = 64
MLA_V = 128
D_FF = 2816
CHUNK = 64
ROPE_BASE = 10000.0
EPS = 1e-6
EXP_CLIP = 60.0
MASK_NEG = -1e30
LOG2E = 1.4426950408889634

LANES = 128
VMEM_LIMIT = 56 * 1024 * 1024

LEVELS = (32, 16, 8, 4, 2, 1)


def _rms(x, w):
    return x * lax.rsqrt(jnp.mean(x * x, axis=-1, keepdims=True) + EPS) * w


def _dot(a, b):
    return jnp.dot(a, b, preferred_element_type=F32)


def _dot_nt(a, b):
    return lax.dot_general(a, b, (((1,), (1,)), ((), ())), preferred_element_type=F32)


def _dot_tn(a, b):
    return lax.dot_general(a, b, (((0,), (0,)), ((), ())), preferred_element_type=F32)


def _params(sem):
    return pltpu.CompilerParams(dimension_semantics=sem, vmem_limit_bytes=VMEM_LIMIT)


def _rope_table_kernel(pos_ref, inv_ref, rc_ref, rs_ref, mc_ref, ms_ref):
    pos = pos_ref[...].astype(F32)
    ang = inv_ref[...] * pos
    ct = jnp.transpose(jnp.cos(ang))
    st = jnp.transpose(jnp.sin(ang))
    lo = lax.broadcasted_iota(jnp.int32, ct.shape, 1) < 64
    ct_r = pltpu.roll(ct, 64, 1)
    st_r = pltpu.roll(st, 64, 1)
    rc_ref[...] = jnp.where(lo, ct, ct_r)
    rs_ref[...] = jnp.where(lo, -st, st_r)
    mc_ref[...] = jnp.where(lo, ct_r, ct)
    ms_ref[...] = jnp.where(lo, -st_r, st)


def _rope_tables(positions, tt):
    t = positions.size
    pos = positions.reshape(1, t)
    inv64 = ROPE_BASE ** (-jnp.arange(64, dtype=F32) / 64)
    inv32 = ROPE_BASE ** (-jnp.arange(32, dtype=F32) / 32)
    inv = jnp.concatenate([inv64, inv32, inv32]).reshape(LANES, 1)
    tab = jax.ShapeDtypeStruct((t, LANES), F32)
    return pl.pallas_call(
        _rope_table_kernel,
        grid=(t // tt,),
        in_specs=[pl.BlockSpec((1, tt), lambda i: (0, i)),
                  pl.BlockSpec((LANES, 1), lambda i: (0, 0))],
        out_specs=[pl.BlockSpec((tt, LANES), lambda i: (i, 0))] * 4,
        out_shape=[tab] * 4,
        compiler_params=_params(("parallel",)),
        name="rope_tables",
    )(pos, inv)


def _rope(t, c, s):
    return t * c + pltpu.roll(t, 64, 1) * s


def _split3(x):
    a = x.astype(BF16)
    r = x - a.astype(F32)
    b = r.astype(BF16)
    c = (r - b.astype(F32)).astype(BF16)
    return a, b, c


def _level_ref(cum, m):
    w = cum.shape[1]
    if m >= 8:
        g = CHUNK // (2 * m)
        c3 = cum.reshape(g, 2 * m, w)
        return jnp.broadcast_to(c3[:, m - 1:m, :], (g, 2 * m, w)).reshape(CHUNK, w)
    c3 = cum.reshape(8, 8, w)

    def row(r):
        return jnp.broadcast_to(c3[:, r:r + 1, :], (8, 8, w))

    if m == 4:
        out = row(3)
    else:
        sub = lax.broadcasted_iota(jnp.int32, (8, 8, w), 1)
        out = jnp.where(sub < 4, row(1), row(5))
    return out.reshape(CHUNK, w)


def _hg_kernel(x_ref, nw_ref, w_ref, lb_ref, gw_ref, tril_ref, masks_ref, o_ref, p_scr, st_scr, *, ts):
    @pl.when(pl.program_id(1) == 0)
    def _():
        st_scr[...] = jnp.zeros_like(st_scr)

    h = _rms(x_ref[...], nw_ref[...]).astype(BF16)
    p_scr[...] = _dot(h, w_ref[...])

    width = HEADS * HG_K
    lb = lb_ref[...]
    gw = gw_ref[...]
    tril = tril_ref[...]
    rows = lax.broadcasted_iota(jnp.int32, (CHUNK, width), 0)
    upper = {m: (rows % (2 * m)) >= m for m in LEVELS}

    def chunk(ci, carry):
        r0 = pl.multiple_of(ci * CHUNK, CHUNK)
        q = jax.nn.silu(p_scr[pl.ds(r0, CHUNK), 0:width])
        hf = p_scr[pl.ds(r0, CHUNK), width:2 * width]
        v = p_scr[pl.ds(r0, CHUNK), 2 * width:3 * width].astype(BF16)
        gate = jax.nn.silu(p_scr[pl.ds(r0, CHUNK), 3 * width:4 * width])
        logf = jax.nn.log_sigmoid(hf) + jnp.log1p(lb * jnp.exp(jnp.minimum(-hf, EXP_CLIP)))
        k = (1.0 - lb) * jax.nn.sigmoid(-hf)

        a, b, c = _split3(logf)
        cum = _dot(tril, a) + _dot(tril, b) + _dot(tril, c)
        c_last = cum[CHUNK - 1:CHUNK, :]
        qd = (q * jnp.exp(cum)).astype(BF16)
        kd = (k * jnp.exp(c_last - cum)).astype(BF16)
        dec = jnp.exp(c_last)

        lq, lk = [q.astype(BF16)], [k.astype(BF16)]
        for m in LEVELS:
            if m == 1:
                e = jnp.exp(logf)
                lq.append(jnp.where(upper[m], q * e, 0.0).astype(BF16))
                lk.append(jnp.where(upper[m], 0.0, k).astype(BF16))
            else:
                e = jnp.exp(-jnp.abs(cum - _level_ref(cum, m)))
                lq.append(jnp.where(upper[m], q * e, 0.0).astype(BF16))
                lk.append(jnp.where(upper[m], 0.0, k * e).astype(BF16))

        outs = []
        for hd in range(HEADS):
            hs = slice(hd * HG_K, (hd + 1) * HG_K)
            att = jnp.zeros((CHUNK, CHUNK), F32)
            for li in range(len(lq)):
                att = att + masks_ref[li] * _dot_nt(lq[li][:, hs], lk[li][:, hs])
            st = st_scr[hd]
            o = _dot_nt(qd[:, hs], st.astype(BF16)) + _dot(att.astype(BF16), v[:, hs])
            st_scr[hd] = st * dec[:, hs] + _dot_tn(v[:, hs], kd[:, hs])
            o = o * lax.rsqrt(jnp.mean(o * o, axis=-1, keepdims=True) + EPS)
            outs.append(o)
        y = jnp.concatenate(outs, axis=1) * gw * gate
        o_ref[pl.ds(r0, CHUNK), :] = y.astype(o_ref.dtype)
        return carry

    lax.fori_loop(0, ts // CHUNK, chunk, 0)


def _level_masks():
    t = np.arange(CHUNK)[:, None]
    s = np.arange(CHUNK)[None, :]
    masks = [(t == s)]
    for m in LEVELS:
        masks.append((t // (2 * m) == s // (2 * m)) & (t % (2 * m) >= m) & (s % (2 * m) < m))
    return jnp.asarray(np.stack(masks).astype(np.float32))


def _hg_mixer(x2, nw, w, lb, gw, b, s, ts):
    width = HEADS * HG_K
    nblk = s // ts
    tril = jnp.asarray(np.tril(np.ones((CHUNK, CHUNK), np.float32)), BF16)
    masks = _level_masks()
    const = lambda shape: pl.BlockSpec(shape, lambda i, j: (0,) * len(shape))
    return pl.pallas_call(
        functools.partial(_hg_kernel, ts=ts),
        grid=(b, nblk),
        in_specs=[pl.BlockSpec((ts, D_MODEL), lambda i, j: (i * nblk + j, 0)),
                  const((1, D_MODEL)), const((D_MODEL, 4 * width)), const((1, width)), const((1, width)),
                  const((CHUNK, CHUNK)), const(masks.shape)],
        out_specs=pl.BlockSpec((ts, HEADS * HG_V), lambda i, j: (i * nblk + j, 0)),
        out_shape=jax.ShapeDtypeStruct((b * s, HEADS * HG_V), BF16),
        scratch_shapes=[pltpu.VMEM((ts, 4 * width), F32), pltpu.VMEM((HEADS, HG_V, HG_K), F32)],
        compiler_params=_params(("parallel", "arbitrary")),
        name="hgrn2_mixer",
    )(x2, nw, w, lb, gw, tril, masks)


def _ret_kernel(x_ref, nw_ref, w_ref, gw_ref, rc_ref, rs_ref, intra_ref, qdec_ref, kdec_ref, cdec_ref,
                o_ref, p_scr, st_scr, *, ts):
    @pl.when(pl.program_id(1) == 0)
    def _():
        st_scr[...] = jnp.zeros_like(st_scr)

    h = _rms(x_ref[...], nw_ref[...]).astype(BF16)
    p_scr[...] = _dot(h, w_ref[...])

    qk_w = HEADS * RET_K
    v_w = HEADS * RET_V
    gw = gw_ref[...]
    k_scale = RET_K ** -0.5

    def chunk(ci, carry):
        r0 = pl.multiple_of(ci * CHUNK, CHUNK)
        cos = rc_ref[pl.ds(r0, CHUNK), :]
        sin = rs_ref[pl.ds(r0, CHUNK), :]
        gate = jax.nn.silu(p_scr[pl.ds(r0, CHUNK), 2 * qk_w + v_w:2 * qk_w + 2 * v_w])
        outs = []
        for hd in range(HEADS):
            q = _rope(p_scr[pl.ds(r0, CHUNK), hd * RET_K:(hd + 1) * RET_K], cos, sin)
            k = _rope(p_scr[pl.ds(r0, CHUNK), qk_w + hd * RET_K:qk_w + (hd + 1) * RET_K], cos, sin) * k_scale
            v = p_scr[pl.ds(r0, CHUNK), 2 * qk_w + hd * RET_V:2 * qk_w + (hd + 1) * RET_V].astype(BF16)
            att = _dot_nt(q.astype(BF16), k.astype(BF16)) * intra_ref[hd]
            st = st_scr[hd]
            o = _dot(att.astype(BF16), v) + _dot_nt((q * qdec_ref[hd]).astype(BF16), st.astype(BF16))
            st_scr[hd] = st * cdec_ref[hd] + _dot_tn(v, (k * kdec_ref[hd]).astype(BF16))
            o = o * lax.rsqrt(jnp.mean(o * o, axis=-1, keepdims=True) + EPS)
            outs.append(o)
        y = jnp.concatenate(outs, axis=1) * gw * gate
        o_ref[pl.ds(r0, CHUNK), :] = y.astype(o_ref.dtype)
        return carry

    lax.fori_loop(0, ts // CHUNK, chunk, 0)


def _ret_tables():
    log_gamma = jnp.log1p(-jnp.exp2(-5.0 - jnp.arange(HEADS, dtype=F32)))
    idx = jnp.arange(CHUNK, dtype=F32)
    causal = jnp.tril(jnp.ones((CHUNK, CHUNK), dtype=bool))[None]
    lg = log_gamma[:, None, None]
    dist = jnp.where(causal, (idx[:, None] - idx[None, :])[None], 0.0)
    intra = jnp.where(causal, jnp.exp(dist * lg), 0.0)
    q_decay = jnp.exp((idx + 1.0)[None, :] * log_gamma[:, None])[..., None]
    k_decay = jnp.exp((CHUNK - 1.0 - idx)[None, :] * log_gamma[:, None])[..., None]
    chunk_decay = jnp.exp(CHUNK * log_gamma)[:, None, None]
    return (intra,
            jnp.broadcast_to(q_decay, (HEADS, CHUNK, RET_K)),
            jnp.broadcast_to(k_decay, (HEADS, CHUNK, RET_K)),
            jnp.broadcast_to(chunk_decay, (HEADS, 1, RET_K)))


def _ret_mixer(x2, nw, w, gw, rc, rs, b, s, ts):
    nblk = s // ts
    intra, qdec, kdec, cdec = _ret_tables()
    n_proj = 2 * HEADS * RET_K + 2 * HEADS * RET_V
    const = lambda shape: pl.BlockSpec(shape, lambda i, j: (0,) * len(shape))
    rows = lambda width: pl.BlockSpec((ts, width), lambda i, j: (i * nblk + j, 0))
    return pl.pallas_call(
        functools.partial(_ret_kernel, ts=ts),
        grid=(b, nblk),
        in_specs=[rows(D_MODEL), const((1, D_MODEL)), const((D_MODEL, n_proj)), const((1, HEADS * RET_V)),
                  rows(LANES), rows(LANES),
                  const(intra.shape), const(qdec.shape), const(kdec.shape), const(cdec.shape)],
        out_specs=rows(HEADS * RET_V),
        out_shape=jax.ShapeDtypeStruct((b * s, HEADS * RET_V), BF16),
        scratch_shapes=[pltpu.VMEM((ts, n_proj), F32), pltpu.VMEM((HEADS, RET_V, RET_K), F32)],
        compiler_params=_params(("parallel", "arbitrary")),
        name="retention_mixer",
    )(x2, nw, w, gw, rc, rs, intra, qdec, kdec, cdec)


def _mla_pre_kernel(x_ref, nw_ref, wc_ref, qn_ref, wuq_ref, kvn_ref, wukv_ref, mc_ref, ms_ref,
                    q_ref, k_ref, v_ref):
    h = _rms(x_ref[...], nw_ref[...]).astype(BF16)
    c = _dot(h, wc_ref[...])
    cos = mc_ref[...]
    sin = ms_ref[...]
    cq = _rms(c[:, 0:MLA_Q_RANK], qn_ref[...]).astype(BF16)
    qm = _dot(cq, wuq_ref[...])
    ckv = _rms(c[:, MLA_Q_RANK:MLA_Q_RANK + MLA_KV_RANK], kvn_ref[...]).astype(BF16)
    kv = _dot(ckv, wukv_ref[...])
    k_pe = _rope(c[:, MLA_Q_RANK + MLA_KV_RANK:], cos, sin).astype(BF16)
    lane = lax.broadcasted_iota(jnp.int32, k_pe.shape, 1)
    first = (lane % 64) < 32
    nope_w = HEADS * MLA_NOPE
    for pair in range(HEADS // 2):
        q_pe = _rope(qm[:, nope_w + pair * LANES:nope_w + (pair + 1) * LANES], cos, sin)
        for sub in range(2):
            hd = 2 * pair + sub
            keep = first if sub == 0 else jnp.logical_not(first)
            q_ref[:, hd * 256:hd * 256 + MLA_NOPE] = qm[:, hd * MLA_NOPE:(hd + 1) * MLA_NOPE].astype(BF16)
            q_ref[:, hd * 256 + MLA_NOPE:(hd + 1) * 256] = jnp.where(keep, q_pe, 0.0).astype(BF16)
    for hd in range(HEADS):
        k_ref[:, hd * 256:hd * 256 + MLA_NOPE] = kv[:, hd * MLA_NOPE:(hd + 1) * MLA_NOPE].astype(BF16)
        k_ref[:, hd * 256 + MLA_NOPE:(hd + 1) * 256] = k_pe
    v_ref[...] = kv[:, nope_w:].astype(BF16)


def _mla_pre(x2, nw, wc, qn, wuq, kvn, wukv, mc, ms, tm):
    t = x2.shape[0]
    const = lambda shape: pl.BlockSpec(shape, lambda i: (0,) * len(shape))
    rows = lambda width: pl.BlockSpec((tm, width), lambda i: (i, 0))
    return pl.pallas_call(
        _mla_pre_kernel,
        grid=(t // tm,),
        in_specs=[rows(D_MODEL), const((1, D_MODEL)), const(wc.shape), const((1, MLA_Q_RANK)), const(wuq.shape),
                  const((1, MLA_KV_RANK)), const(wukv.shape), rows(LANES), rows(LANES)],
        out_specs=[rows(HEADS * 256), rows(HEADS * 256), rows(HEADS * MLA_V)],
        out_shape=[jax.ShapeDtypeStruct((t, HEADS * 256), BF16), jax.ShapeDtypeStruct((t, HEADS * 256), BF16),
                   jax.ShapeDtypeStruct((t, HEADS * MLA_V), BF16)],
        compiler_params=_params(("parallel",)),
        name="mla_projections",
    )(x2, nw, wc, qn, wuq, kvn, wukv, mc, ms)


def _mla_attn_kernel(q_ref, k_ref, v_ref, o_ref, *, tq):
    qi = pl.program_id(1)
    c_exp = (MLA_NOPE + MLA_ROPE) ** -0.5 * LOG2E
    row = lax.broadcasted_iota(jnp.int32, (tq, tq), 0)
    col = lax.broadcasted_iota(jnp.int32, (tq, tq), 1)
    causal = col <= row

    for hd in range(HEADS):
        q = q_ref[:, hd * 256:(hd + 1) * 256]

        def step(kj, carry, masked):
            m, l, acc = carry
            k0 = pl.multiple_of(kj * tq, tq)
            s = _dot_nt(q, k_ref[pl.ds(k0, tq), hd * 256:(hd + 1) * 256]) * c_exp
            if masked:
                s = jnp.where(causal, s, MASK_NEG)
            m_new = jnp.maximum(m, jnp.max(s, axis=-1, keepdims=True))
            alpha = jnp.exp2(m - m_new)
            p = jnp.exp2(s - m_new)
            l = l * alpha + jnp.sum(p, axis=-1, keepdims=True)
            acc = acc * alpha + _dot(p.astype(BF16), v_ref[pl.ds(k0, tq), hd * MLA_V:(hd + 1) * MLA_V])
            return m_new, l, acc

        init = (jnp.full((tq, 1), MASK_NEG, F32), jnp.zeros((tq, 1), F32), jnp.zeros((tq, MLA_V), F32))
        carry = lax.fori_loop(0, qi, functools.partial(step, masked=False), init)
        m, l, acc = step(qi, carry, True)
        o_ref[:, hd * MLA_V:(hd + 1) * MLA_V] = (acc / l).astype(o_ref.dtype)


def _mla_attn(q, k, v, b, s, tq):
    nq = s // tq
    return pl.pallas_call(
        functools.partial(_mla_attn_kernel, tq=tq),
        grid=(b, nq),
        in_specs=[pl.BlockSpec((tq, HEADS * 256), lambda i, j: (i * nq + j, 0)),
                  pl.BlockSpec((s, HEADS * 256), lambda i, j: (i, 0)),
                  pl.BlockSpec((s, HEADS * MLA_V), lambda i, j: (i, 0))],
        out_specs=pl.BlockSpec((tq, HEADS * MLA_V), lambda i, j: (i * nq + j, 0)),
        out_shape=jax.ShapeDtypeStruct((b * s, HEADS * MLA_V), BF16),
        compiler_params=_params(("parallel", "arbitrary")),
        name="mla_attention",
    )(q, k, v)


def _merge_kernel(x_ref, nw_ref, wg_ref, ya_ref, yb_ref, yc_ref, wa_ref, wb_ref, wc_ref, wo_ref, o_ref):
    x = x_ref[...]
    h = _rms(x, nw_ref[...]).astype(BF16)
    merged = (jax.nn.sigmoid(_dot(h, wg_ref[:, 0:D_MODEL])) * _dot(ya_ref[...], wa_ref[...])
              + jax.nn.sigmoid(_dot(h, wg_ref[:, D_MODEL:2 * D_MODEL])) * _dot(yb_ref[...], wb_ref[...])
              + jax.nn.sigmoid(_dot(h, wg_ref[:, 2 * D_MODEL:3 * D_MODEL])) * _dot(yc_ref[...], wc_ref[...]))
    o_ref[...] = x + _dot(merged.astype(BF16), wo_ref[...])


def _merge(x2, nw, wg, ya, yb, yc, wa, wb, wc, wo, tm):
    t = x2.shape[0]
    const = lambda a: pl.BlockSpec(a.shape, lambda i: (0,) * a.ndim)
    rows = lambda a: pl.BlockSpec((tm, a.shape[1]), lambda i: (i, 0))
    return pl.pallas_call(
        _merge_kernel,
        grid=(t // tm,),
        in_specs=[rows(x2), const(nw), const(wg), rows(ya), rows(yb), rows(yc),
                  const(wa), const(wb), const(wc), const(wo)],
        out_specs=rows(x2),
        out_shape=jax.ShapeDtypeStruct(x2.shape, F32),
        compiler_params=_params(("parallel",)),
        name="gated_merge",
    )(x2, nw, wg, ya, yb, yc, wa, wb, wc, wo)


def _ffn_kernel(x_ref, nw_ref, wi_ref, wo_ref, fw_ref, o_ref, *, final_norm):
    x = x_ref[...]
    h = _rms(x, nw_ref[...]).astype(BF16)
    g = _dot(h, wi_ref[:, 0:D_FF])
    u = _dot(h, wi_ref[:, D_FF:2 * D_FF])
    y = x + _dot((jax.nn.silu(g) * u).astype(BF16), wo_ref[...])
    if final_norm:
        y = _rms(y, fw_ref[...])
    o_ref[...] = y


def _ffn(x2, nw, wi, wo, fw, tm, final_norm):
    t = x2.shape[0]
    const = lambda a: pl.BlockSpec(a.shape, lambda i: (0,) * a.ndim)
    rows = lambda a: pl.BlockSpec((tm, a.shape[1]), lambda i: (i, 0))
    return pl.pallas_call(
        functools.partial(_ffn_kernel, final_norm=final_norm),
        grid=(t // tm,),
        in_specs=[rows(x2), const(nw), const(wi), const(wo), const(fw)],
        out_specs=rows(x2),
        out_shape=jax.ShapeDtypeStruct(x2.shape, F32),
        compiler_params=_params(("parallel",)),
        name="swiglu_ffn",
    )(x2, nw, wi, wo, fw)


def _in_proj_slices(w_in_l):
    widths = (512, 512, 512, 512, 512, 512, 1024, 1024, MLA_Q_RANK, MLA_KV_RANK, MLA_ROPE, 3 * D_MODEL)
    offs = np.concatenate([[0], np.cumsum(widths)])
    cols = [w_in_l[:, int(offs[i]):int(offs[i + 1])] for i in range(len(widths))]
    w_hg = jnp.concatenate(cols[0:4], axis=1)
    w_ret = jnp.concatenate(cols[4:8], axis=1)
    kr = cols[10]
    kr1, kr2 = kr[:, :32], kr[:, 32:]
    w_c = jnp.concatenate([cols[8], cols[9], kr1, kr1, kr2, kr2], axis=1)
    return w_hg.astype(BF16), w_ret.astype(BF16), w_c.astype(BF16), cols[11].astype(BF16)


def _uq_layout(w_uq_l):
    per = MLA_NOPE + MLA_ROPE
    heads = [w_uq_l[:, hd * per:(hd + 1) * per] for hd in range(HEADS)]
    nope = [hh[:, :MLA_NOPE] for hh in heads]
    pe = [hh[:, MLA_NOPE:] for hh in heads]
    pairs = []
    for p in range(HEADS // 2):
        a, b = pe[2 * p], pe[2 * p + 1]
        pairs += [a[:, :32], b[:, :32], a[:, 32:], b[:, 32:]]
    return jnp.concatenate(nope + pairs, axis=1).astype(BF16)


def _ukv_layout(w_ukv_l):
    per = MLA_NOPE + MLA_V
    heads = [w_ukv_l[:, hd * per:(hd + 1) * per] for hd in range(HEADS)]
    return jnp.concatenate([hh[:, :MLA_NOPE] for hh in heads] + [hh[:, MLA_NOPE:] for hh in heads],
                           axis=1).astype(BF16)


def _block(n, want):
    return want if n % want == 0 else n


def kernel(x, positions, norm_mix_w, w_in, hg_lower_bounds, hg_norm_w, ret_norm_w, mla_q_norm_w, mla_w_uq, mla_kv_norm_w, mla_w_ukv, w_br_a, w_br_b, w_br_c, w_out, norm_ffn_w, w_ffn_in, w_ffn_out, final_norm_w):
    b, s, d = x.shape
    depth = w_in.shape[0]
    assert d == D_MODEL and s % CHUNK == 0
    t = b * s
    ts = _block(s, 512)
    tm = _block(t, 512)
    tq = _block(s, 256)

    lb_p = jax.nn.softmax(hg_lower_bounds.astype(F32), axis=0)
    lb_all = jnp.cumsum(lb_p, axis=0) - lb_p[0]

    rc, rs, mc, ms = _rope_tables(positions, _block(t, 512))
    row = lambda a: a.reshape(1, -1).astype(F32)

    x2 = x.reshape(t, d)
    for l in range(depth):
        w_hg, w_ret, w_c, w_g = _in_proj_slices(w_in[l])
        nw = row(norm_mix_w[l])
        ya = _hg_mixer(x2, nw, w_hg, row(lb_all[l]), row(hg_norm_w[l]), b, s, ts)
        yb = _ret_mixer(x2, nw, w_ret, row(ret_norm_w[l]), rc, rs, b, s, ts)
        q, k, v = _mla_pre(x2, nw, w_c, row(mla_q_norm_w[l]), _uq_layout(mla_w_uq[l]),
                           row(mla_kv_norm_w[l]), _ukv_layout(mla_w_ukv[l]), mc, ms, tm)
        yc = _mla_attn(q, k, v, b, s, tq)
        x2 = _merge(x2, nw, w_g, ya, yb, yc, w_br_a[l].astype(BF16), w_br_b[l].astype(BF16),
                    w_br_c[l].astype(BF16), w_out[l].astype(BF16), tm)
        x2 = _ffn(x2, row(norm_ffn_w[l]), w_ffn_in[l].astype(BF16), w_ffn_out[l].astype(BF16),
                  row(final_norm_w), tm, final_norm=(l == depth - 1))
    return x2.reshape(b, s, d)
```

```python
import functools

import numpy as np
import jax
import jax.numpy as jnp
from jax import lax
from jax.experimental import pallas as pl
from jax.experimental.pallas import tpu as pltpu

F32 = jnp.float32
BF16 = jnp.bfloat16

D_MODEL = 1024
HEADS = 4
HG_K = 128
HG_V = 128
RET_K = 128
RET_V = 256
MLA_Q_RANK = 256
MLA_KV_RANK = 128
MLA_NOPE = 128
MLA_ROPE = 64
MLA_V = 128
D_FF = 2816
CHUNK = 64
ROPE_BASE = 10000.0
EPS = 1e-6
EXP_CLIP = 60.0
MASK_NEG = -1e30
LOG2E = 1.4426950408889634

LANES = 128
VMEM_LIMIT = 56 * 1024 * 1024

LEVELS = (32, 16, 8, 4, 2, 1)


def _rms(x, w):
    return x * lax.rsqrt(jnp.mean(x * x, axis=-1, keepdims=True) + EPS) * w


def _sigmoid(x):
    return 0.5 + 0.5 * jnp.tanh(0.5 * x)


def _silu(x):
    return x * _sigmoid(x)


def _dot(a, b):
    return jnp.dot(a, b, preferred_element_type=F32)


def _dot_nt(a, b):
    return lax.dot_general(a, b, (((1,), (1,)), ((), ())), preferred_element_type=F32)


def _dot_tn(a, b):
    return lax.dot_general(a, b, (((0,), (0,)), ((), ())), preferred_element_type=F32)


def _params(sem):
    return pltpu.CompilerParams(dimension_semantics=sem, vmem_limit_bytes=VMEM_LIMIT)


def _rope_table_kernel(pos_ref, inv_ref, rc_ref, rs_ref, mc_ref, ms_ref):
    pos = pos_ref[...].astype(F32)
    ang = inv_ref[...] * pos
    ct = jnp.transpose(jnp.cos(ang))
    st = jnp.transpose(jnp.sin(ang))
    lo = lax.broadcasted_iota(jnp.int32, ct.shape, 1) < 64
    ct_r = pltpu.roll(ct, 64, 1)
    st_r = pltpu.roll(st, 64, 1)
    rc_ref[...] = jnp.where(lo, ct, ct_r)
    rs_ref[...] = jnp.where(lo, -st, st_r)
    mc_ref[...] = jnp.where(lo, ct_r, ct)
    ms_ref[...] = jnp.where(lo, -st_r, st)


def _rope_tables(positions, tt):
    t = positions.size
    pos = positions.reshape(1, t)
    inv64 = ROPE_BASE ** (-jnp.arange(64, dtype=F32) / 64)
    inv32 = ROPE_BASE ** (-jnp.arange(32, dtype=F32) / 32)
    inv = jnp.concatenate([inv64, inv32, inv32]).reshape(LANES, 1)
    tab = jax.ShapeDtypeStruct((t, LANES), F32)
    return pl.pallas_call(
        _rope_table_kernel,
        grid=(t // tt,),
        in_specs=[pl.BlockSpec((1, tt), lambda i: (0, i)),
                  pl.BlockSpec((LANES, 1), lambda i: (0, 0))],
        out_specs=[pl.BlockSpec((tt, LANES), lambda i: (i, 0))] * 4,
        out_shape=[tab] * 4,
        compiler_params=_params(("parallel",)),
        name="rope_tables",
    )(pos, inv)


def _rope(t, c, s):
    return t * c + pltpu.roll(t, 64, 1) * s


def _split3(x):
    a = x.astype(BF16)
    r = x - a.astype(F32)
    b = r.astype(BF16)
    c = (r - b.astype(F32)).astype(BF16)
    return a, b, c


def _level_ref(cum, m):
    w = cum.shape[1]
    if m >= 8:
        g = CHUNK // (2 * m)
        c3 = cum.reshape(g, 2 * m, w)
        return jnp.broadcast_to(c3[:, m - 1:m, :], (g, 2 * m, w)).reshape(CHUNK, w)
    c3 = cum.reshape(8, 8, w)

    def row(r):
        return jnp.broadcast_to(c3[:, r:r + 1, :], (8, 8, w))

    if m == 4:
        out = row(3)
    else:
        sub = lax.broadcasted_iota(jnp.int32, (8, 8, w), 1)
        out = jnp.where(sub < 4, row(1), row(5))
    return out.reshape(CHUNK, w)


def _hg_kernel(x_ref, nw_ref, w_ref, lb_ref, gw_ref, tril_ref, masks_ref, o_ref, p_scr, st_scr, *, ts):
    @pl.when(pl.program_id(1) == 0)
    def _():
        st_scr[...] = jnp.zeros_like(st_scr)

    h = _rms(x_ref[...], nw_ref[...]).astype(BF16)
    p_scr[...] = _dot(h, w_ref[...])

    width = HEADS * HG_K
    pw = 2 * HG_K
    tril = tril_ref[...]
    rows = lax.broadcasted_iota(jnp.int32, (CHUNK, pw), 0)
    upper = {m: (rows % (2 * m)) >= m for m in LEVELS}
    zk = jnp.zeros((CHUNK, HG_K), BF16)

    def blockdiag(t):
        return jnp.concatenate([jnp.concatenate([t[:, :HG_K], zk], axis=1),
                                jnp.concatenate([zk, t[:, HG_K:]], axis=1)], axis=0)

    def chunk(ci, carry):
        r0 = pl.multiple_of(ci * CHUNK, CHUNK)
        for pair in range(HEADS // 2):
            c0 = pair * pw
            lb = lb_ref[:, c0:c0 + pw]
            q = _silu(p_scr[pl.ds(r0, CHUNK), c0:c0 + pw])
            hf = p_scr[pl.ds(r0, CHUNK), width + c0:width + c0 + pw]
            v = p_scr[pl.ds(r0, CHUNK), 2 * width + c0:2 * width + c0 + pw].astype(BF16)
            gate = _silu(p_scr[pl.ds(r0, CHUNK), 3 * width + c0:3 * width + c0 + pw])
            e = jnp.exp2(jnp.abs(hf) * -LOG2E)
            u = jnp.exp2(jnp.minimum(hf * -LOG2E, EXP_CLIP * LOG2E))
            r = 1.0 / (1.0 + e)
            logf = jnp.minimum(hf, 0.0) * LOG2E + jnp.log((1.0 + lb * u) * r) * LOG2E
            k = (1.0 - lb) * (jnp.where(hf >= 0.0, e, 1.0) * r)

            a, b, c = _split3(logf)
            cum = _dot(tril, a) + _dot(tril, b) + _dot(tril, c)
            c_last = cum[CHUNK - 1:CHUNK, :]
            qd = (q * jnp.exp2(cum)).astype(BF16)
            kd = (k * jnp.exp2(c_last - cum)).astype(BF16)
            dec = jnp.exp2(c_last)

            att = masks_ref[0] * _dot_nt(q.astype(BF16), blockdiag(k.astype(BF16)))
            for li, m in enumerate(LEVELS):
                if m == 1:
                    xl = jnp.where(upper[m], q * jnp.exp2(logf), k)
                else:
                    xl = jnp.where(upper[m], q, k) * jnp.exp2(-jnp.abs(cum - _level_ref(cum, m)))
                xl = xl.astype(BF16)
                att = att + masks_ref[li + 1] * _dot_nt(xl, blockdiag(xl))
            o = _dot(att.astype(BF16), blockdiag(v))
            outs = []
            for sub in range(2):
                hs = slice(sub * HG_K, (sub + 1) * HG_K)
                st = st_scr[2 * pair + sub]
                oh = o[:, sub * HG_V:(sub + 1) * HG_V] + _dot_nt(qd[:, hs], st.astype(BF16))
                st_scr[2 * pair + sub] = st * dec[:, hs] + _dot_tn(v[:, hs], kd[:, hs])
                outs.append(oh * lax.rsqrt(jnp.mean(oh * oh, axis=-1, keepdims=True) + EPS))
            y = jnp.concatenate(outs, axis=1) * gw_ref[:, pair * 2 * HG_V:(pair + 1) * 2 * HG_V] * gate
            o_ref[pl.ds(r0, CHUNK), pair * 2 * HG_V:(pair + 1) * 2 * HG_V] = y.astype(o_ref.dtype)
        return carry

    lax.fori_loop(0, ts // CHUNK, chunk, 0, unroll=2)


def _level_masks():
    t = np.arange(CHUNK)[:, None]
    s = np.arange(CHUNK)[None, :]
    masks = [(t == s)]
    for m in LEVELS:
        masks.append((t // (2 * m) == s // (2 * m)) & (t % (2 * m) >= m) & (s % (2 * m) < m))
    return jnp.asarray(np.tile(np.stack(masks).astype(np.float32), (1, 1, 2)))


def _hg_mixer(x2, nw, w, lb, gw, b, s, ts):
    width = HEADS * HG_K
    nblk = s // ts
    tril = jnp.asarray(np.tril(np.ones((CHUNK, CHUNK), np.float32)), BF16)
    masks = _level_masks()
    const = lambda shape: pl.BlockSpec(shape, lambda i, j: (0,) * len(shape))
    return pl.pallas_call(
        functools.partial(_hg_kernel, ts=ts),
        grid=(b, nblk),
        in_specs=[pl.BlockSpec((ts, D_MODEL), lambda i, j: (i * nblk + j, 0)),
                  const((1, D_MODEL)), const((D_MODEL, 4 * width)), const((1, width)), const((1, width)),
                  const((CHUNK, CHUNK)), const(masks.shape)],
        out_specs=pl.BlockSpec((ts, HEADS * HG_V), lambda i, j: (i * nblk + j, 0)),
        out_shape=jax.ShapeDtypeStruct((b * s, HEADS * HG_V), BF16),
        scratch_shapes=[pltpu.VMEM((ts, 4 * width), F32), pltpu.VMEM((HEADS, HG_V, HG_K), F32)],
        compiler_params=_params(("parallel", "arbitrary")),
        name="hgrn2_mixer",
    )(x2, nw, w, lb, gw, tril, masks)


def _ret_kernel(x_ref, nw_ref, w_ref, gw_ref, rc_ref, rs_ref, intra_ref, qdec_ref, kdec_ref, cdec_ref,
                o_ref, p_scr, st_scr, *, ts):
    @pl.when(pl.program_id(1) == 0)
    def _():
        st_scr[...] = jnp.zeros_like(st_scr)

    h = _rms(x_ref[...], nw_ref[...]).astype(BF16)
    p_scr[...] = _dot(h, w_ref[...])

    qk_w = HEADS * RET_K
    v_w = HEADS * RET_V
    gw = gw_ref[...]
    k_scale = RET_K ** -0.5

    def chunk(ci, carry):
        r0 = pl.multiple_of(ci * CHUNK, CHUNK)
        cos = rc_ref[pl.ds(r0, CHUNK), :]
        sin = rs_ref[pl.ds(r0, CHUNK), :]
        gate = _silu(p_scr[pl.ds(r0, CHUNK), 2 * qk_w + v_w:2 * qk_w + 2 * v_w])
        outs = []
        for hd in range(HEADS):
            q = _rope(p_scr[pl.ds(r0, CHUNK), hd * RET_K:(hd + 1) * RET_K], cos, sin)
            k = _rope(p_scr[pl.ds(r0, CHUNK), qk_w + hd * RET_K:qk_w + (hd + 1) * RET_K], cos, sin) * k_scale
            v = p_scr[pl.ds(r0, CHUNK), 2 * qk_w + hd * RET_V:2 * qk_w + (hd + 1) * RET_V].astype(BF16)
            att = _dot_nt(q.astype(BF16), k.astype(BF16)) * intra_ref[hd]
            st = st_scr[hd]
            o = _dot(att.astype(BF16), v) + _dot_nt((q * qdec_ref[hd]).astype(BF16), st.astype(BF16))
            st_scr[hd] = st * cdec_ref[hd] + _dot_tn(v, (k * kdec_ref[hd]).astype(BF16))
            o = o * lax.rsqrt(jnp.mean(o * o, axis=-1, keepdims=True) + EPS)
            outs.append(o)
        y = jnp.concatenate(outs, axis=1) * gw * gate
        o_ref[pl.ds(r0, CHUNK), :] = y.astype(o_ref.dtype)
        return carry

    lax.fori_loop(0, ts // CHUNK, chunk, 0, unroll=2)


def _ret_tables():
    log_gamma = jnp.log1p(-jnp.exp2(-5.0 - jnp.arange(HEADS, dtype=F32)))
    idx = jnp.arange(CHUNK, dtype=F32)
    causal = jnp.tril(jnp.ones((CHUNK, CHUNK), dtype=bool))[None]
    lg = log_gamma[:, None, None]
    dist = jnp.where(causal, (idx[:, None] - idx[None, :])[None], 0.0)
    intra = jnp.where(causal, jnp.exp(dist * lg), 0.0)
    q_decay = jnp.exp((idx + 1.0)[None, :] * log_gamma[:, None])[..., None]
    k_decay = jnp.exp((CHUNK - 1.0 - idx)[None, :] * log_gamma[:, None])[..., None]
    chunk_decay = jnp.exp(CHUNK * log_gamma)[:, None, None]
    return (intra,
            jnp.broadcast_to(q_decay, (HEADS, CHUNK, RET_K)),
            jnp.broadcast_to(k_decay, (HEADS, CHUNK, RET_K)),
            jnp.broadcast_to(chunk_decay, (HEADS, 1, RET_K)))


def _ret_mixer(x2, nw, w, gw, rc, rs, b, s, ts):
    nblk = s // ts
    intra, qdec, kdec, cdec = _ret_tables()
    n_proj = 2 * HEADS * RET_K + 2 * HEADS * RET_V
    const = lambda shape: pl.BlockSpec(shape, lambda i, j: (0,) * len(shape))
    rows = lambda width: pl.BlockSpec((ts, width), lambda i, j: (i * nblk + j, 0))
    return pl.pallas_call(
        functools.partial(_ret_kernel, ts=ts),
        grid=(b, nblk),
        in_specs=[rows(D_MODEL), const((1, D_MODEL)), const((D_MODEL, n_proj)), const((1, HEADS * RET_V)),
                  rows(LANES), rows(LANES),
                  const(intra.shape), const(qdec.shape), const(kdec.shape), const(cdec.shape)],
        out_specs=rows(HEADS * RET_V),
        out_shape=jax.ShapeDtypeStruct((b * s, HEADS * RET_V), BF16),
        scratch_shapes=[pltpu.VMEM((ts, n_proj), F32), pltpu.VMEM((HEADS, RET_V, RET_K), F32)],
        compiler_params=_params(("parallel", "arbitrary")),
        name="retention_mixer",
    )(x2, nw, w, gw, rc, rs, intra, qdec, kdec, cdec)


def _mla_pre_kernel(x_ref, nw_ref, wc_ref, qn_ref, wuq_ref, kvn_ref, wukv_ref, mc_ref, ms_ref,
                    q_ref, k_ref, v_ref):
    h = _rms(x_ref[...], nw_ref[...]).astype(BF16)
    c = _dot(h, wc_ref[...])
    cos = mc_ref[...]
    sin = ms_ref[...]
    cq = _rms(c[:, 0:MLA_Q_RANK], qn_ref[...]).astype(BF16)
    qm = _dot(cq, wuq_ref[...]) * ((MLA_NOPE + MLA_ROPE) ** -0.5 * LOG2E)
    ckv = _rms(c[:, MLA_Q_RANK:MLA_Q_RANK + MLA_KV_RANK], kvn_ref[...]).astype(BF16)
    kv = _dot(ckv, wukv_ref[...])
    k_pe = _rope(c[:, MLA_Q_RANK + MLA_KV_RANK:], cos, sin).astype(BF16)
    lane = lax.broadcasted_iota(jnp.int32, k_pe.shape, 1)
    first = (lane % 64) < 32
    nope_w = HEADS * MLA_NOPE
    for pair in range(HEADS // 2):
        q_pe = _rope(qm[:, nope_w + pair * LANES:nope_w + (pair + 1) * LANES], cos, sin)
        for sub in range(2):
            hd = 2 * pair + sub
            keep = first if sub == 0 else jnp.logical_not(first)
            q_ref[:, hd * 256:hd * 256 + MLA_NOPE] = qm[:, hd * MLA_NOPE:(hd + 1) * MLA_NOPE].astype(BF16)
            q_ref[:, hd * 256 + MLA_NOPE:(hd + 1) * 256] = jnp.where(keep, q_pe, 0.0).astype(BF16)
    for hd in range(HEADS):
        k_ref[:, hd * 256:hd * 256 + MLA_NOPE] = kv[:, hd * MLA_NOPE:(hd + 1) * MLA_NOPE].astype(BF16)
        k_ref[:, hd * 256 + MLA_NOPE:(hd + 1) * 256] = k_pe
    v_ref[...] = kv[:, nope_w:].astype(BF16)


def _mla_pre(x2, nw, wc, qn, wuq, kvn, wukv, mc, ms, tm):
    t = x2.shape[0]
    const = lambda shape: pl.BlockSpec(shape, lambda i: (0,) * len(shape))
    rows = lambda width: pl.BlockSpec((tm, width), lambda i: (i, 0))
    return pl.pallas_call(
        _mla_pre_kernel,
        grid=(t // tm,),
        in_specs=[rows(D_MODEL), const((1, D_MODEL)), const(wc.shape), const((1, MLA_Q_RANK)), const(wuq.shape),
                  const((1, MLA_KV_RANK)), const(wukv.shape), rows(LANES), rows(LANES)],
        out_specs=[rows(HEADS * 256), rows(HEADS * 256), rows(HEADS * MLA_V)],
        out_shape=[jax.ShapeDtypeStruct((t, HEADS * 256), BF16), jax.ShapeDtypeStruct((t, HEADS * 256), BF16),
                   jax.ShapeDtypeStruct((t, HEADS * MLA_V), BF16)],
        compiler_params=_params(("parallel",)),
        name="mla_projections",
    )(x2, nw, wc, qn, wuq, kvn, wukv, mc, ms)


def _mla_attn_kernel(q_ref, k_ref, v_ref, o_ref, acc_scr, *, tq):
    qi = pl.program_id(1)
    key = lax.broadcasted_iota(jnp.int32, (tq, tq), 0)
    qry = lax.broadcasted_iota(jnp.int32, (tq, tq), 1)
    causal = key <= qry
    acc_scr[...] = jnp.zeros_like(acc_scr)

    def step(kj, carry, masked):
        ms, ls = carry
        k0 = pl.multiple_of(kj * tq, tq)
        new_m, new_l = [], []
        for hd in range(HEADS):
            s_t = _dot_nt(k_ref[pl.ds(k0, tq), hd * 256:(hd + 1) * 256], q_ref[:, hd * 256:(hd + 1) * 256])
            if masked:
                s_t = jnp.where(causal, s_t, MASK_NEG)
            m_new = jnp.maximum(ms[hd], jnp.max(s_t, axis=0, keepdims=True))
            alpha = jnp.exp2(ms[hd] - m_new)
            p = jnp.exp2(s_t - m_new)
            new_m.append(m_new)
            new_l.append(ls[hd] * alpha + jnp.sum(p, axis=0, keepdims=True))
            pv = _dot_tn(v_ref[pl.ds(k0, tq), hd * MLA_V:(hd + 1) * MLA_V], p.astype(BF16))
            acc_scr[hd] = acc_scr[hd] * alpha + pv
        return tuple(new_m), tuple(new_l)

    init = (tuple(jnp.full((1, tq), MASK_NEG, F32) for _ in range(HEADS)),
            tuple(jnp.zeros((1, tq), F32) for _ in range(HEADS)))
    carry = lax.fori_loop(0, qi, functools.partial(step, masked=False), init)
    _, ls = step(qi, carry, True)
    for hd in range(HEADS):
        o_ref[:, hd * MLA_V:(hd + 1) * MLA_V] = jnp.transpose(acc_scr[hd] / ls[hd]).astype(o_ref.dtype)


def _mla_attn(q, k, v, b, s, tq):
    nq = s // tq
    return pl.pallas_call(
        functools.partial(_mla_attn_kernel, tq=tq),
        grid=(b, nq),
        in_specs=[pl.BlockSpec((tq, HEADS * 256), lambda i, j: (i * nq + j, 0)),
                  pl.BlockSpec((s, HEADS * 256), lambda i, j: (i, 0)),
                  pl.BlockSpec((s, HEADS * MLA_V), lambda i, j: (i, 0))],
        out_specs=pl.BlockSpec((tq, HEADS * MLA_V), lambda i, j: (i * nq + j, 0)),
        out_shape=jax.ShapeDtypeStruct((b * s, HEADS * MLA_V), BF16),
        scratch_shapes=[pltpu.VMEM((HEADS, MLA_V, tq), F32)],
        compiler_params=_params(("parallel", "arbitrary")),
        name="mla_attention",
    )(q, k, v)


def _merge_kernel(x_ref, nw_ref, wg_ref, ya_ref, yb_ref, yc_ref, wa_ref, wb_ref, wc_ref, wo_ref, o_ref):
    x = x_ref[...]
    h = _rms(x, nw_ref[...]).astype(BF16)
    merged = (_sigmoid(_dot(h, wg_ref[:, 0:D_MODEL])) * _dot(ya_ref[...], wa_ref[...])
              + _sigmoid(_dot(h, wg_ref[:, D_MODEL:2 * D_MODEL])) * _dot(yb_ref[...], wb_ref[...])
              + _sigmoid(_dot(h, wg_ref[:, 2 * D_MODEL:3 * D_MODEL])) * _dot(yc_ref[...], wc_ref[...]))
    o_ref[...] = x + _dot(merged.astype(BF16), wo_ref[...])


def _merge(x2, nw, wg, ya, yb, yc, wa, wb, wc, wo, tm):
    t = x2.shape[0]
    const = lambda a: pl.BlockSpec(a.shape, lambda i: (0,) * a.ndim)
    rows = lambda a: pl.BlockSpec((tm, a.shape[1]), lambda i: (i, 0))
    return pl.pallas_call(
        _merge_kernel,
        grid=(t // tm,),
        in_specs=[rows(x2), const(nw), const(wg), rows(ya), rows(yb), rows(yc),
                  const(wa), const(wb), const(wc), const(wo)],
        out_specs=rows(x2),
        out_shape=jax.ShapeDtypeStruct(x2.shape, F32),
        compiler_params=_params(("parallel",)),
        name="gated_merge",
    )(x2, nw, wg, ya, yb, yc, wa, wb, wc, wo)


def _ffn_kernel(x_ref, nw_ref, wi_ref, wo_ref, fw_ref, o_ref, *, final_norm):
    x = x_ref[...]
    h = _rms(x, nw_ref[...]).astype(BF16)
    g = _dot(h, wi_ref[:, 0:D_FF])
    u = _dot(h, wi_ref[:, D_FF:2 * D_FF])
    y = x + _dot((_silu(g) * u).astype(BF16), wo_ref[...])
    if final_norm:
        y = _rms(y, fw_ref[...])
    o_ref[...] = y


def _ffn(x2, nw, wi, wo, fw, tm, final_norm):
    t = x2.shape[0]
    const = lambda a: pl.BlockSpec(a.shape, lambda i: (0,) * a.ndim)
    rows = lambda a: pl.BlockSpec((tm, a.shape[1]), lambda i: (i, 0))
    return pl.pallas_call(
        functools.partial(_ffn_kernel, final_norm=final_norm),
        grid=(t // tm,),
        in_specs=[rows(x2), const(nw), const(wi), const(wo), const(fw)],
        out_specs=rows(x2),
        out_shape=jax.ShapeDtypeStruct(x2.shape, F32),
        compiler_params=_params(("parallel",)),
        name="swiglu_ffn",
    )(x2, nw, wi, wo, fw)


def _in_proj_slices(w_in_l):
    widths = (512, 512, 512, 512, 512, 512, 1024, 1024, MLA_Q_RANK, MLA_KV_RANK, MLA_ROPE, 3 * D_MODEL)
    offs = np.concatenate([[0], np.cumsum(widths)])
    cols = [w_in_l[:, int(offs[i]):int(offs[i + 1])] for i in range(len(widths))]
    w_hg = jnp.concatenate(cols[0:4], axis=1)
    w_ret = jnp.concatenate(cols[4:8], axis=1)
    kr = cols[10]
    kr1, kr2 = kr[:, :32], kr[:, 32:]
    w_c = jnp.concatenate([cols[8], cols[9], kr1, kr1, kr2, kr2], axis=1)
    return w_hg.astype(BF16), w_ret.astype(BF16), w_c.astype(BF16), cols[11].astype(BF16)


def _uq_layout(w_uq_l):
    per = MLA_NOPE + MLA_ROPE
    heads = [w_uq_l[:, hd * per:(hd + 1) * per] for hd in range(HEADS)]
    nope = [hh[:, :MLA_NOPE] for hh in heads]
    pe = [hh[:, MLA_NOPE:] for hh in heads]
    pairs = []
    for p in range(HEADS // 2):
        a, b = pe[2 * p], pe[2 * p + 1]
        pairs += [a[:, :32], b[:, :32], a[:, 32:], b[:, 32:]]
    return jnp.concatenate(nope + pairs, axis=1).astype(BF16)


def _ukv_layout(w_ukv_l):
    per = MLA_NOPE + MLA_V
    heads = [w_ukv_l[:, hd * per:(hd + 1) * per] for hd in range(HEADS)]
    return jnp.concatenate([hh[:, :MLA_NOPE] for hh in heads] + [hh[:, MLA_NOPE:] for hh in heads],
                           axis=1).astype(BF16)


def _block(n, want):
    return want if n % want == 0 else n


def kernel(x, positions, norm_mix_w, w_in, hg_lower_bounds, hg_norm_w, ret_norm_w, mla_q_norm_w, mla_w_uq, mla_kv_norm_w, mla_w_ukv, w_br_a, w_br_b, w_br_c, w_out, norm_ffn_w, w_ffn_in, w_ffn_out, final_norm_w):
    b, s, d = x.shape
    depth = w_in.shape[0]
    assert d == D_MODEL and s % CHUNK == 0
    t = b * s
    ts = _block(s, 512)
    tm = _block(t, 512)
    tq = _block(s, 512)

    lb_p = jax.nn.softmax(hg_lower_bounds.astype(F32), axis=0)
    lb_all = jnp.cumsum(lb_p, axis=0) - lb_p[0]

    rc, rs, mc, ms = _rope_tables(positions, _block(t, 512))
    row = lambda a: a.reshape(1, -1).astype(F32)

    x2 = x.reshape(t, d)
    for l in range(depth):
        w_hg, w_ret, w_c, w_g = _in_proj_slices(w_in[l])
        nw = row(norm_mix_w[l])
        ya = _hg_mixer(x2, nw, w_hg, row(lb_all[l]), row(hg_norm_w[l]), b, s, ts)
        yb = _ret_mixer(x2, nw, w_ret, row(ret_norm_w[l]), rc, rs, b, s, ts)
        q, k, v = _mla_pre(x2, nw, w_c, row(mla_q_norm_w[l]), _uq_layout(mla_w_uq[l]),
                           row(mla_kv_norm_w[l]), _ukv_layout(mla_w_ukv[l]), mc, ms, tm)
        yc = _mla_attn(q, k, v, b, s, tq)
        x2 = _merge(x2, nw, w_g, ya, yb, yc, w_br_a[l].astype(BF16), w_br_b[l].astype(BF16),
                    w_br_c[l].astype(BF16), w_out[l].astype(BF16), tm)
        x2 = _ffn(x2, row(norm_ffn_w[l]), w_ffn_in[l].astype(BF16), w_ffn_out[l].astype(BF16),
                  row(final_norm_w), tm, final_norm=(l == depth - 1))
    return x2.reshape(b, s, d)
```

```python
import functools

import numpy as np
import jax
import jax.numpy as jnp
from jax import lax
from jax.experimental import pallas as pl
from jax.experimental.pallas import tpu as pltpu

F32 = jnp.float32
BF16 = jnp.bfloat16

D_MODEL = 1024
HEADS = 4
HG_K = 128
HG_V = 128
RET_K = 128
RET_V = 256
MLA_Q_RANK = 256
MLA_KV_RANK = 128
MLA_NOPE = 128
MLA_ROPE = 64
MLA_V = 128
D_FF = 2816
CHUNK = 64
ROPE_BASE = 10000.0
EPS = 1e-6
EXP_CLIP = 60.0
MASK_NEG = -1e30
LOG2E = 1.4426950408889634

LANES = 128
VMEM_LIMIT = 56 * 1024 * 1024

LEVELS = (32, 16, 8, 4, 2, 1)


def _rms(x, w):
    return x * lax.rsqrt(jnp.mean(x * x, axis=-1, keepdims=True) + EPS) * w


def _sigmoid(x):
    return 0.5 + 0.5 * jnp.tanh(0.5 * x)


def _silu(x):
    return x * _sigmoid(x)


def _dot(a, b):
    return jnp.dot(a, b, preferred_element_type=F32)


def _dot_nt(a, b):
    return lax.dot_general(a, b, (((1,), (1,)), ((), ())), preferred_element_type=F32)


def _dot_tn(a, b):
    return lax.dot_general(a, b, (((0,), (0,)), ((), ())), preferred_element_type=F32)


def _params(sem):
    return pltpu.CompilerParams(dimension_semantics=sem, vmem_limit_bytes=VMEM_LIMIT)


def _rope_table_kernel(pos_ref, inv_ref, rc_ref, rs_ref, mc_ref, ms_ref):
    pos = pos_ref[...].astype(F32)
    ang = inv_ref[...] * pos
    ct = jnp.transpose(jnp.cos(ang))
    st = jnp.transpose(jnp.sin(ang))
    lo = lax.broadcasted_iota(jnp.int32, ct.shape, 1) < 64
    ct_r = pltpu.roll(ct, 64, 1)
    st_r = pltpu.roll(st, 64, 1)
    rc_ref[...] = jnp.where(lo, ct, ct_r)
    rs_ref[...] = jnp.where(lo, -st, st_r)
    mc_ref[...] = jnp.where(lo, ct_r, ct)
    ms_ref[...] = jnp.where(lo, -st_r, st)


def _rope_tables(positions, tt):
    t = positions.size
    pos = positions.reshape(1, t)
    inv64 = ROPE_BASE ** (-jnp.arange(64, dtype=F32) / 64)
    inv32 = ROPE_BASE ** (-jnp.arange(32, dtype=F32) / 32)
    inv = jnp.concatenate([inv64, inv32, inv32]).reshape(LANES, 1)
    tab = jax.ShapeDtypeStruct((t, LANES), F32)
    return pl.pallas_call(
        _rope_table_kernel,
        grid=(t // tt,),
        in_specs=[pl.BlockSpec((1, tt), lambda i: (0, i)),
                  pl.BlockSpec((LANES, 1), lambda i: (0, 0))],
        out_specs=[pl.BlockSpec((tt, LANES), lambda i: (i, 0))] * 4,
        out_shape=[tab] * 4,
        compiler_params=_params(("parallel",)),
        name="rope_tables",
    )(pos, inv)


def _rope(t, c, s):
    return t * c + pltpu.roll(t, 64, 1) * s


def _split3(x):
    a = x.astype(BF16)
    r = x - a.astype(F32)
    b = r.astype(BF16)
    c = (r - b.astype(F32)).astype(BF16)
    return a, b, c


def _level_ref(cum, m):
    w = cum.shape[1]
    if m >= 8:
        g = CHUNK // (2 * m)
        c3 = cum.reshape(g, 2 * m, w)
        return jnp.broadcast_to(c3[:, m - 1:m, :], (g, 2 * m, w)).reshape(CHUNK, w)
    c3 = cum.reshape(8, 8, w)

    def row(r):
        return jnp.broadcast_to(c3[:, r:r + 1, :], (8, 8, w))

    if m == 4:
        out = row(3)
    else:
        sub = lax.broadcasted_iota(jnp.int32, (8, 8, w), 1)
        out = jnp.where(sub < 4, row(1), row(5))
    return out.reshape(CHUNK, w)


def _hg_kernel(x_ref, nw_ref, w_ref, lb_ref, gw_ref, tril_ref, masks_ref, o_ref, p_scr, st_scr, *, ts):
    @pl.when(pl.program_id(1) == 0)
    def _():
        st_scr[...] = jnp.zeros_like(st_scr)

    h = _rms(x_ref[...], nw_ref[...]).astype(BF16)
    p_scr[...] = _dot(h, w_ref[...])

    width = HEADS * HG_K
    pw = 2 * HG_K
    tril = tril_ref[...]
    rows = lax.broadcasted_iota(jnp.int32, (CHUNK, pw), 0)
    upper = {m: (rows % (2 * m)) >= m for m in LEVELS}
    zk = jnp.zeros((CHUNK, HG_K), BF16)

    def blockdiag(t):
        return jnp.concatenate([jnp.concatenate([t[:, :HG_K], zk], axis=1),
                                jnp.concatenate([zk, t[:, HG_K:]], axis=1)], axis=0)


    def gates(r0, pair):
        c0 = pair * pw
        lb = lb_ref[:, c0:c0 + pw]
        q = _silu(p_scr[pl.ds(r0, CHUNK), c0:c0 + pw])
        hf = p_scr[pl.ds(r0, CHUNK), width + c0:width + c0 + pw]
        e = jnp.exp2(jnp.abs(hf) * -LOG2E)
        u = jnp.exp2(jnp.minimum(hf * -LOG2E, EXP_CLIP * LOG2E))
        r = 1.0 / (1.0 + e)
        logf = (jnp.minimum(hf, 0.0) + jnp.log((1.0 + lb * u) * r)) * LOG2E
        k = (1.0 - lb) * (jnp.where(hf >= 0.0, e, 1.0) * r)
        a, b, c = _split3(logf)
        cum = _dot(tril, a) + _dot(tril, b) + _dot(tril, c)
        return q, k, logf, cum

    def level_dots(q, k, logf, cum):
        c_last = cum[CHUNK - 1:CHUNK, :]
        qd = (q * jnp.exp2(cum)).astype(BF16)
        kd = (k * jnp.exp2(c_last - cum)).astype(BF16)
        dec = jnp.exp2(c_last)
        prods = [_dot_nt(q.astype(BF16), blockdiag(k.astype(BF16)))]
        for m in LEVELS:
            if m == 1:
                xl = jnp.where(upper[m], q * jnp.exp2(logf), k)
            else:
                xl = jnp.where(upper[m], q, k) * jnp.exp2(-jnp.abs(cum - _level_ref(cum, m)))
            xl = xl.astype(BF16)
            prods.append(_dot_nt(xl, blockdiag(xl)))
        return qd, kd, dec, prods

    def finish(r0, pair, qd, kd, dec, prods):
        c0 = pair * pw
        v = p_scr[pl.ds(r0, CHUNK), 2 * width + c0:2 * width + c0 + pw].astype(BF16)
        gate = _silu(p_scr[pl.ds(r0, CHUNK), 3 * width + c0:3 * width + c0 + pw])
        att = masks_ref[0] * prods[0]
        for li in range(1, len(prods)):
            att = att + masks_ref[li] * prods[li]
        o = _dot(att.astype(BF16), blockdiag(v))
        outs = []
        for sub in range(2):
            hs = slice(sub * HG_K, (sub + 1) * HG_K)
            st = st_scr[2 * pair + sub]
            oh = o[:, sub * HG_V:(sub + 1) * HG_V] + _dot_nt(qd[:, hs], st.astype(BF16))
            st_scr[2 * pair + sub] = st * dec[:, hs] + _dot_tn(v[:, hs], kd[:, hs])
            outs.append(oh * lax.rsqrt(jnp.mean(oh * oh, axis=-1, keepdims=True) + EPS))
        y = jnp.concatenate(outs, axis=1) * gw_ref[:, c0:c0 + pw] * gate
        o_ref[pl.ds(r0, CHUNK), c0:c0 + pw] = y.astype(o_ref.dtype)

    chunks_per_iter = 4

    def body(it, carry):
        units = [(pl.multiple_of((it * chunks_per_iter + ci) * CHUNK, CHUNK), pair)
                 for ci in range(chunks_per_iter) for pair in range(HEADS // 2)]
        stage1 = [gates(r0, pair) for r0, pair in units]
        stage2 = [level_dots(*vals) for vals in stage1]
        for (r0, pair), vals in zip(units, stage2):
            finish(r0, pair, *vals)
        return carry

    lax.fori_loop(0, ts // (CHUNK * chunks_per_iter), body, 0)


def _level_masks():
    t = np.arange(CHUNK)[:, None]
    s = np.arange(CHUNK)[None, :]
    masks = [(t == s)]
    for m in LEVELS:
        masks.append((t // (2 * m) == s // (2 * m)) & (t % (2 * m) >= m) & (s % (2 * m) < m))
    return jnp.asarray(np.tile(np.stack(masks).astype(np.float32), (1, 1, 2)))


def _hg_mixer(x2, nw, w, lb, gw, b, s, ts):
    width = HEADS * HG_K
    nblk = s // ts
    tril = jnp.asarray(np.tril(np.ones((CHUNK, CHUNK), np.float32)), BF16)
    masks = _level_masks()
    const = lambda shape: pl.BlockSpec(shape, lambda i, j: (0,) * len(shape))
    return pl.pallas_call(
        functools.partial(_hg_kernel, ts=ts),
        grid=(b, nblk),
        in_specs=[pl.BlockSpec((ts, D_MODEL), lambda i, j: (i * nblk + j, 0)),
                  const((1, D_MODEL)), const((D_MODEL, 4 * width)), const((1, width)), const((1, width)),
                  const((CHUNK, CHUNK)), const(masks.shape)],
        out_specs=pl.BlockSpec((ts, HEADS * HG_V), lambda i, j: (i * nblk + j, 0)),
        out_shape=jax.ShapeDtypeStruct((b * s, HEADS * HG_V), BF16),
        scratch_shapes=[pltpu.VMEM((ts, 4 * width), F32), pltpu.VMEM((HEADS, HG_V, HG_K), F32)],
        compiler_params=_params(("parallel", "arbitrary")),
        name="hgrn2_mixer",
    )(x2, nw, w, lb, gw, tril, masks)


def _ret_kernel(x_ref, nw_ref, w_ref, gw_ref, rc_ref, rs_ref, intra_ref, qdec_ref, kdec_ref, cdec_ref,
                o_ref, p_scr, st_scr, *, ts):
    @pl.when(pl.program_id(1) == 0)
    def _():
        st_scr[...] = jnp.zeros_like(st_scr)

    h = _rms(x_ref[...], nw_ref[...]).astype(BF16)
    p_scr[...] = _dot(h, w_ref[...])

    qk_w = HEADS * RET_K
    v_w = HEADS * RET_V
    k_scale = RET_K ** -0.5


    def scores(r0, hd):
        cos = rc_ref[pl.ds(r0, CHUNK), :]
        sin = rs_ref[pl.ds(r0, CHUNK), :]
        q = _rope(p_scr[pl.ds(r0, CHUNK), hd * RET_K:(hd + 1) * RET_K], cos, sin)
        k = _rope(p_scr[pl.ds(r0, CHUNK), qk_w + hd * RET_K:qk_w + (hd + 1) * RET_K], cos, sin) * k_scale
        return q, k, _dot_nt(q.astype(BF16), k.astype(BF16))

    def finish(r0, hd, q, k, qk):
        v = p_scr[pl.ds(r0, CHUNK), 2 * qk_w + hd * RET_V:2 * qk_w + (hd + 1) * RET_V].astype(BF16)
        gate = _silu(p_scr[pl.ds(r0, CHUNK), 2 * qk_w + v_w + hd * RET_V:2 * qk_w + v_w + (hd + 1) * RET_V])
        att = qk * intra_ref[hd]
        st = st_scr[hd]
        o = _dot(att.astype(BF16), v) + _dot_nt((q * qdec_ref[hd]).astype(BF16), st.astype(BF16))
        st_scr[hd] = st * cdec_ref[hd] + _dot_tn(v, (k * kdec_ref[hd]).astype(BF16))
        o = o * lax.rsqrt(jnp.mean(o * o, axis=-1, keepdims=True) + EPS)
        y = o * gw_ref[:, hd * RET_V:(hd + 1) * RET_V] * gate
        o_ref[pl.ds(r0, CHUNK), hd * RET_V:(hd + 1) * RET_V] = y.astype(o_ref.dtype)

    chunks_per_iter = 4

    def body(it, carry):
        units = [(pl.multiple_of((it * chunks_per_iter + ci) * CHUNK, CHUNK), hd)
                 for ci in range(chunks_per_iter) for hd in range(HEADS)]
        stage1 = [scores(r0, hd) for r0, hd in units]
        for (r0, hd), vals in zip(units, stage1):
            finish(r0, hd, *vals)
        return carry

    lax.fori_loop(0, ts // (CHUNK * chunks_per_iter), body, 0)


def _ret_tables():
    log_gamma = jnp.log1p(-jnp.exp2(-5.0 - jnp.arange(HEADS, dtype=F32)))
    idx = jnp.arange(CHUNK, dtype=F32)
    causal = jnp.tril(jnp.ones((CHUNK, CHUNK), dtype=bool))[None]
    lg = log_gamma[:, None, None]
    dist = jnp.where(causal, (idx[:, None] - idx[None, :])[None], 0.0)
    intra = jnp.where(causal, jnp.exp(dist * lg), 0.0)
    q_decay = jnp.exp((idx + 1.0)[None, :] * log_gamma[:, None])[..., None]
    k_decay = jnp.exp((CHUNK - 1.0 - idx)[None, :] * log_gamma[:, None])[..., None]
    chunk_decay = jnp.exp(CHUNK * log_gamma)[:, None, None]
    return (intra,
            jnp.broadcast_to(q_decay, (HEADS, CHUNK, RET_K)),
            jnp.broadcast_to(k_decay, (HEADS, CHUNK, RET_K)),
            jnp.broadcast_to(chunk_decay, (HEADS, 1, RET_K)))


def _ret_mixer(x2, nw, w, gw, rc, rs, b, s, ts):
    nblk = s // ts
    intra, qdec, kdec, cdec = _ret_tables()
    n_proj = 2 * HEADS * RET_K + 2 * HEADS * RET_V
    const = lambda shape: pl.BlockSpec(shape, lambda i, j: (0,) * len(shape))
    rows = lambda width: pl.BlockSpec((ts, width), lambda i, j: (i * nblk + j, 0))
    return pl.pallas_call(
        functools.partial(_ret_kernel, ts=ts),
        grid=(b, nblk),
        in_specs=[rows(D_MODEL), const((1, D_MODEL)), const((D_MODEL, n_proj)), const((1, HEADS * RET_V)),
                  rows(LANES), rows(LANES),
                  const(intra.shape), const(qdec.shape), const(kdec.shape), const(cdec.shape)],
        out_specs=rows(HEADS * RET_V),
        out_shape=jax.ShapeDtypeStruct((b * s, HEADS * RET_V), BF16),
        scratch_shapes=[pltpu.VMEM((ts, n_proj), F32), pltpu.VMEM((HEADS, RET_V, RET_K), F32)],
        compiler_params=_params(("parallel", "arbitrary")),
        name="retention_mixer",
    )(x2, nw, w, gw, rc, rs, intra, qdec, kdec, cdec)


def _mla_pre_kernel(x_ref, nw_ref, wc_ref, qn_ref, wuq_ref, kvn_ref, wukv_ref, mc_ref, ms_ref,
                    q_ref, k_ref, v_ref):
    h = _rms(x_ref[...], nw_ref[...]).astype(BF16)
    c = _dot(h, wc_ref[...])
    cos = mc_ref[...]
    sin = ms_ref[...]
    cq = _rms(c[:, 0:MLA_Q_RANK], qn_ref[...]).astype(BF16)
    qm = _dot(cq, wuq_ref[...]) * ((MLA_NOPE + MLA_ROPE) ** -0.5 * LOG2E)
    ckv = _rms(c[:, MLA_Q_RANK:MLA_Q_RANK + MLA_KV_RANK], kvn_ref[...]).astype(BF16)
    kv = _dot(ckv, wukv_ref[...])
    k_pe = _rope(c[:, MLA_Q_RANK + MLA_KV_RANK:], cos, sin).astype(BF16)
    lane = lax.broadcasted_iota(jnp.int32, k_pe.shape, 1)
    first = (lane % 64) < 32
    nope_w = HEADS * MLA_NOPE
    for pair in range(HEADS // 2):
        q_pe = _rope(qm[:, nope_w + pair * LANES:nope_w + (pair + 1) * LANES], cos, sin)
        for sub in range(2):
            hd = 2 * pair + sub
            keep = first if sub == 0 else jnp.logical_not(first)
            q_ref[:, hd * 256:hd * 256 + MLA_NOPE] = qm[:, hd * MLA_NOPE:(hd + 1) * MLA_NOPE].astype(BF16)
            q_ref[:, hd * 256 + MLA_NOPE:(hd + 1) * 256] = jnp.where(keep, q_pe, 0.0).astype(BF16)
    for hd in range(HEADS):
        k_ref[:, hd * 256:hd * 256 + MLA_NOPE] = kv[:, hd * MLA_NOPE:(hd + 1) * MLA_NOPE].astype(BF16)
        k_ref[:, hd * 256 + MLA_NOPE:(hd + 1) * 256] = k_pe
    v_ref[...] = kv[:, nope_w:].astype(BF16)


def _mla_pre(x2, nw, wc, qn, wuq, kvn, wukv, mc, ms, tm):
    t = x2.shape[0]
    const = lambda shape: pl.BlockSpec(shape, lambda i: (0,) * len(shape))
    rows = lambda width: pl.BlockSpec((tm, width), lambda i: (i, 0))
    return pl.pallas_call(
        _mla_pre_kernel,
        grid=(t // tm,),
        in_specs=[rows(D_MODEL), const((1, D_MODEL)), const(wc.shape), const((1, MLA_Q_RANK)), const(wuq.shape),
                  const((1, MLA_KV_RANK)), const(wukv.shape), rows(LANES), rows(LANES)],
        out_specs=[rows(HEADS * 256), rows(HEADS * 256), rows(HEADS * MLA_V)],
        out_shape=[jax.ShapeDtypeStruct((t, HEADS * 256), BF16), jax.ShapeDtypeStruct((t, HEADS * 256), BF16),
                   jax.ShapeDtypeStruct((t, HEADS * MLA_V), BF16)],
        compiler_params=_params(("parallel",)),
        name="mla_projections",
    )(x2, nw, wc, qn, wuq, kvn, wukv, mc, ms)


def _mla_attn_kernel(q_ref, k_ref, v_ref, o_ref, acc_scr, *, tq):
    qi = pl.program_id(1)
    key = lax.broadcasted_iota(jnp.int32, (tq, tq), 0)
    qry = lax.broadcasted_iota(jnp.int32, (tq, tq), 1)
    causal = key <= qry
    acc_scr[...] = jnp.zeros_like(acc_scr)

    def step(kj, carry, masked):
        ms, ls = carry
        k0 = pl.multiple_of(kj * tq, tq)
        new_m, new_l = [], []
        scores = [_dot_nt(k_ref[pl.ds(k0, tq), hd * 256:(hd + 1) * 256], q_ref[:, hd * 256:(hd + 1) * 256])
                  for hd in range(HEADS)]
        for hd in range(HEADS):
            s_t = scores[hd]
            if masked:
                s_t = jnp.where(causal, s_t, MASK_NEG)
            m_new = jnp.maximum(ms[hd], jnp.max(s_t, axis=0, keepdims=True))
            alpha = jnp.exp2(ms[hd] - m_new)
            p = jnp.exp2(s_t - m_new)
            new_m.append(m_new)
            new_l.append(ls[hd] * alpha + jnp.sum(p, axis=0, keepdims=True))
            pv = _dot_tn(v_ref[pl.ds(k0, tq), hd * MLA_V:(hd + 1) * MLA_V], p.astype(BF16))
            acc_scr[hd] = acc_scr[hd] * alpha + pv
        return tuple(new_m), tuple(new_l)

    init = (tuple(jnp.full((1, tq), MASK_NEG, F32) for _ in range(HEADS)),
            tuple(jnp.zeros((1, tq), F32) for _ in range(HEADS)))
    carry = lax.fori_loop(0, qi, functools.partial(step, masked=False), init)
    _, ls = step(qi, carry, True)
    for hd in range(HEADS):
        o_ref[:, hd * MLA_V:(hd + 1) * MLA_V] = jnp.transpose(acc_scr[hd] / ls[hd]).astype(o_ref.dtype)


def _mla_attn(q, k, v, b, s, tq):
    nq = s // tq
    return pl.pallas_call(
        functools.partial(_mla_attn_kernel, tq=tq),
        grid=(b, nq),
        in_specs=[pl.BlockSpec((tq, HEADS * 256), lambda i, j: (i * nq + j, 0)),
                  pl.BlockSpec((s, HEADS * 256), lambda i, j: (i, 0)),
                  pl.BlockSpec((s, HEADS * MLA_V), lambda i, j: (i, 0))],
        out_specs=pl.BlockSpec((tq, HEADS * MLA_V), lambda i, j: (i * nq + j, 0)),
        out_shape=jax.ShapeDtypeStruct((b * s, HEADS * MLA_V), BF16),
        scratch_shapes=[pltpu.VMEM((HEADS, MLA_V, tq), F32)],
        compiler_params=_params(("parallel", "arbitrary")),
        name="mla_attention",
    )(q, k, v)


def _merge_kernel(x_ref, nw_ref, wg_ref, ya_ref, yb_ref, yc_ref, wa_ref, wb_ref, wc_ref, wo_ref, o_ref):
    x = x_ref[...]
    h = _rms(x, nw_ref[...]).astype(BF16)
    merged = (_sigmoid(_dot(h, wg_ref[:, 0:D_MODEL])) * _dot(ya_ref[...], wa_ref[...])
              + _sigmoid(_dot(h, wg_ref[:, D_MODEL:2 * D_MODEL])) * _dot(yb_ref[...], wb_ref[...])
              + _sigmoid(_dot(h, wg_ref[:, 2 * D_MODEL:3 * D_MODEL])) * _dot(yc_ref[...], wc_ref[...]))
    o_ref[...] = x + _dot(merged.astype(BF16), wo_ref[...])


def _merge(x2, nw, wg, ya, yb, yc, wa, wb, wc, wo, tm):
    t = x2.shape[0]
    const = lambda a: pl.BlockSpec(a.shape, lambda i: (0,) * a.ndim)
    rows = lambda a: pl.BlockSpec((tm, a.shape[1]), lambda i: (i, 0))
    return pl.pallas_call(
        _merge_kernel,
        grid=(t // tm,),
        in_specs=[rows(x2), const(nw), const(wg), rows(ya), rows(yb), rows(yc),
                  const(wa), const(wb), const(wc), const(wo)],
        out_specs=rows(x2),
        out_shape=jax.ShapeDtypeStruct(x2.shape, F32),
        compiler_params=_params(("parallel",)),
        name="gated_merge",
    )(x2, nw, wg, ya, yb, yc, wa, wb, wc, wo)


def _ffn_kernel(x_ref, nw_ref, wi_ref, wo_ref, fw_ref, o_ref, *, final_norm):
    x = x_ref[...]
    h = _rms(x, nw_ref[...]).astype(BF16)
    g = _dot(h, wi_ref[:, 0:D_FF])
    u = _dot(h, wi_ref[:, D_FF:2 * D_FF])
    y = x + _dot((_silu(g) * u).astype(BF16), wo_ref[...])
    if final_norm:
        y = _rms(y, fw_ref[...])
    o_ref[...] = y


def _ffn(x2, nw, wi, wo, fw, tm, final_norm):
    t = x2.shape[0]
    const = lambda a: pl.BlockSpec(a.shape, lambda i: (0,) * a.ndim)
    rows = lambda a: pl.BlockSpec((tm, a.shape[1]), lambda i: (i, 0))
    return pl.pallas_call(
        functools.partial(_ffn_kernel, final_norm=final_norm),
        grid=(t // tm,),
        in_specs=[rows(x2), const(nw), const(wi), const(wo), const(fw)],
        out_specs=rows(x2),
        out_shape=jax.ShapeDtypeStruct(x2.shape, F32),
        compiler_params=_params(("parallel",)),
        name="swiglu_ffn",
    )(x2, nw, wi, wo, fw)


def _in_proj_slices(w_in_l):
    widths = (512, 512, 512, 512, 512, 512, 1024, 1024, MLA_Q_RANK, MLA_KV_RANK, MLA_ROPE, 3 * D_MODEL)
    offs = np.concatenate([[0], np.cumsum(widths)])
    cols = [w_in_l[:, int(offs[i]):int(offs[i + 1])] for i in range(len(widths))]
    w_hg = jnp.concatenate(cols[0:4], axis=1)
    w_ret = jnp.concatenate(cols[4:8], axis=1)
    kr = cols[10]
    kr1, kr2 = kr[:, :32], kr[:, 32:]
    w_c = jnp.concatenate([cols[8], cols[9], kr1, kr1, kr2, kr2], axis=1)
    return w_hg.astype(BF16), w_ret.astype(BF16), w_c.astype(BF16), cols[11].astype(BF16)


def _uq_layout(w_uq_l):
    per = MLA_NOPE + MLA_ROPE
    heads = [w_uq_l[:, hd * per:(hd + 1) * per] for hd in range(HEADS)]
    nope = [hh[:, :MLA_NOPE] for hh in heads]
    pe = [hh[:, MLA_NOPE:] for hh in heads]
    pairs = []
    for p in range(HEADS // 2):
        a, b = pe[2 * p], pe[2 * p + 1]
        pairs += [a[:, :32], b[:, :32], a[:, 32:], b[:, 32:]]
    return jnp.concatenate(nope + pairs, axis=1).astype(BF16)


def _ukv_layout(w_ukv_l):
    per = MLA_NOPE + MLA_V
    heads = [w_ukv_l[:, hd * per:(hd + 1) * per] for hd in range(HEADS)]
    return jnp.concatenate([hh[:, :MLA_NOPE] for hh in heads] + [hh[:, MLA_NOPE:] for hh in heads],
                           axis=1).astype(BF16)


def _block(n, want):
    return want if n % want == 0 else n


def kernel(x, positions, norm_mix_w, w_in, hg_lower_bounds, hg_norm_w, ret_norm_w, mla_q_norm_w, mla_w_uq, mla_kv_norm_w, mla_w_ukv, w_br_a, w_br_b, w_br_c, w_out, norm_ffn_w, w_ffn_in, w_ffn_out, final_norm_w):
    b, s, d = x.shape
    depth = w_in.shape[0]
    assert d == D_MODEL and s % (2 * CHUNK) == 0
    t = b * s
    ts = _block(s, 512)
    tm = _block(t, 512)
    tq = _block(s, 512)

    lb_p = jax.nn.softmax(hg_lower_bounds.astype(F32), axis=0)
    lb_all = jnp.cumsum(lb_p, axis=0) - lb_p[0]

    rc, rs, mc, ms = _rope_tables(positions, _block(t, 512))
    row = lambda a: a.reshape(1, -1).astype(F32)

    x2 = x.reshape(t, d)
    for l in range(depth):
        w_hg, w_ret, w_c, w_g = _in_proj_slices(w_in[l])
        nw = row(norm_mix_w[l])
        ya = _hg_mixer(x2, nw, w_hg, row(lb_all[l]), row(hg_norm_w[l]), b, s, ts)
        yb = _ret_mixer(x2, nw, w_ret, row(ret_norm_w[l]), rc, rs, b, s, ts)
        q, k, v = _mla_pre(x2, nw, w_c, row(mla_q_norm_w[l]), _uq_layout(mla_w_uq[l]),
                           row(mla_kv_norm_w[l]), _ukv_layout(mla_w_ukv[l]), mc, ms, tm)
        yc = _mla_attn(q, k, v, b, s, tq)
        x2 = _merge(x2, nw, w_g, ya, yb, yc, w_br_a[l].astype(BF16), w_br_b[l].astype(BF16),
                    w_br_c[l].astype(BF16), w_out[l].astype(BF16), tm)
        x2 = _ffn(x2, row(norm_ffn_w[l]), w_ffn_in[l].astype(BF16), w_ffn_out[l].astype(BF16),
                  row(final_norm_w), tm, final_norm=(l == depth - 1))
    return x2.reshape(b, s, d)
```

```python
import functools

import numpy as np
import jax
import jax.numpy as jnp
from jax import lax
from jax.experimental import pallas as pl
from jax.experimental.pallas import tpu as pltpu

F32 = jnp.float32
BF16 = jnp.bfloat16

D_MODEL = 1024
HEADS = 4
HG_K = 128
HG_V = 128
RET_K = 128
RET_V = 256
RET_HEAD_COLS = 2 * RET_K + 2 * RET_V
MLA_Q_RANK = 256
MLA_KV_RANK = 128
MLA_NOPE = 128
MLA_ROPE = 64
MLA_V = 128
MLA_QK = 256
D_FF = 2816
CHUNK = 64
ROPE_BASE = 10000.0
EPS = 1e-6
EXP_CLIP = 60.0
MASK_NEG = -1e30
LOG2E = 1.4426950408889634

LANES = 128
VMEM_LIMIT = 56 * 1024 * 1024

LEVELS = (32, 16, 8, 4, 2, 1)
CHUNKS_PER_ITER = 4
N_ITER = 2
PROJ_SLAB = 256
MLA_ROWS = 128


def _rms(x, w):
    return x * lax.rsqrt(jnp.mean(x * x, axis=-1, keepdims=True) + EPS) * w


def _sigmoid(x):
    return 0.5 + 0.5 * jnp.tanh(0.5 * x)


def _silu(x):
    return x * _sigmoid(x)


def _dot(a, b):
    return jnp.dot(a, b, preferred_element_type=F32)


def _dot_nt(a, b):
    return lax.dot_general(a, b, (((1,), (1,)), ((), ())), preferred_element_type=F32)


def _dot_tn(a, b):
    return lax.dot_general(a, b, (((0,), (0,)), ((), ())), preferred_element_type=F32)


def _params(sem):
    return pltpu.CompilerParams(dimension_semantics=sem, vmem_limit_bytes=VMEM_LIMIT)


def _rope_table_kernel(pos_ref, inv_ref, rc_ref, rs_ref, mc_ref, ms_ref):
    pos = pos_ref[...].astype(F32)
    ang = inv_ref[...] * pos
    ct = jnp.transpose(jnp.cos(ang))
    st = jnp.transpose(jnp.sin(ang))
    lo = lax.broadcasted_iota(jnp.int32, ct.shape, 1) < 64
    ct_r = pltpu.roll(ct, 64, 1)
    st_r = pltpu.roll(st, 64, 1)
    rc_ref[...] = jnp.where(lo, ct, ct_r)
    rs_ref[...] = jnp.where(lo, -st, st_r)
    mc_ref[...] = jnp.where(lo, ct_r, ct)
    ms_ref[...] = jnp.where(lo, -st_r, st)


def _rope_tables(positions, tt):
    t = positions.size
    pos = positions.reshape(1, t)
    inv64 = ROPE_BASE ** (-jnp.arange(64, dtype=F32) / 64)
    inv32 = ROPE_BASE ** (-jnp.arange(32, dtype=F32) / 32)
    inv = jnp.concatenate([inv64, inv32, inv32]).reshape(LANES, 1)
    tab = jax.ShapeDtypeStruct((t, LANES), F32)
    return pl.pallas_call(
        _rope_table_kernel,
        grid=(t // tt,),
        in_specs=[pl.BlockSpec((1, tt), lambda i: (0, i)),
                  pl.BlockSpec((LANES, 1), lambda i: (0, 0))],
        out_specs=[pl.BlockSpec((tt, LANES), lambda i: (i, 0))] * 4,
        out_shape=[tab] * 4,
        compiler_params=_params(("parallel",)),
        name="rope_tables",
    )(pos, inv)


def _rope(t, c, s):
    return t * c + pltpu.roll(t, 64, 1) * s


def _split3(x):
    a = x.astype(BF16)
    r = x - a.astype(F32)
    b = r.astype(BF16)
    c = (r - b.astype(F32)).astype(BF16)
    return a, b, c


def _level_ref(cum, m):
    w = cum.shape[1]
    if m >= 8:
        g = CHUNK // (2 * m)
        c3 = cum.reshape(g, 2 * m, w)
        return jnp.broadcast_to(c3[:, m - 1:m, :], (g, 2 * m, w)).reshape(CHUNK, w)
    c3 = cum.reshape(8, 8, w)

    def row(r):
        return jnp.broadcast_to(c3[:, r:r + 1, :], (8, 8, w))

    if m == 4:
        out = row(3)
    else:
        sub = lax.broadcasted_iota(jnp.int32, (8, 8, w), 1)
        out = jnp.where(sub < 4, row(1), row(5))
    return out.reshape(CHUNK, w)


def _level_masks():
    t = np.arange(CHUNK)[:, None]
    s = np.arange(CHUNK)[None, :]
    masks = [(t == s)]
    for m in LEVELS:
        masks.append((t // (2 * m) == s // (2 * m)) & (t % (2 * m) >= m) & (s % (2 * m) < m))
    return jnp.asarray(np.tile(np.stack(masks).astype(np.float32), (1, 1, 2)))


def _emit_interleaved(tasks, fillers, before):
    assert len(before) == len(fillers)
    for i, task in enumerate(tasks):
        for j, pos in enumerate(before):
            if pos == i:
                fillers[j]()
        task()


def _hg_group(row_starts, p_scr, st_scr, lb_ref, gw_ref, tril_ref, masks_ref, o_ref, fillers):
    width = HEADS * HG_K
    pw = 2 * HG_K
    tril = tril_ref[...]
    rows = lax.broadcasted_iota(jnp.int32, (CHUNK, pw), 0)
    upper = {m: (rows % (2 * m)) >= m for m in LEVELS}
    zk = jnp.zeros((CHUNK, HG_K), BF16)

    def blockdiag(t):
        return jnp.concatenate([jnp.concatenate([t[:, :HG_K], zk], axis=1),
                                jnp.concatenate([zk, t[:, HG_K:]], axis=1)], axis=0)

    def gates(r0, pair):
        c0 = pair * pw
        lb = lb_ref[:, c0:c0 + pw]
        q = _silu(p_scr[pl.ds(r0, CHUNK), c0:c0 + pw])
        hf = p_scr[pl.ds(r0, CHUNK), width + c0:width + c0 + pw]
        e = jnp.exp2(jnp.abs(hf) * -LOG2E)
        u = jnp.exp2(jnp.minimum(hf * -LOG2E, EXP_CLIP * LOG2E))
        r = 1.0 / (1.0 + e)
        logf = (jnp.minimum(hf, 0.0) + jnp.log((1.0 + lb * u) * r)) * LOG2E
        k = (1.0 - lb) * (jnp.where(hf >= 0.0, e, 1.0) * r)
        a, b, c = _split3(logf)
        cum = _dot(tril, a) + _dot(tril, b) + _dot(tril, c)
        return q, k, logf, cum

    def level_dots(q, k, logf, cum):
        c_last = cum[CHUNK - 1:CHUNK, :]
        qd = (q * jnp.exp2(cum)).astype(BF16)
        kd = (k * jnp.exp2(c_last - cum)).astype(BF16)
        dec = jnp.exp2(c_last)
        prods = [_dot_nt(q.astype(BF16), blockdiag(k.astype(BF16)))]
        for m in LEVELS:
            if m == 1:
                xl = jnp.where(upper[m], q * jnp.exp2(logf), k)
            else:
                xl = jnp.where(upper[m], q, k) * jnp.exp2(-jnp.abs(cum - _level_ref(cum, m)))
            xl = xl.astype(BF16)
            prods.append(_dot_nt(xl, blockdiag(xl)))
        return qd, kd, dec, prods

    def finish(r0, pair, qd, kd, dec, prods):
        c0 = pair * pw
        v = p_scr[pl.ds(r0, CHUNK), 2 * width + c0:2 * width + c0 + pw].astype(BF16)
        gate = _silu(p_scr[pl.ds(r0, CHUNK), 3 * width + c0:3 * width + c0 + pw])
        att = masks_ref[0] * prods[0]
        for li in range(1, len(prods)):
            att = att + masks_ref[li] * prods[li]
        o = _dot(att.astype(BF16), blockdiag(v))
        outs = []
        for sub in range(2):
            hs = slice(sub * HG_K, (sub + 1) * HG_K)
            st = st_scr[2 * pair + sub]
            oh = o[:, sub * HG_V:(sub + 1) * HG_V] + _dot_nt(qd[:, hs], st.astype(BF16))
            st_scr[2 * pair + sub] = st * dec[:, hs] + _dot_tn(v[:, hs], kd[:, hs])
            outs.append(oh * lax.rsqrt(jnp.mean(oh * oh, axis=-1, keepdims=True) + EPS))
        y = jnp.concatenate(outs, axis=1) * gw_ref[:, c0:c0 + pw] * gate
        o_ref[pl.ds(r0, CHUNK), c0:c0 + pw] = y.astype(o_ref.dtype)

    units = [(r0, pair) for r0 in row_starts for pair in range(HEADS // 2)]
    stage1, stage2 = {}, {}
    tasks = ([lambda i=i, u=u: stage1.__setitem__(i, gates(*u)) for i, u in enumerate(units)]
             + [lambda i=i: stage2.__setitem__(i, level_dots(*stage1[i])) for i in range(len(units))]
             + [lambda i=i, u=u: finish(*u, *stage2[i]) for i, u in enumerate(units)])
    n = len(units)
    _emit_interleaved(tasks, fillers, (0, 2, 4, 6, 2 * n, 2 * n + n // 2))


def _ret_tables():
    log_gamma = jnp.log1p(-jnp.exp2(-5.0 - jnp.arange(HEADS, dtype=F32)))
    idx = jnp.arange(CHUNK, dtype=F32)
    causal = jnp.tril(jnp.ones((CHUNK, CHUNK), dtype=bool))[None]
    lg = log_gamma[:, None, None]
    dist = jnp.where(causal, (idx[:, None] - idx[None, :])[None], 0.0)
    intra = jnp.where(causal, jnp.exp(dist * lg), 0.0)
    q_decay = jnp.exp((idx + 1.0)[None, :] * log_gamma[:, None])[..., None]
    k_decay = jnp.exp((CHUNK - 1.0 - idx)[None, :] * log_gamma[:, None])[..., None]
    chunk_decay = jnp.exp(CHUNK * log_gamma)[:, None, None]
    return (intra,
            jnp.broadcast_to(q_decay, (HEADS, CHUNK, RET_K)),
            jnp.broadcast_to(k_decay, (HEADS, CHUNK, RET_K)),
            jnp.broadcast_to(chunk_decay, (HEADS, 1, RET_K)))


def _ret_group(row_starts, p_scr, st_scr, gw_ref, rc_ref, rs_ref, intra_ref, qdec_ref, kdec_ref, cdec_ref, o_ref,
               fillers):
    k_scale = RET_K ** -0.5

    def cols(hd, lo, hi):
        base = (hd % 2) * RET_HEAD_COLS
        return hd // 2, slice(base + lo, base + hi)

    def scores(r0, hd):
        cos = rc_ref[pl.ds(r0, CHUNK), :]
        sin = rs_ref[pl.ds(r0, CHUNK), :]
        half, qs = cols(hd, 0, RET_K)
        _, ks = cols(hd, RET_K, 2 * RET_K)
        q = _rope(p_scr[half, pl.ds(r0, CHUNK), qs], cos, sin)
        k = _rope(p_scr[half, pl.ds(r0, CHUNK), ks], cos, sin) * k_scale
        return q, k, _dot_nt(q.astype(BF16), k.astype(BF16))

    def finish(r0, hd, q, k, qk):
        half, vs = cols(hd, 2 * RET_K, 2 * RET_K + RET_V)
        _, gs = cols(hd, 2 * RET_K + RET_V, RET_HEAD_COLS)
        v = p_scr[half, pl.ds(r0, CHUNK), vs].astype(BF16)
        gate = _silu(p_scr[half, pl.ds(r0, CHUNK), gs])
        att = qk * intra_ref[hd]
        st = st_scr[hd]
        o = _dot(att.astype(BF16), v) + _dot_nt((q * qdec_ref[hd]).astype(BF16), st.astype(BF16))
        st_scr[hd] = st * cdec_ref[hd] + _dot_tn(v, (k * kdec_ref[hd]).astype(BF16))
        o = o * lax.rsqrt(jnp.mean(o * o, axis=-1, keepdims=True) + EPS)
        y = o * gw_ref[:, hd * RET_V:(hd + 1) * RET_V] * gate
        o_ref[pl.ds(r0, CHUNK), hd * RET_V:(hd + 1) * RET_V] = y.astype(o_ref.dtype)

    units = [(r0, hd) for r0 in row_starts for hd in range(HEADS)]
    stage1 = {}
    tasks = ([lambda i=i, u=u: stage1.__setitem__(i, scores(*u)) for i, u in enumerate(units)]
             + [lambda i=i, u=u: finish(*u, *stage1[i]) for i, u in enumerate(units)])
    _emit_interleaved(tasks, fillers, (0, len(units) // 2))


def _mla_rows(r0, n, h_scr, wc_ref, qn_ref, wuq_ref, kvn_ref, wukv_ref, mc_ref, ms_ref, q_ref, k_ref, v_ref):
    rows = pl.ds(r0, n)
    c = _dot(h_scr[rows, :], wc_ref[...])
    cos = mc_ref[rows, :]
    sin = ms_ref[rows, :]
    cq = _rms(c[:, 0:MLA_Q_RANK], qn_ref[...]).astype(BF16)
    qm = _dot(cq, wuq_ref[...]) * ((MLA_NOPE + MLA_ROPE) ** -0.5 * LOG2E)
    ckv = _rms(c[:, MLA_Q_RANK:MLA_Q_RANK + MLA_KV_RANK], kvn_ref[...]).astype(BF16)
    kv = _dot(ckv, wukv_ref[...])
    k_pe = _rope(c[:, MLA_Q_RANK + MLA_KV_RANK:], cos, sin).astype(BF16)
    lane = lax.broadcasted_iota(jnp.int32, k_pe.shape, 1)
    first = (lane % 64) < 32
    nope_w = HEADS * MLA_NOPE
    for pair in range(HEADS // 2):
        q_pe = _rope(qm[:, nope_w + pair * LANES:nope_w + (pair + 1) * LANES], cos, sin)
        for sub in range(2):
            hd = 2 * pair + sub
            keep = first if sub == 0 else jnp.logical_not(first)
            q_ref[rows, hd * MLA_QK:hd * MLA_QK + MLA_NOPE] = qm[:, hd * MLA_NOPE:(hd + 1) * MLA_NOPE].astype(BF16)
            q_ref[rows, hd * MLA_QK + MLA_NOPE:(hd + 1) * MLA_QK] = jnp.where(keep, q_pe, 0.0).astype(BF16)
    for hd in range(HEADS):
        k_ref[rows, hd * MLA_QK:hd * MLA_QK + MLA_NOPE] = kv[:, hd * MLA_NOPE:(hd + 1) * MLA_NOPE].astype(BF16)
        k_ref[rows, hd * MLA_QK + MLA_NOPE:(hd + 1) * MLA_QK] = k_pe
    v_ref[rows, :] = kv[:, nope_w:].astype(BF16)


def _mixers_kernel(x_ref, nw_ref, whg_ref, wret_ref, wc_ref, qn_ref, wuq_ref, kvn_ref, wukv_ref,
                   lb_ref, hgw_ref, rgw_ref, rc_ref, rs_ref, mc_ref, ms_ref, tril_ref, masks_ref,
                   intra_ref, qdec_ref, kdec_ref, cdec_ref,
                   ya_ref, yb_ref, q_ref, k_ref, v_ref,
                   h_scr, phg_scr, pret_scr, sthg_scr, stret_scr, *, ts):
    @pl.when(pl.program_id(1) == 0)
    def _():
        sthg_scr[...] = jnp.zeros_like(sthg_scr)
        stret_scr[...] = jnp.zeros_like(stret_scr)

    h_scr[...] = _rms(x_ref[...], nw_ref[...]).astype(BF16)
    phg_scr[...] = _dot(h_scr[...], whg_ref[...])

    group = CHUNKS_PER_ITER * CHUNK
    starts = lambda it: [pl.multiple_of(it * group + ci * CHUNK, CHUNK) for ci in range(CHUNKS_PER_ITER)]

    def hg_body(it, carry):
        def ret_proj(c0):
            pret_scr[it, :, c0:c0 + PROJ_SLAB] = _dot(h_scr[...], wret_ref[it, :, c0:c0 + PROJ_SLAB])

        fillers = [functools.partial(ret_proj, c0) for c0 in range(0, 2 * RET_HEAD_COLS, PROJ_SLAB)]
        _hg_group(starts(it), phg_scr, sthg_scr, lb_ref, hgw_ref, tril_ref, masks_ref, ya_ref, fillers)
        return carry

    lax.fori_loop(0, N_ITER, hg_body, 0)

    def ret_body(it, carry):
        def mla(part):
            r0 = pl.multiple_of(it * group + part * MLA_ROWS, MLA_ROWS)
            _mla_rows(r0, MLA_ROWS, h_scr, wc_ref, qn_ref, wuq_ref, kvn_ref, wukv_ref, mc_ref, ms_ref,
                      q_ref, k_ref, v_ref)

        fillers = [functools.partial(mla, part) for part in range(group // MLA_ROWS)]
        _ret_group(starts(it), pret_scr, stret_scr, rgw_ref, rc_ref, rs_ref, intra_ref, qdec_ref, kdec_ref,
                   cdec_ref, yb_ref, fillers)
        return carry

    lax.fori_loop(0, N_ITER, ret_body, 0)


def _mixers(x2, nw, w_hg, w_ret, w_c, qn, wuq, kvn, wukv, lb, hgw, rgw, rc, rs, mc, ms, b, s, ts):
    assert ts == N_ITER * CHUNKS_PER_ITER * CHUNK
    nblk = s // ts
    t = b * s
    tril = jnp.asarray(np.tril(np.ones((CHUNK, CHUNK), np.float32)), BF16)
    masks = _level_masks()
    intra, qdec, kdec, cdec = _ret_tables()
    const = lambda a: pl.BlockSpec(a.shape, lambda i, j: (0,) * a.ndim)
    rows = lambda width: pl.BlockSpec((ts, width), lambda i, j: (i * nblk + j, 0))
    consts1 = (nw, w_hg, w_ret, w_c, qn, wuq, kvn, wukv, lb, hgw, rgw)
    consts2 = (tril, masks, intra, qdec, kdec, cdec)
    out_widths = (HEADS * HG_V, HEADS * RET_V, HEADS * MLA_QK, HEADS * MLA_QK, HEADS * MLA_V)
    return pl.pallas_call(
        functools.partial(_mixers_kernel, ts=ts),
        grid=(b, nblk),
        in_specs=([rows(D_MODEL)] + [const(a) for a in consts1] + [rows(LANES)] * 4 + [const(a) for a in consts2]),
        out_specs=[rows(w) for w in out_widths],
        out_shape=[jax.ShapeDtypeStruct((t, w), BF16) for w in out_widths],
        scratch_shapes=[pltpu.VMEM((ts, D_MODEL), BF16),
                        pltpu.VMEM((ts, 4 * HEADS * HG_K), F32),
                        pltpu.VMEM((N_ITER, ts, 2 * RET_HEAD_COLS), F32),
                        pltpu.VMEM((HEADS, HG_V, HG_K), F32),
                        pltpu.VMEM((HEADS, RET_V, RET_K), F32)],
        compiler_params=_params(("parallel", "arbitrary")),
        name="mixers",
    )(x2, *consts1, rc, rs, mc, ms, *consts2)


def _mla_attn_kernel(q_ref, k_ref, v_ref, o_ref, acc_scr, *, tq):
    qi = pl.program_id(1)
    key = lax.broadcasted_iota(jnp.int32, (tq, tq), 0)
    qry = lax.broadcasted_iota(jnp.int32, (tq, tq), 1)
    causal = key <= qry
    acc_scr[...] = jnp.zeros_like(acc_scr)

    def step(kj, carry, masked):
        ms, ls = carry
        k0 = pl.multiple_of(kj * tq, tq)
        new_m, new_l = [], []
        scores = [_dot_nt(k_ref[pl.ds(k0, tq), hd * MLA_QK:(hd + 1) * MLA_QK], q_ref[:, hd * MLA_QK:(hd + 1) * MLA_QK])
                  for hd in range(HEADS)]
        for hd in range(HEADS):
            s_t = scores[hd]
            if masked:
                s_t = jnp.where(causal, s_t, MASK_NEG)
            m_new = jnp.maximum(ms[hd], jnp.max(s_t, axis=0, keepdims=True))
            alpha = jnp.exp2(ms[hd] - m_new)
            p = jnp.exp2(s_t - m_new)
            new_m.append(m_new)
            new_l.append(ls[hd] * alpha + jnp.sum(p, axis=0, keepdims=True))
            pv = _dot_tn(v_ref[pl.ds(k0, tq), hd * MLA_V:(hd + 1) * MLA_V], p.astype(BF16))
            acc_scr[hd] = acc_scr[hd] * alpha + pv
        return tuple(new_m), tuple(new_l)

    init = (tuple(jnp.full((1, tq), MASK_NEG, F32) for _ in range(HEADS)),
            tuple(jnp.zeros((1, tq), F32) for _ in range(HEADS)))
    carry = lax.fori_loop(0, qi, functools.partial(step, masked=False), init)
    _, ls = step(qi, carry, True)
    for hd in range(HEADS):
        o_ref[:, hd * MLA_V:(hd + 1) * MLA_V] = jnp.transpose(acc_scr[hd] / ls[hd]).astype(o_ref.dtype)


def _mla_attn(q, k, v, b, s, tq):
    nq = s // tq
    return pl.pallas_call(
        functools.partial(_mla_attn_kernel, tq=tq),
        grid=(b, nq),
        in_specs=[pl.BlockSpec((tq, HEADS * MLA_QK), lambda i, j: (i * nq + j, 0)),
                  pl.BlockSpec((s, HEADS * MLA_QK), lambda i, j: (i, 0)),
                  pl.BlockSpec((s, HEADS * MLA_V), lambda i, j: (i, 0))],
        out_specs=pl.BlockSpec((tq, HEADS * MLA_V), lambda i, j: (i * nq + j, 0)),
        out_shape=jax.ShapeDtypeStruct((b * s, HEADS * MLA_V), BF16),
        scratch_shapes=[pltpu.VMEM((HEADS, MLA_V, tq), F32)],
        compiler_params=_params(("parallel", "arbitrary")),
        name="mla_attention",
    )(q, k, v)


def _merge_kernel(x_ref, nw_ref, wg_ref, ya_ref, yb_ref, yc_ref, wa_ref, wb_ref, wc_ref, wo_ref, o_ref):
    x = x_ref[...]
    h = _rms(x, nw_ref[...]).astype(BF16)
    merged = (_sigmoid(_dot(h, wg_ref[:, 0:D_MODEL])) * _dot(ya_ref[...], wa_ref[...])
              + _sigmoid(_dot(h, wg_ref[:, D_MODEL:2 * D_MODEL])) * _dot(yb_ref[...], wb_ref[...])
              + _sigmoid(_dot(h, wg_ref[:, 2 * D_MODEL:3 * D_MODEL])) * _dot(yc_ref[...], wc_ref[...]))
    o_ref[...] = x + _dot(merged.astype(BF16), wo_ref[...])


def _merge(x2, nw, wg, ya, yb, yc, wa, wb, wc, wo, tm):
    t = x2.shape[0]
    const = lambda a: pl.BlockSpec(a.shape, lambda i: (0,) * a.ndim)
    rows = lambda a: pl.BlockSpec((tm, a.shape[1]), lambda i: (i, 0))
    return pl.pallas_call(
        _merge_kernel,
        grid=(t // tm,),
        in_specs=[rows(x2), const(nw), const(wg), rows(ya), rows(yb), rows(yc),
                  const(wa), const(wb), const(wc), const(wo)],
        out_specs=rows(x2),
        out_shape=jax.ShapeDtypeStruct(x2.shape, F32),
        compiler_params=_params(("parallel",)),
        name="gated_merge",
    )(x2, nw, wg, ya, yb, yc, wa, wb, wc, wo)


def _ffn_kernel(x_ref, nw_ref, wi_ref, wo_ref, fw_ref, o_ref, *, final_norm):
    x = x_ref[...]
    h = _rms(x, nw_ref[...]).astype(BF16)
    g = _dot(h, wi_ref[:, 0:D_FF])
    u = _dot(h, wi_ref[:, D_FF:2 * D_FF])
    y = x + _dot((_silu(g) * u).astype(BF16), wo_ref[...])
    if final_norm:
        y = _rms(y, fw_ref[...])
    o_ref[...] = y


def _ffn(x2, nw, wi, wo, fw, tm, final_norm):
    t = x2.shape[0]
    const = lambda a: pl.BlockSpec(a.shape, lambda i: (0,) * a.ndim)
    rows = lambda a: pl.BlockSpec((tm, a.shape[1]), lambda i: (i, 0))
    return pl.pallas_call(
        functools.partial(_ffn_kernel, final_norm=final_norm),
        grid=(t // tm,),
        in_specs=[rows(x2), const(nw), const(wi), const(wo), const(fw)],
        out_specs=rows(x2),
        out_shape=jax.ShapeDtypeStruct(x2.shape, F32),
        compiler_params=_params(("parallel",)),
        name="swiglu_ffn",
    )(x2, nw, wi, wo, fw)


def _in_proj_slices(w_in_l):
    widths = (512, 512, 512, 512, 512, 512, 1024, 1024, MLA_Q_RANK, MLA_KV_RANK, MLA_ROPE, 3 * D_MODEL)
    offs = np.concatenate([[0], np.cumsum(widths)])
    cols = [w_in_l[:, int(offs[i]):int(offs[i + 1])] for i in range(len(widths))]
    w_hg = jnp.concatenate(cols[0:4], axis=1)
    rq, rk, rv, rg = cols[4:8]
    per_head = [jnp.concatenate([rq[:, hd * RET_K:(hd + 1) * RET_K], rk[:, hd * RET_K:(hd + 1) * RET_K],
                                 rv[:, hd * RET_V:(hd + 1) * RET_V], rg[:, hd * RET_V:(hd + 1) * RET_V]], axis=1)
                for hd in range(HEADS)]
    w_ret = jnp.stack([jnp.concatenate(per_head[2 * i:2 * i + 2], axis=1) for i in range(N_ITER)])
    kr = cols[10]
    kr1, kr2 = kr[:, :32], kr[:, 32:]
    w_c = jnp.concatenate([cols[8], cols[9], kr1, kr1, kr2, kr2], axis=1)
    return w_hg.astype(BF16), w_ret.astype(BF16), w_c.astype(BF16), cols[11].astype(BF16)


def _uq_layout(w_uq_l):
    per = MLA_NOPE + MLA_ROPE
    heads = [w_uq_l[:, hd * per:(hd + 1) * per] for hd in range(HEADS)]
    nope = [hh[:, :MLA_NOPE] for hh in heads]
    pe = [hh[:, MLA_NOPE:] for hh in heads]
    pairs = []
    for p in range(HEADS // 2):
        a, b = pe[2 * p], pe[2 * p + 1]
        pairs += [a[:, :32], b[:, :32], a[:, 32:], b[:, 32:]]
    return jnp.concatenate(nope + pairs, axis=1).astype(BF16)


def _ukv_layout(w_ukv_l):
    per = MLA_NOPE + MLA_V
    heads = [w_ukv_l[:, hd * per:(hd + 1) * per] for hd in range(HEADS)]
    return jnp.concatenate([hh[:, :MLA_NOPE] for hh in heads] + [hh[:, MLA_NOPE:] for hh in heads],
                           axis=1).astype(BF16)


def _block(n, want):
    return want if n % want == 0 else n


def kernel(x, positions, norm_mix_w, w_in, hg_lower_bounds, hg_norm_w, ret_norm_w, mla_q_norm_w, mla_w_uq, mla_kv_norm_w, mla_w_ukv, w_br_a, w_br_b, w_br_c, w_out, norm_ffn_w, w_ffn_in, w_ffn_out, final_norm_w):
    b, s, d = x.shape
    depth = w_in.shape[0]
    ts = N_ITER * CHUNKS_PER_ITER * CHUNK
    assert d == D_MODEL and s % ts == 0
    t = b * s
    tm = _block(t, 512)
    tq = _block(s, 512)

    lb_p = jax.nn.softmax(hg_lower_bounds.astype(F32), axis=0)
    lb_all = jnp.cumsum(lb_p, axis=0) - lb_p[0]

    rc, rs, mc, ms = _rope_tables(positions, _block(t, 512))
    row = lambda a: a.reshape(1, -1).astype(F32)

    x2 = x.reshape(t, d)
    for l in range(depth):
        w_hg, w_ret, w_c, w_g = _in_proj_slices(w_in[l])
        nw = row(norm_mix_w[l])
        ya, yb, q, k, v = _mixers(x2, nw, w_hg, w_ret, w_c, row(mla_q_norm_w[l]), _uq_layout(mla_w_uq[l]),
                                  row(mla_kv_norm_w[l]), _ukv_layout(mla_w_ukv[l]), row(lb_all[l]),
                                  row(hg_norm_w[l]), row(ret_norm_w[l]), rc, rs, mc, ms, b, s, ts)
        yc = _mla_attn(q, k, v, b, s, tq)
        x2 = _merge(x2, nw, w_g, ya, yb, yc, w_br_a[l].astype(BF16), w_br_b[l].astype(BF16),
                    w_br_c[l].astype(BF16), w_out[l].astype(BF16), tm)
        x2 = _ffn(x2, row(norm_ffn_w[l]), w_ffn_in[l].astype(BF16), w_ffn_out[l].astype(BF16),
                  row(final_norm_w), tm, final_norm=(l == depth - 1))
    return x2.reshape(b, s, d)
```

```python
import functools

import numpy as np
import jax
import jax.numpy as jnp
from jax import lax
from jax.experimental import pallas as pl
from jax.experimental.pallas import tpu as pltpu

F32 = jnp.float32
BF16 = jnp.bfloat16

D_MODEL = 1024
HEADS = 4
HG_K = 128
HG_V = 128
RET_K = 128
RET_V = 256
MLA_Q_RANK = 256
MLA_KV_RANK = 128
MLA_NOPE = 128
MLA_ROPE = 64
MLA_V = 128
MLA_QK = 256
D_FF = 2816
CHUNK = 64
CHUNKS_PER_ITER = 4
RET_CHUNK = 128
RET_CHUNKS_PER_ITER = 2
ROPE_BASE = 10000.0
EPS = 1e-6
EXP_CLIP = 60.0
MASK_NEG = -1e30
LOG2E = 1.4426950408889634

LANES = 128
VMEM_LIMIT = 56 * 1024 * 1024

LEVELS = (32, 16, 8, 4, 2, 1)


def _rms(x, w):
    return x * lax.rsqrt(jnp.mean(x * x, axis=-1, keepdims=True) + EPS) * w


def _sigmoid(x):
    return 0.5 + 0.5 * jnp.tanh(0.5 * x)


def _silu(x):
    return x * _sigmoid(x)


def _dot(a, b):
    return jnp.dot(a, b, preferred_element_type=F32)


def _dot_nt(a, b):
    return lax.dot_general(a, b, (((1,), (1,)), ((), ())), preferred_element_type=F32)


def _dot_tn(a, b):
    return lax.dot_general(a, b, (((0,), (0,)), ((), ())), preferred_element_type=F32)


def _params(sem):
    return pltpu.CompilerParams(dimension_semantics=sem, vmem_limit_bytes=VMEM_LIMIT)


def _rope_table_kernel(pos_ref, inv_ref, rc_ref, rs_ref, mc_ref, ms_ref):
    pos = pos_ref[...].astype(F32)
    ang = inv_ref[...] * pos
    ct = jnp.transpose(jnp.cos(ang))
    st = jnp.transpose(jnp.sin(ang))
    lo = lax.broadcasted_iota(jnp.int32, ct.shape, 1) < 64
    ct_r = pltpu.roll(ct, 64, 1)
    st_r = pltpu.roll(st, 64, 1)
    rc_ref[...] = jnp.where(lo, ct, ct_r)
    rs_ref[...] = jnp.where(lo, -st, st_r)
    mc_ref[...] = jnp.where(lo, ct_r, ct)
    ms_ref[...] = jnp.where(lo, -st_r, st)


def _rope_tables(positions, tt):
    t = positions.size
    pos = positions.reshape(1, t)
    inv64 = ROPE_BASE ** (-jnp.arange(64, dtype=F32) / 64)
    inv32 = ROPE_BASE ** (-jnp.arange(32, dtype=F32) / 32)
    inv = jnp.concatenate([inv64, inv32, inv32]).reshape(LANES, 1)
    tab = jax.ShapeDtypeStruct((t, LANES), F32)
    return pl.pallas_call(
        _rope_table_kernel,
        grid=(t // tt,),
        in_specs=[pl.BlockSpec((1, tt), lambda i: (0, i)),
                  pl.BlockSpec((LANES, 1), lambda i: (0, 0))],
        out_specs=[pl.BlockSpec((tt, LANES), lambda i: (i, 0))] * 4,
        out_shape=[tab] * 4,
        compiler_params=_params(("parallel",)),
        name="rope_tables",
    )(pos, inv)


def _rope(t, c, s):
    return t * c + pltpu.roll(t, 64, 1) * s


def _split3(x):
    a = x.astype(BF16)
    r = x - a.astype(F32)
    b = r.astype(BF16)
    c = (r - b.astype(F32)).astype(BF16)
    return a, b, c


def _level_ref(cum, m):
    w = cum.shape[1]
    if m >= 8:
        g = CHUNK // (2 * m)
        c3 = cum.reshape(g, 2 * m, w)
        return jnp.broadcast_to(c3[:, m - 1:m, :], (g, 2 * m, w)).reshape(CHUNK, w)
    c3 = cum.reshape(8, 8, w)

    def row(r):
        return jnp.broadcast_to(c3[:, r:r + 1, :], (8, 8, w))

    if m == 4:
        out = row(3)
    else:
        sub = lax.broadcasted_iota(jnp.int32, (8, 8, w), 1)
        out = jnp.where(sub < 4, row(1), row(5))
    return out.reshape(CHUNK, w)


def _upper_q_lower_k(q, k, m, upper):
    if m < 8:
        return jnp.where(upper[m], q, k)
    return jnp.concatenate([(q if blk % 2 else k)[blk * m:(blk + 1) * m] for blk in range(CHUNK // m)], axis=0)


def _hg_kernel(x_ref, nw_ref, w_ref, lb_ref, gw_ref, tril_ref, masks_ref, o_ref, p_scr, st_scr, *, ts):
    @pl.when(pl.program_id(1) == 0)
    def _():
        st_scr[...] = jnp.zeros_like(st_scr)

    h = _rms(x_ref[...], nw_ref[...]).astype(BF16)
    p_scr[...] = _dot(h, w_ref[...])

    width = HEADS * HG_K
    pw = 2 * HG_K
    tril = tril_ref[...]
    rows = lax.broadcasted_iota(jnp.int32, (CHUNK, pw), 0)
    upper = {m: (rows % (2 * m)) >= m for m in LEVELS}
    zk = jnp.zeros((CHUNK, HG_K), BF16)

    def blockdiag(t):
        return jnp.concatenate([jnp.concatenate([t[:, :HG_K], zk], axis=1),
                                jnp.concatenate([zk, t[:, HG_K:]], axis=1)], axis=0)


    def gates(r0, pair):
        c0 = pair * pw
        lb = lb_ref[:, c0:c0 + pw]
        q = _silu(p_scr[pl.ds(r0, CHUNK), c0:c0 + pw])
        hf = p_scr[pl.ds(r0, CHUNK), width + c0:width + c0 + pw]
        e = jnp.exp2(jnp.abs(hf) * -LOG2E)
        u = jnp.exp2(jnp.minimum(hf * -LOG2E, EXP_CLIP * LOG2E))
        r = 1.0 / (1.0 + e)
        logf = (jnp.minimum(hf, 0.0) + jnp.log((1.0 + lb * u) * r)) * LOG2E
        k = (1.0 - lb) * (jnp.where(hf >= 0.0, e, 1.0) * r)
        cums = _dot(tril, jnp.concatenate(_split3(logf), axis=1))
        cum = cums[:, :pw] + cums[:, pw:2 * pw] + cums[:, 2 * pw:]
        return q, k, logf, cum

    def level_dots(q, k, logf, cum):
        c_last = cum[CHUNK - 1:CHUNK, :]
        qd = (q * jnp.exp2(cum)).astype(BF16)
        kd = (k * jnp.exp2(c_last - cum)).astype(BF16)
        dec = jnp.exp2(c_last)
        prods = [_dot_nt(q.astype(BF16), blockdiag(k.astype(BF16)))]
        for m in LEVELS:
            if m == 1:
                xl = jnp.where(upper[m], q * jnp.exp2(logf), k)
            else:
                xl = _upper_q_lower_k(q, k, m, upper) * jnp.exp2(-jnp.abs(cum - _level_ref(cum, m)))
            xl = xl.astype(BF16)
            prods.append(_dot_nt(xl, blockdiag(xl)))
        return qd, kd, dec, prods

    def finish(r0, pair, qd, kd, dec, prods):
        c0 = pair * pw
        v = p_scr[pl.ds(r0, CHUNK), 2 * width + c0:2 * width + c0 + pw].astype(BF16)
        gate = _silu(p_scr[pl.ds(r0, CHUNK), 3 * width + c0:3 * width + c0 + pw])
        att = masks_ref[0] * prods[0]
        for li in range(1, len(prods)):
            att = att + masks_ref[li] * prods[li]
        o = _dot(att.astype(BF16), blockdiag(v))
        outs = []
        for sub in range(2):
            hs = slice(sub * HG_K, (sub + 1) * HG_K)
            st = st_scr[2 * pair + sub]
            oh = o[:, sub * HG_V:(sub + 1) * HG_V] + _dot_nt(qd[:, hs], st.astype(BF16))
            st_scr[2 * pair + sub] = st * dec[:, hs] + _dot_tn(v[:, hs], kd[:, hs])
            outs.append(oh * lax.rsqrt(jnp.mean(oh * oh, axis=-1, keepdims=True) + EPS))
        y = jnp.concatenate(outs, axis=1) * gw_ref[:, c0:c0 + pw] * gate
        o_ref[pl.ds(r0, CHUNK), c0:c0 + pw] = y.astype(o_ref.dtype)

    def body(it, carry):
        units = [(pl.multiple_of((it * CHUNKS_PER_ITER + ci) * CHUNK, CHUNK), pair)
                 for ci in range(CHUNKS_PER_ITER) for pair in range(HEADS // 2)]
        stage1 = [gates(r0, pair) for r0, pair in units]
        stage2 = [level_dots(*vals) for vals in stage1]
        for (r0, pair), vals in zip(units, stage2):
            finish(r0, pair, *vals)
        return carry

    lax.fori_loop(0, ts // (CHUNK * CHUNKS_PER_ITER), body, 0)


def _level_masks():
    t = np.arange(CHUNK)[:, None]
    s = np.arange(CHUNK)[None, :]
    masks = [(t == s)]
    for m in LEVELS:
        masks.append((t // (2 * m) == s // (2 * m)) & (t % (2 * m) >= m) & (s % (2 * m) < m))
    return jnp.asarray(np.tile(np.stack(masks).astype(np.float32), (1, 1, 2)))


def _hg_mixer(x2, nw, w, lb, gw, b, s, ts):
    width = HEADS * HG_K
    nblk = s // ts
    tril = jnp.asarray(np.tril(np.ones((CHUNK, CHUNK), np.float32)), BF16)
    masks = _level_masks()
    const = lambda shape: pl.BlockSpec(shape, lambda i, j: (0,) * len(shape))
    return pl.pallas_call(
        functools.partial(_hg_kernel, ts=ts),
        grid=(b, nblk),
        in_specs=[pl.BlockSpec((ts, D_MODEL), lambda i, j: (i * nblk + j, 0)),
                  const((1, D_MODEL)), const((D_MODEL, 4 * width)), const((1, width)), const((1, width)),
                  const((CHUNK, CHUNK)), const(masks.shape)],
        out_specs=pl.BlockSpec((ts, HEADS * HG_V), lambda i, j: (i * nblk + j, 0)),
        out_shape=jax.ShapeDtypeStruct((b * s, HEADS * HG_V), BF16),
        scratch_shapes=[pltpu.VMEM((ts, 4 * width), F32), pltpu.VMEM((HEADS, HG_V, HG_K), F32)],
        compiler_params=_params(("parallel", "arbitrary")),
        name="hgrn2_mixer",
    )(x2, nw, w, lb, gw, tril, masks)


def _ret_kernel(x_ref, nw_ref, w_ref, gw_ref, rc_ref, rs_ref, intra_ref, qdec_ref, kdec_ref, cdec_ref,
                wc_ref, qn_ref, wuq_ref, kvn_ref, wukv_ref, mc_ref, ms_ref,
                o_ref, q_ref, k_ref, v_ref, p_scr, st_scr, h_scr, kc_scr, ks_scr, *, ts):
    @pl.when(pl.program_id(1) == 0)
    def _():
        st_scr[...] = jnp.zeros_like(st_scr)

    h_scr[...] = _rms(x_ref[...], nw_ref[...]).astype(BF16)
    p_scr[...] = _dot(h_scr[...], w_ref[...])
    kc_scr[...] = rc_ref[...] * (RET_K ** -0.5)
    ks_scr[...] = rs_ref[...] * (RET_K ** -0.5)

    qk_w = HEADS * RET_K
    v_w = HEADS * RET_V


    def scores(r0, hd):
        rows = pl.ds(r0, RET_CHUNK)
        q = _rope(p_scr[rows, hd * RET_K:(hd + 1) * RET_K], rc_ref[rows, :], rs_ref[rows, :])
        k = _rope(p_scr[rows, qk_w + hd * RET_K:qk_w + (hd + 1) * RET_K], kc_scr[rows, :], ks_scr[rows, :])
        return q, k, _dot_nt(q.astype(BF16), k.astype(BF16))

    def finish(r0, hd, q, k, qk):
        rows = pl.ds(r0, RET_CHUNK)
        v = p_scr[rows, 2 * qk_w + hd * RET_V:2 * qk_w + (hd + 1) * RET_V].astype(BF16)
        gate = _silu(p_scr[rows, 2 * qk_w + v_w + hd * RET_V:2 * qk_w + v_w + (hd + 1) * RET_V])
        att = (qk * intra_ref[hd]).astype(BF16)
        st = st_scr[hd]
        o = _dot(jnp.concatenate([att, (q * qdec_ref[hd]).astype(BF16)], axis=1),
                 jnp.concatenate([v, st.astype(BF16)], axis=0))
        st_scr[hd] = st * cdec_ref[hd] + _dot_tn((k * kdec_ref[hd]).astype(BF16), v)
        o = o * lax.rsqrt(jnp.mean(o * o, axis=-1, keepdims=True) + EPS)
        y = o * gw_ref[:, hd * RET_V:(hd + 1) * RET_V] * gate
        o_ref[rows, hd * RET_V:(hd + 1) * RET_V] = y.astype(o_ref.dtype)

    def body(it, carry):
        units = [(pl.multiple_of((it * RET_CHUNKS_PER_ITER + ci) * RET_CHUNK, RET_CHUNK), hd)
                 for ci in range(RET_CHUNKS_PER_ITER) for hd in range(HEADS)]
        stage1 = [scores(r0, hd) for r0, hd in units]
        for (r0, hd), vals in zip(units, stage1):
            finish(r0, hd, *vals)
        return carry

    lax.fori_loop(0, ts // (RET_CHUNK * RET_CHUNKS_PER_ITER), body, 0)

    _mla_project(h_scr[...], wc_ref, qn_ref, wuq_ref, kvn_ref, wukv_ref, mc_ref[...], ms_ref[...],
                 q_ref, k_ref, v_ref)


def _ret_tables():
    log_gamma = jnp.log1p(-jnp.exp2(-5.0 - jnp.arange(HEADS, dtype=F32)))
    idx = jnp.arange(RET_CHUNK, dtype=F32)
    causal = jnp.tril(jnp.ones((RET_CHUNK, RET_CHUNK), dtype=bool))[None]
    lg = log_gamma[:, None, None]
    dist = jnp.where(causal, (idx[:, None] - idx[None, :])[None], 0.0)
    intra = jnp.where(causal, jnp.exp(dist * lg), 0.0)
    q_decay = jnp.exp((idx + 1.0)[None, :] * log_gamma[:, None])[..., None]
    k_decay = jnp.exp((RET_CHUNK - 1.0 - idx)[None, :] * log_gamma[:, None])[..., None]
    chunk_decay = jnp.exp(RET_CHUNK * log_gamma)[:, None, None]
    return (intra,
            jnp.broadcast_to(q_decay, (HEADS, RET_CHUNK, RET_K)),
            jnp.broadcast_to(k_decay, (HEADS, RET_CHUNK, RET_K)),
            jnp.broadcast_to(chunk_decay, (HEADS, 1, RET_V)))


def _ret_mixer(x2, nw, w, gw, rc, rs, wc, qn, wuq, kvn, wukv, mc, ms, b, s, ts):
    nblk = s // ts
    t = b * s
    intra, qdec, kdec, cdec = _ret_tables()
    n_proj = 2 * HEADS * RET_K + 2 * HEADS * RET_V
    const = lambda a: pl.BlockSpec(a.shape, lambda i, j: (0,) * a.ndim)
    rows = lambda width: pl.BlockSpec((ts, width), lambda i, j: (i * nblk + j, 0))
    out_widths = (HEADS * RET_V, HEADS * MLA_QK, HEADS * MLA_QK, HEADS * MLA_V)
    return pl.pallas_call(
        functools.partial(_ret_kernel, ts=ts),
        grid=(b, nblk),
        in_specs=[rows(D_MODEL), const(nw), const(w), const(gw), rows(LANES), rows(LANES),
                  const(intra), const(qdec), const(kdec), const(cdec),
                  const(wc), const(qn), const(wuq), const(kvn), const(wukv), rows(LANES), rows(LANES)],
        out_specs=[rows(wd) for wd in out_widths],
        out_shape=[jax.ShapeDtypeStruct((t, wd), BF16) for wd in out_widths],
        scratch_shapes=[pltpu.VMEM((ts, n_proj), F32), pltpu.VMEM((HEADS, RET_K, RET_V), F32),
                        pltpu.VMEM((ts, D_MODEL), BF16), pltpu.VMEM((ts, LANES), F32), pltpu.VMEM((ts, LANES), F32)],
        compiler_params=_params(("parallel", "arbitrary")),
        name="retention_mla_mixer",
    )(x2, nw, w, gw, rc, rs, intra, qdec, kdec, cdec, wc, qn, wuq, kvn, wukv, mc, ms)


def _mla_project(h, wc_ref, qn_ref, wuq_ref, kvn_ref, wukv_ref, cos, sin, q_ref, k_ref, v_ref):
    c = _dot(h, wc_ref[...])
    cq = _rms(c[:, 0:MLA_Q_RANK], qn_ref[...]).astype(BF16)
    qm = _dot(cq, wuq_ref[...]) * ((MLA_NOPE + MLA_ROPE) ** -0.5 * LOG2E)
    ckv = _rms(c[:, MLA_Q_RANK:MLA_Q_RANK + MLA_KV_RANK], kvn_ref[...]).astype(BF16)
    kv = _dot(ckv, wukv_ref[...])
    k_pe = _rope(c[:, MLA_Q_RANK + MLA_KV_RANK:], cos, sin).astype(BF16)
    lane = lax.broadcasted_iota(jnp.int32, k_pe.shape, 1)
    first = (lane % 64) < 32
    nope_w = HEADS * MLA_NOPE
    for pair in range(HEADS // 2):
        q_pe = _rope(qm[:, nope_w + pair * LANES:nope_w + (pair + 1) * LANES], cos, sin)
        for sub in range(2):
            hd = 2 * pair + sub
            keep = first if sub == 0 else jnp.logical_not(first)
            q_ref[:, hd * MLA_QK:hd * MLA_QK + MLA_NOPE] = qm[:, hd * MLA_NOPE:(hd + 1) * MLA_NOPE].astype(BF16)
            q_ref[:, hd * MLA_QK + MLA_NOPE:(hd + 1) * MLA_QK] = jnp.where(keep, q_pe, 0.0).astype(BF16)
    for hd in range(HEADS):
        k_ref[:, hd * MLA_QK:hd * MLA_QK + MLA_NOPE] = kv[:, hd * MLA_NOPE:(hd + 1) * MLA_NOPE].astype(BF16)
        k_ref[:, hd * MLA_QK + MLA_NOPE:(hd + 1) * MLA_QK] = k_pe
    v_ref[...] = kv[:, nope_w:].astype(BF16)


def _mla_attn_kernel(q_ref, k_ref, v_ref, o_ref, acc_scr, *, tq):
    qi = pl.program_id(1)
    key = lax.broadcasted_iota(jnp.int32, (tq, tq), 0)
    qry = lax.broadcasted_iota(jnp.int32, (tq, tq), 1)
    causal = key <= qry
    acc_scr[...] = jnp.zeros_like(acc_scr)

    def step(kj, carry, masked):
        ms, ls = carry
        k0 = pl.multiple_of(kj * tq, tq)
        new_m, new_l = [], []
        scores = [_dot_nt(k_ref[pl.ds(k0, tq), hd * MLA_QK:(hd + 1) * MLA_QK], q_ref[:, hd * MLA_QK:(hd + 1) * MLA_QK])
                  for hd in range(HEADS)]
        for hd in range(HEADS):
            s_t = scores[hd]
            if masked:
                s_t = jnp.where(causal, s_t, MASK_NEG)
            m_new = jnp.maximum(ms[hd], jnp.max(s_t, axis=0, keepdims=True))
            alpha = jnp.exp2(ms[hd] - m_new)
            p = jnp.exp2(s_t - m_new)
            new_m.append(m_new)
            new_l.append(ls[hd] * alpha + jnp.sum(p, axis=0, keepdims=True))
            pv = _dot_tn(v_ref[pl.ds(k0, tq), hd * MLA_V:(hd + 1) * MLA_V], p.astype(BF16))
            acc_scr[hd] = acc_scr[hd] * alpha + pv
        return tuple(new_m), tuple(new_l)

    init = (tuple(jnp.full((1, tq), MASK_NEG, F32) for _ in range(HEADS)),
            tuple(jnp.zeros((1, tq), F32) for _ in range(HEADS)))
    carry = lax.fori_loop(0, qi, functools.partial(step, masked=False), init)
    _, ls = step(qi, carry, True)
    for hd in range(HEADS):
        o_ref[:, hd * MLA_V:(hd + 1) * MLA_V] = jnp.transpose(acc_scr[hd] / ls[hd]).astype(o_ref.dtype)


def _mla_attn(q, k, v, b, s, tq):
    nq = s // tq
    return pl.pallas_call(
        functools.partial(_mla_attn_kernel, tq=tq),
        grid=(b, nq),
        in_specs=[pl.BlockSpec((tq, HEADS * MLA_QK), lambda i, j: (i * nq + j, 0)),
                  pl.BlockSpec((s, HEADS * MLA_QK), lambda i, j: (i, 0)),
                  pl.BlockSpec((s, HEADS * MLA_V), lambda i, j: (i, 0))],
        out_specs=pl.BlockSpec((tq, HEADS * MLA_V), lambda i, j: (i * nq + j, 0)),
        out_shape=jax.ShapeDtypeStruct((b * s, HEADS * MLA_V), BF16),
        scratch_shapes=[pltpu.VMEM((HEADS, MLA_V, tq), F32)],
        compiler_params=_params(("parallel", "arbitrary")),
        name="mla_attention",
    )(q, k, v)


def _merge_kernel(x_ref, nw_ref, wg_ref, ya_ref, yb_ref, yc_ref, wa_ref, wb_ref, wc_ref, wo_ref, o_ref):
    x = x_ref[...]
    h = _rms(x, nw_ref[...]).astype(BF16)
    merged = (_sigmoid(_dot(h, wg_ref[:, 0:D_MODEL])) * _dot(ya_ref[...], wa_ref[...])
              + _sigmoid(_dot(h, wg_ref[:, D_MODEL:2 * D_MODEL])) * _dot(yb_ref[...], wb_ref[...])
              + _sigmoid(_dot(h, wg_ref[:, 2 * D_MODEL:3 * D_MODEL])) * _dot(yc_ref[...], wc_ref[...]))
    o_ref[...] = x + _dot(merged.astype(BF16), wo_ref[...])


def _merge(x2, nw, wg, ya, yb, yc, wa, wb, wc, wo, tm):
    t = x2.shape[0]
    const = lambda a: pl.BlockSpec(a.shape, lambda i: (0,) * a.ndim)
    rows = lambda a: pl.BlockSpec((tm, a.shape[1]), lambda i: (i, 0))
    return pl.pallas_call(
        _merge_kernel,
        grid=(t // tm,),
        in_specs=[rows(x2), const(nw), const(wg), rows(ya), rows(yb), rows(yc),
                  const(wa), const(wb), const(wc), const(wo)],
        out_specs=rows(x2),
        out_shape=jax.ShapeDtypeStruct(x2.shape, F32),
        compiler_params=_params(("parallel",)),
        name="gated_merge",
    )(x2, nw, wg, ya, yb, yc, wa, wb, wc, wo)


def _ffn_kernel(x_ref, nw_ref, wi_ref, wo_ref, fw_ref, o_ref, *, final_norm):
    x = x_ref[...]
    h = _rms(x, nw_ref[...]).astype(BF16)
    g = _dot(h, wi_ref[:, 0:D_FF])
    u = _dot(h, wi_ref[:, D_FF:2 * D_FF])
    y = x + _dot((_silu(g) * u).astype(BF16), wo_ref[...])
    if final_norm:
        y = _rms(y, fw_ref[...])
    o_ref[...] = y


def _ffn(x2, nw, wi, wo, fw, tm, final_norm):
    t = x2.shape[0]
    const = lambda a: pl.BlockSpec(a.shape, lambda i: (0,) * a.ndim)
    rows = lambda a: pl.BlockSpec((tm, a.shape[1]), lambda i: (i, 0))
    return pl.pallas_call(
        functools.partial(_ffn_kernel, final_norm=final_norm),
        grid=(t // tm,),
        in_specs=[rows(x2), const(nw), const(wi), const(wo), const(fw)],
        out_specs=rows(x2),
        out_shape=jax.ShapeDtypeStruct(x2.shape, F32),
        compiler_params=_params(("parallel",)),
        name="swiglu_ffn",
    )(x2, nw, wi, wo, fw)


def _in_proj_slices(w_in_l):
    widths = (512, 512, 512, 512, 512, 512, 1024, 1024, MLA_Q_RANK, MLA_KV_RANK, MLA_ROPE, 3 * D_MODEL)
    offs = np.concatenate([[0], np.cumsum(widths)])
    cols = [w_in_l[:, int(offs[i]):int(offs[i + 1])] for i in range(len(widths))]
    w_hg = jnp.concatenate(cols[0:4], axis=1)
    w_ret = jnp.concatenate(cols[4:8], axis=1)
    kr = cols[10]
    kr1, kr2 = kr[:, :32], kr[:, 32:]
    w_c = jnp.concatenate([cols[8], cols[9], kr1, kr1, kr2, kr2], axis=1)
    return w_hg.astype(BF16), w_ret.astype(BF16), w_c.astype(BF16), cols[11].astype(BF16)


def _uq_layout(w_uq_l):
    per = MLA_NOPE + MLA_ROPE
    heads = [w_uq_l[:, hd * per:(hd + 1) * per] for hd in range(HEADS)]
    nope = [hh[:, :MLA_NOPE] for hh in heads]
    pe = [hh[:, MLA_NOPE:] for hh in heads]
    pairs = []
    for p in range(HEADS // 2):
        a, b = pe[2 * p], pe[2 * p + 1]
        pairs += [a[:, :32], b[:, :32], a[:, 32:], b[:, 32:]]
    return jnp.concatenate(nope + pairs, axis=1).astype(BF16)


def _ukv_layout(w_ukv_l):
    per = MLA_NOPE + MLA_V
    heads = [w_ukv_l[:, hd * per:(hd + 1) * per] for hd in range(HEADS)]
    return jnp.concatenate([hh[:, :MLA_NOPE] for hh in heads] + [hh[:, MLA_NOPE:] for hh in heads],
                           axis=1).astype(BF16)


def _block(n, want):
    return want if n % want == 0 else n


def kernel(x, positions, norm_mix_w, w_in, hg_lower_bounds, hg_norm_w, ret_norm_w, mla_q_norm_w, mla_w_uq, mla_kv_norm_w, mla_w_ukv, w_br_a, w_br_b, w_br_c, w_out, norm_ffn_w, w_ffn_in, w_ffn_out, final_norm_w):
    b, s, d = x.shape
    depth = w_in.shape[0]
    assert d == D_MODEL and s % (CHUNKS_PER_ITER * CHUNK) == 0
    t = b * s
    ts = _block(s, 512)
    tm = _block(t, 512)
    tq = _block(s, 512)

    lb_p = jax.nn.softmax(hg_lower_bounds.astype(F32), axis=0)
    lb_all = jnp.cumsum(lb_p, axis=0) - lb_p[0]

    rc, rs, mc, ms = _rope_tables(positions, _block(t, 512))
    row = lambda a: a.reshape(1, -1).astype(F32)

    x2 = x.reshape(t, d)
    for l in range(depth):
        w_hg, w_ret, w_c, w_g = _in_proj_slices(w_in[l])
        nw = row(norm_mix_w[l])
        ya = _hg_mixer(x2, nw, w_hg, row(lb_all[l]), row(hg_norm_w[l]), b, s, ts)
        yb, q, k, v = _ret_mixer(x2, nw, w_ret, row(ret_norm_w[l]), rc, rs, w_c, row(mla_q_norm_w[l]),
                                 _uq_layout(mla_w_uq[l]), row(mla_kv_norm_w[l]), _ukv_layout(mla_w_ukv[l]),
                                 mc, ms, b, s, ts)
        yc = _mla_attn(q, k, v, b, s, tq)
        x2 = _merge(x2, nw, w_g, ya, yb, yc, w_br_a[l].astype(BF16), w_br_b[l].astype(BF16),
                    w_br_c[l].astype(BF16), w_out[l].astype(BF16), tm)
        x2 = _ffn(x2, row(norm_ffn_w[l]), w_ffn_in[l].astype(BF16), w_ffn_out[l].astype(BF16),
                  row(final_norm_w), tm, final_norm=(l == depth - 1))
    return x2.reshape(b, s, d)
```

```python
import functools

import numpy as np
import jax
import jax.numpy as jnp
from jax import lax
from jax.experimental import pallas as pl
from jax.experimental.pallas import tpu as pltpu

F32 = jnp.float32
BF16 = jnp.bfloat16

D_MODEL = 1024
HEADS = 4
HG_K = 128
HG_V = 128
RET_K = 128
RET_V = 256
MLA_Q_RANK = 256
MLA_KV_RANK = 128
MLA_NOPE = 128
MLA_ROPE = 64
MLA_V = 128
MLA_QK = 256
D_FF = 2816
CHUNK = 64
CHUNKS_PER_ITER = 8
RET_CHUNK = 128
RET_CHUNKS_PER_ITER = 4
ROPE_BASE = 10000.0
EPS = 1e-6
EXP_CLIP = 60.0
MASK_NEG = -1e30
LOG2E = 1.4426950408889634

LANES = 128
VMEM_LIMIT = 56 * 1024 * 1024

LEVELS = (32, 16, 8, 4, 2, 1)


def _rms(x, w):
    return x * lax.rsqrt(jnp.mean(x * x, axis=-1, keepdims=True) + EPS) * w


def _sigmoid(x):
    return 0.5 + 0.5 * jnp.tanh(0.5 * x)


def _silu(x):
    return x * _sigmoid(x)


def _dot(a, b):
    return jnp.dot(a, b, preferred_element_type=F32)


def _dot_nt(a, b):
    return lax.dot_general(a, b, (((1,), (1,)), ((), ())), preferred_element_type=F32)


def _dot_tn(a, b):
    return lax.dot_general(a, b, (((0,), (0,)), ((), ())), preferred_element_type=F32)


def _params(sem):
    return pltpu.CompilerParams(dimension_semantics=sem, vmem_limit_bytes=VMEM_LIMIT)


def _rope_table_kernel(pos_ref, inv_ref, rc_ref, rs_ref, mc_ref, ms_ref):
    pos = pos_ref[...].astype(F32)
    ang = inv_ref[...] * pos
    ct = jnp.transpose(jnp.cos(ang))
    st = jnp.transpose(jnp.sin(ang))
    lo = lax.broadcasted_iota(jnp.int32, ct.shape, 1) < 64
    ct_r = pltpu.roll(ct, 64, 1)
    st_r = pltpu.roll(st, 64, 1)
    rc_ref[...] = jnp.where(lo, ct, ct_r)
    rs_ref[...] = jnp.where(lo, -st, st_r)
    mc_ref[...] = jnp.where(lo, ct_r, ct)
    ms_ref[...] = jnp.where(lo, -st_r, st)


def _rope_tables(positions, tt):
    t = positions.size
    pos = positions.reshape(1, t)
    inv64 = ROPE_BASE ** (-jnp.arange(64, dtype=F32) / 64)
    inv32 = ROPE_BASE ** (-jnp.arange(32, dtype=F32) / 32)
    inv = jnp.concatenate([inv64, inv32, inv32]).reshape(LANES, 1)
    tab = jax.ShapeDtypeStruct((t, LANES), F32)
    return pl.pallas_call(
        _rope_table_kernel,
        grid=(t // tt,),
        in_specs=[pl.BlockSpec((1, tt), lambda i: (0, i)),
                  pl.BlockSpec((LANES, 1), lambda i: (0, 0))],
        out_specs=[pl.BlockSpec((tt, LANES), lambda i: (i, 0))] * 4,
        out_shape=[tab] * 4,
        compiler_params=_params(("parallel",)),
        name="rope_tables",
    )(pos, inv)


def _rope(t, c, s):
    return t * c + pltpu.roll(t, 64, 1) * s


def _split3(x):
    a = x.astype(BF16)
    r = x - a.astype(F32)
    b = r.astype(BF16)
    c = (r - b.astype(F32)).astype(BF16)
    return a, b, c


def _level_ref(cum, m):
    w = cum.shape[1]
    if m >= 8:
        g = CHUNK // (2 * m)
        c3 = cum.reshape(g, 2 * m, w)
        return jnp.broadcast_to(c3[:, m - 1:m, :], (g, 2 * m, w)).reshape(CHUNK, w)
    c3 = cum.reshape(8, 8, w)

    def row(r):
        return jnp.broadcast_to(c3[:, r:r + 1, :], (8, 8, w))

    if m == 4:
        out = row(3)
    else:
        sub = lax.broadcasted_iota(jnp.int32, (8, 8, w), 1)
        out = jnp.where(sub < 4, row(1), row(5))
    return out.reshape(CHUNK, w)


def _upper_q_lower_k(q, k, m, upper):
    if m < 8:
        return jnp.where(upper[m], q, k)
    return jnp.concatenate([(q if blk % 2 else k)[blk * m:(blk + 1) * m] for blk in range(CHUNK // m)], axis=0)


def _hg_kernel(x_ref, nw_ref, w_ref, lb_ref, gw_ref, tril_ref, masks_ref, o_ref, p_scr, st_scr, *, ts):
    @pl.when(pl.program_id(1) == 0)
    def _():
        st_scr[...] = jnp.zeros_like(st_scr)

    h = _rms(x_ref[...], nw_ref[...]).astype(BF16)
    p_scr[...] = _dot(h, w_ref[...])

    width = HEADS * HG_K
    pw = 2 * HG_K
    tril = tril_ref[...]
    rows = lax.broadcasted_iota(jnp.int32, (CHUNK, pw), 0)
    upper = {m: (rows % (2 * m)) >= m for m in LEVELS}
    zk = jnp.zeros((CHUNK, HG_K), BF16)

    def blockdiag(t):
        return jnp.concatenate([jnp.concatenate([t[:, :HG_K], zk], axis=1),
                                jnp.concatenate([zk, t[:, HG_K:]], axis=1)], axis=0)


    def gates(r0, pair):
        c0 = pair * pw
        lb = lb_ref[:, c0:c0 + pw]
        q = _silu(p_scr[pl.ds(r0, CHUNK), c0:c0 + pw])
        hf = p_scr[pl.ds(r0, CHUNK), width + c0:width + c0 + pw]
        e = jnp.exp2(jnp.abs(hf) * -LOG2E)
        u = jnp.exp2(jnp.minimum(hf * -LOG2E, EXP_CLIP * LOG2E))
        r = 1.0 / (1.0 + e)
        logf = (jnp.minimum(hf, 0.0) + jnp.log((1.0 + lb * u) * r)) * LOG2E
        k = (1.0 - lb) * (jnp.where(hf >= 0.0, e, 1.0) * r)
        cums = _dot(tril, jnp.concatenate(_split3(logf), axis=1))
        cum = cums[:, :pw] + cums[:, pw:2 * pw] + cums[:, 2 * pw:]
        return q, k, logf, cum

    def level_dots(q, k, logf, cum):
        c_last = cum[CHUNK - 1:CHUNK, :]
        qd = (q * jnp.exp2(cum)).astype(BF16)
        kd = (k * jnp.exp2(c_last - cum)).astype(BF16)
        dec = jnp.exp2(c_last)
        prods = [_dot_nt(q.astype(BF16), blockdiag(k.astype(BF16)))]
        for m in LEVELS:
            if m == 1:
                xl = jnp.where(upper[m], q * jnp.exp2(logf), k)
            else:
                xl = _upper_q_lower_k(q, k, m, upper) * jnp.exp2(-jnp.abs(cum - _level_ref(cum, m)))
            xl = xl.astype(BF16)
            prods.append(_dot_nt(xl, blockdiag(xl)))
        return qd, kd, dec, prods

    def finish(r0, pair, qd, kd, dec, prods):
        c0 = pair * pw
        v = p_scr[pl.ds(r0, CHUNK), 2 * width + c0:2 * width + c0 + pw].astype(BF16)
        gate = _silu(p_scr[pl.ds(r0, CHUNK), 3 * width + c0:3 * width + c0 + pw])
        att = masks_ref[0] * prods[0]
        for li in range(1, len(prods)):
            att = att + masks_ref[li] * prods[li]
        o = _dot(att.astype(BF16), blockdiag(v))
        outs = []
        for sub in range(2):
            hs = slice(sub * HG_K, (sub + 1) * HG_K)
            st = st_scr[2 * pair + sub]
            oh = o[:, sub * HG_V:(sub + 1) * HG_V] + _dot_nt(qd[:, hs], st.astype(BF16))
            st_scr[2 * pair + sub] = st * dec[:, hs] + _dot_tn(v[:, hs], kd[:, hs])
            outs.append(oh * lax.rsqrt(jnp.mean(oh * oh, axis=-1, keepdims=True) + EPS))
        y = jnp.concatenate(outs, axis=1) * gw_ref[:, c0:c0 + pw] * gate
        o_ref[pl.ds(r0, CHUNK), c0:c0 + pw] = y.astype(o_ref.dtype)

    def body(it, carry):
        units = [(pl.multiple_of((it * CHUNKS_PER_ITER + ci) * CHUNK, CHUNK), pair)
                 for ci in range(CHUNKS_PER_ITER) for pair in range(HEADS // 2)]
        stage1 = [gates(r0, pair) for r0, pair in units]
        stage2 = [level_dots(*vals) for vals in stage1]
        for (r0, pair), vals in zip(units, stage2):
            finish(r0, pair, *vals)
        return carry

    lax.fori_loop(0, ts // (CHUNK * CHUNKS_PER_ITER), body, 0)


def _level_masks():
    t = np.arange(CHUNK)[:, None]
    s = np.arange(CHUNK)[None, :]
    masks = [(t == s)]
    for m in LEVELS:
        masks.append((t // (2 * m) == s // (2 * m)) & (t % (2 * m) >= m) & (s % (2 * m) < m))
    return jnp.asarray(np.tile(np.stack(masks).astype(np.float32), (1, 1, 2)))


def _hg_mixer(x2, nw, w, lb, gw, b, s, ts):
    width = HEADS * HG_K
    nblk = s // ts
    tril = jnp.asarray(np.tril(np.ones((CHUNK, CHUNK), np.float32)), BF16)
    masks = _level_masks()
    const = lambda shape: pl.BlockSpec(shape, lambda i, j: (0,) * len(shape))
    return pl.pallas_call(
        functools.partial(_hg_kernel, ts=ts),
        grid=(b, nblk),
        in_specs=[pl.BlockSpec((ts, D_MODEL), lambda i, j: (i * nblk + j, 0)),
                  const((1, D_MODEL)), const((D_MODEL, 4 * width)), const((1, width)), const((1, width)),
                  const((CHUNK, CHUNK)), const(masks.shape)],
        out_specs=pl.BlockSpec((ts, HEADS * HG_V), lambda i, j: (i * nblk + j, 0)),
        out_shape=jax.ShapeDtypeStruct((b * s, HEADS * HG_V), BF16),
        scratch_shapes=[pltpu.VMEM((ts, 4 * width), F32), pltpu.VMEM((HEADS, HG_V, HG_K), F32)],
        compiler_params=_params(("parallel", "arbitrary")),
        name="hgrn2_mixer",
    )(x2, nw, w, lb, gw, tril, masks)


def _ret_kernel(x_ref, nw_ref, w_ref, gw_ref, rc_ref, rs_ref, intra_ref, qdec_ref, kdec_ref, cdec_ref,
                wc_ref, qn_ref, wuq_ref, kvn_ref, wukv_ref, mc_ref, ms_ref,
                o_ref, q_ref, k_ref, v_ref, p_scr, st_scr, h_scr, kc_scr, ks_scr, *, ts):
    @pl.when(pl.program_id(1) == 0)
    def _():
        st_scr[...] = jnp.zeros_like(st_scr)

    h_scr[...] = _rms(x_ref[...], nw_ref[...]).astype(BF16)
    p_scr[...] = _dot(h_scr[...], w_ref[...])
    kc_scr[...] = rc_ref[...] * (RET_K ** -0.5)
    ks_scr[...] = rs_ref[...] * (RET_K ** -0.5)

    qk_w = HEADS * RET_K
    v_w = HEADS * RET_V


    def scores(r0, hd):
        rows = pl.ds(r0, RET_CHUNK)
        q = _rope(p_scr[rows, hd * RET_K:(hd + 1) * RET_K], rc_ref[rows, :], rs_ref[rows, :])
        k = _rope(p_scr[rows, qk_w + hd * RET_K:qk_w + (hd + 1) * RET_K], kc_scr[rows, :], ks_scr[rows, :])
        return q, k, _dot_nt(q.astype(BF16), k.astype(BF16))

    def finish(r0, hd, q, k, qk):
        rows = pl.ds(r0, RET_CHUNK)
        v = p_scr[rows, 2 * qk_w + hd * RET_V:2 * qk_w + (hd + 1) * RET_V].astype(BF16)
        gate = _silu(p_scr[rows, 2 * qk_w + v_w + hd * RET_V:2 * qk_w + v_w + (hd + 1) * RET_V])
        att = (qk * intra_ref[hd]).astype(BF16)
        st = st_scr[hd]
        o = _dot(jnp.concatenate([att, (q * qdec_ref[hd]).astype(BF16)], axis=1),
                 jnp.concatenate([v, st.astype(BF16)], axis=0))
        st_scr[hd] = st * cdec_ref[hd] + _dot_tn((k * kdec_ref[hd]).astype(BF16), v)
        o = o * lax.rsqrt(jnp.mean(o * o, axis=-1, keepdims=True) + EPS)
        y = o * gw_ref[:, hd * RET_V:(hd + 1) * RET_V] * gate
        o_ref[rows, hd * RET_V:(hd + 1) * RET_V] = y.astype(o_ref.dtype)

    def body(it, carry):
        units = [(pl.multiple_of((it * RET_CHUNKS_PER_ITER + ci) * RET_CHUNK, RET_CHUNK), hd)
                 for ci in range(RET_CHUNKS_PER_ITER) for hd in range(HEADS)]
        stage1 = [scores(r0, hd) for r0, hd in units]
        for (r0, hd), vals in zip(units, stage1):
            finish(r0, hd, *vals)
        return carry

    lax.fori_loop(0, ts // (RET_CHUNK * RET_CHUNKS_PER_ITER), body, 0)

    _mla_project(h_scr[...], wc_ref, qn_ref, wuq_ref, kvn_ref, wukv_ref, mc_ref[...], ms_ref[...],
                 q_ref, k_ref, v_ref)


def _ret_tables():
    log_gamma = jnp.log1p(-jnp.exp2(-5.0 - jnp.arange(HEADS, dtype=F32)))
    idx = jnp.arange(RET_CHUNK, dtype=F32)
    causal = jnp.tril(jnp.ones((RET_CHUNK, RET_CHUNK), dtype=bool))[None]
    lg = log_gamma[:, None, None]
    dist = jnp.where(causal, (idx[:, None] - idx[None, :])[None], 0.0)
    intra = jnp.where(causal, jnp.exp(dist * lg), 0.0)
    q_decay = jnp.exp((idx + 1.0)[None, :] * log_gamma[:, None])[..., None]
    k_decay = jnp.exp((RET_CHUNK - 1.0 - idx)[None, :] * log_gamma[:, None])[..., None]
    chunk_decay = jnp.exp(RET_CHUNK * log_gamma)[:, None, None]
    return (intra,
            jnp.broadcast_to(q_decay, (HEADS, RET_CHUNK, RET_K)),
            jnp.broadcast_to(k_decay, (HEADS, RET_CHUNK, RET_K)),
            jnp.broadcast_to(chunk_decay, (HEADS, 1, RET_V)))


def _ret_mixer(x2, nw, w, gw, rc, rs, wc, qn, wuq, kvn, wukv, mc, ms, b, s, ts):
    nblk = s // ts
    t = b * s
    intra, qdec, kdec, cdec = _ret_tables()
    n_proj = 2 * HEADS * RET_K + 2 * HEADS * RET_V
    const = lambda a: pl.BlockSpec(a.shape, lambda i, j: (0,) * a.ndim)
    rows = lambda width: pl.BlockSpec((ts, width), lambda i, j: (i * nblk + j, 0))
    out_widths = (HEADS * RET_V, HEADS * MLA_QK, HEADS * MLA_QK, HEADS * MLA_V)
    return pl.pallas_call(
        functools.partial(_ret_kernel, ts=ts),
        grid=(b, nblk),
        in_specs=[rows(D_MODEL), const(nw), const(w), const(gw), rows(LANES), rows(LANES),
                  const(intra), const(qdec), const(kdec), const(cdec),
                  const(wc), const(qn), const(wuq), const(kvn), const(wukv), rows(LANES), rows(LANES)],
        out_specs=[rows(wd) for wd in out_widths],
        out_shape=[jax.ShapeDtypeStruct((t, wd), BF16) for wd in out_widths],
        scratch_shapes=[pltpu.VMEM((ts, n_proj), F32), pltpu.VMEM((HEADS, RET_K, RET_V), F32),
                        pltpu.VMEM((ts, D_MODEL), BF16), pltpu.VMEM((ts, LANES), F32), pltpu.VMEM((ts, LANES), F32)],
        compiler_params=_params(("parallel", "arbitrary")),
        name="retention_mla_mixer",
    )(x2, nw, w, gw, rc, rs, intra, qdec, kdec, cdec, wc, qn, wuq, kvn, wukv, mc, ms)


def _mla_project(h, wc_ref, qn_ref, wuq_ref, kvn_ref, wukv_ref, cos, sin, q_ref, k_ref, v_ref):
    c = _dot(h, wc_ref[...])
    cq = _rms(c[:, 0:MLA_Q_RANK], qn_ref[...]).astype(BF16)
    qm = _dot(cq, wuq_ref[...]) * ((MLA_NOPE + MLA_ROPE) ** -0.5 * LOG2E)
    ckv = _rms(c[:, MLA_Q_RANK:MLA_Q_RANK + MLA_KV_RANK], kvn_ref[...]).astype(BF16)
    kv = _dot(ckv, wukv_ref[...])
    k_pe = _rope(c[:, MLA_Q_RANK + MLA_KV_RANK:], cos, sin).astype(BF16)
    lane = lax.broadcasted_iota(jnp.int32, k_pe.shape, 1)
    first = (lane % 64) < 32
    nope_w = HEADS * MLA_NOPE
    for pair in range(HEADS // 2):
        q_pe = _rope(qm[:, nope_w + pair * LANES:nope_w + (pair + 1) * LANES], cos, sin)
        for sub in range(2):
            hd = 2 * pair + sub
            keep = first if sub == 0 else jnp.logical_not(first)
            q_ref[:, hd * MLA_QK:hd * MLA_QK + MLA_NOPE] = qm[:, hd * MLA_NOPE:(hd + 1) * MLA_NOPE].astype(BF16)
            q_ref[:, hd * MLA_QK + MLA_NOPE:(hd + 1) * MLA_QK] = jnp.where(keep, q_pe, 0.0).astype(BF16)
    for hd in range(HEADS):
        k_ref[:, hd * MLA_QK:hd * MLA_QK + MLA_NOPE] = kv[:, hd * MLA_NOPE:(hd + 1) * MLA_NOPE].astype(BF16)
        k_ref[:, hd * MLA_QK + MLA_NOPE:(hd + 1) * MLA_QK] = k_pe
    v_ref[...] = kv[:, nope_w:].astype(BF16)


def _mla_attn_kernel(q_ref, k_ref, v_ref, o_ref, acc_scr, *, tq):
    qi = pl.program_id(1)
    key = lax.broadcasted_iota(jnp.int32, (tq, tq), 0)
    qry = lax.broadcasted_iota(jnp.int32, (tq, tq), 1)
    causal = key <= qry
    acc_scr[...] = jnp.zeros_like(acc_scr)

    def step(kj, carry, masked):
        ms, ls = carry
        k0 = pl.multiple_of(kj * tq, tq)
        new_m, new_l = [], []
        scores = [_dot_nt(k_ref[pl.ds(k0, tq), hd * MLA_QK:(hd + 1) * MLA_QK], q_ref[:, hd * MLA_QK:(hd + 1) * MLA_QK])
                  for hd in range(HEADS)]
        for hd in range(HEADS):
            s_t = scores[hd]
            if masked:
                s_t = jnp.where(causal, s_t, MASK_NEG)
            m_new = jnp.maximum(ms[hd], jnp.max(s_t, axis=0, keepdims=True))
            alpha = jnp.exp2(ms[hd] - m_new)
            p = jnp.exp2(s_t - m_new)
            new_m.append(m_new)
            new_l.append(ls[hd] * alpha + jnp.sum(p, axis=0, keepdims=True))
            pv = _dot_tn(v_ref[pl.ds(k0, tq), hd * MLA_V:(hd + 1) * MLA_V], p.astype(BF16))
            acc_scr[hd] = acc_scr[hd] * alpha + pv
        return tuple(new_m), tuple(new_l)

    init = (tuple(jnp.full((1, tq), MASK_NEG, F32) for _ in range(HEADS)),
            tuple(jnp.zeros((1, tq), F32) for _ in range(HEADS)))
    carry = lax.fori_loop(0, qi, functools.partial(step, masked=False), init)
    _, ls = step(qi, carry, True)
    for hd in range(HEADS):
        o_ref[:, hd * MLA_V:(hd + 1) * MLA_V] = jnp.transpose(acc_scr[hd] / ls[hd]).astype(o_ref.dtype)


def _mla_attn(q, k, v, b, s, tq):
    nq = s // tq
    return pl.pallas_call(
        functools.partial(_mla_attn_kernel, tq=tq),
        grid=(b, nq),
        in_specs=[pl.BlockSpec((tq, HEADS * MLA_QK), lambda i, j: (i * nq + j, 0)),
                  pl.BlockSpec((s, HEADS * MLA_QK), lambda i, j: (i, 0)),
                  pl.BlockSpec((s, HEADS * MLA_V), lambda i, j: (i, 0))],
        out_specs=pl.BlockSpec((tq, HEADS * MLA_V), lambda i, j: (i * nq + j, 0)),
        out_shape=jax.ShapeDtypeStruct((b * s, HEADS * MLA_V), BF16),
        scratch_shapes=[pltpu.VMEM((HEADS, MLA_V, tq), F32)],
        compiler_params=_params(("parallel", "arbitrary")),
        name="mla_attention",
    )(q, k, v)


def _merge_kernel(x_ref, nw_ref, wg_ref, ya_ref, yb_ref, yc_ref, wa_ref, wb_ref, wc_ref, wo_ref, o_ref):
    x = x_ref[...]
    h = _rms(x, nw_ref[...]).astype(BF16)
    merged = (_sigmoid(_dot(h, wg_ref[:, 0:D_MODEL])) * _dot(ya_ref[...], wa_ref[...])
              + _sigmoid(_dot(h, wg_ref[:, D_MODEL:2 * D_MODEL])) * _dot(yb_ref[...], wb_ref[...])
              + _sigmoid(_dot(h, wg_ref[:, 2 * D_MODEL:3 * D_MODEL])) * _dot(yc_ref[...], wc_ref[...]))
    o_ref[...] = x + _dot(merged.astype(BF16), wo_ref[...])


def _merge(x2, nw, wg, ya, yb, yc, wa, wb, wc, wo, tm):
    t = x2.shape[0]
    const = lambda a: pl.BlockSpec(a.shape, lambda i: (0,) * a.ndim)
    rows = lambda a: pl.BlockSpec((tm, a.shape[1]), lambda i: (i, 0))
    return pl.pallas_call(
        _merge_kernel,
        grid=(t // tm,),
        in_specs=[rows(x2), const(nw), const(wg), rows(ya), rows(yb), rows(yc),
                  const(wa), const(wb), const(wc), const(wo)],
        out_specs=rows(x2),
        out_shape=jax.ShapeDtypeStruct(x2.shape, F32),
        compiler_params=_params(("parallel",)),
        name="gated_merge",
    )(x2, nw, wg, ya, yb, yc, wa, wb, wc, wo)


def _ffn_kernel(x_ref, nw_ref, wi_ref, wo_ref, fw_ref, o_ref, *, final_norm):
    x = x_ref[...]
    h = _rms(x, nw_ref[...]).astype(BF16)
    g = _dot(h, wi_ref[:, 0:D_FF])
    u = _dot(h, wi_ref[:, D_FF:2 * D_FF])
    y = x + _dot((_silu(g) * u).astype(BF16), wo_ref[...])
    if final_norm:
        y = _rms(y, fw_ref[...])
    o_ref[...] = y


def _ffn(x2, nw, wi, wo, fw, tm, final_norm):
    t = x2.shape[0]
    const = lambda a: pl.BlockSpec(a.shape, lambda i: (0,) * a.ndim)
    rows = lambda a: pl.BlockSpec((tm, a.shape[1]), lambda i: (i, 0))
    return pl.pallas_call(
        functools.partial(_ffn_kernel, final_norm=final_norm),
        grid=(t // tm,),
        in_specs=[rows(x2), const(nw), const(wi), const(wo), const(fw)],
        out_specs=rows(x2),
        out_shape=jax.ShapeDtypeStruct(x2.shape, F32),
        compiler_params=_params(("parallel",)),
        name="swiglu_ffn",
    )(x2, nw, wi, wo, fw)


def _in_proj_slices(w_in_l):
    widths = (512, 512, 512, 512, 512, 512, 1024, 1024, MLA_Q_RANK, MLA_KV_RANK, MLA_ROPE, 3 * D_MODEL)
    offs = np.concatenate([[0], np.cumsum(widths)])
    cols = [w_in_l[:, int(offs[i]):int(offs[i + 1])] for i in range(len(widths))]
    w_hg = jnp.concatenate(cols[0:4], axis=1)
    w_ret = jnp.concatenate(cols[4:8], axis=1)
    kr = cols[10]
    kr1, kr2 = kr[:, :32], kr[:, 32:]
    w_c = jnp.concatenate([cols[8], cols[9], kr1, kr1, kr2, kr2], axis=1)
    return w_hg.astype(BF16), w_ret.astype(BF16), w_c.astype(BF16), cols[11].astype(BF16)


def _uq_layout(w_uq_l):
    per = MLA_NOPE + MLA_ROPE
    heads = [w_uq_l[:, hd * per:(hd + 1) * per] for hd in range(HEADS)]
    nope = [hh[:, :MLA_NOPE] for hh in heads]
    pe = [hh[:, MLA_NOPE:] for hh in heads]
    pairs = []
    for p in range(HEADS // 2):
        a, b = pe[2 * p], pe[2 * p + 1]
        pairs += [a[:, :32], b[:, :32], a[:, 32:], b[:, 32:]]
    return jnp.concatenate(nope + pairs, axis=1).astype(BF16)


def _ukv_layout(w_ukv_l):
    per = MLA_NOPE + MLA_V
    heads = [w_ukv_l[:, hd * per:(hd + 1) * per] for hd in range(HEADS)]
    return jnp.concatenate([hh[:, :MLA_NOPE] for hh in heads] + [hh[:, MLA_NOPE:] for hh in heads],
                           axis=1).astype(BF16)


def _block(n, want):
    return want if n % want == 0 else n


def kernel(x, positions, norm_mix_w, w_in, hg_lower_bounds, hg_norm_w, ret_norm_w, mla_q_norm_w, mla_w_uq, mla_kv_norm_w, mla_w_ukv, w_br_a, w_br_b, w_br_c, w_out, norm_ffn_w, w_ffn_in, w_ffn_out, final_norm_w):
    b, s, d = x.shape
    depth = w_in.shape[0]
    assert d == D_MODEL and s % (CHUNKS_PER_ITER * CHUNK) == 0
    t = b * s
    ts = _block(s, 512)
    tm = _block(t, 512)
    tq = _block(s, 512)

    lb_p = jax.nn.softmax(hg_lower_bounds.astype(F32), axis=0)
    lb_all = jnp.cumsum(lb_p, axis=0) - lb_p[0]

    rc, rs, mc, ms = _rope_tables(positions, _block(t, 512))
    row = lambda a: a.reshape(1, -1).astype(F32)

    x2 = x.reshape(t, d)
    for l in range(depth):
        w_hg, w_ret, w_c, w_g = _in_proj_slices(w_in[l])
        nw = row(norm_mix_w[l])
        ya = _hg_mixer(x2, nw, w_hg, row(lb_all[l]), row(hg_norm_w[l]), b, s, ts)
        yb, q, k, v = _ret_mixer(x2, nw, w_ret, row(ret_norm_w[l]), rc, rs, w_c, row(mla_q_norm_w[l]),
                                 _uq_layout(mla_w_uq[l]), row(mla_kv_norm_w[l]), _ukv_layout(mla_w_ukv[l]),
                                 mc, ms, b, s, ts)
        yc = _mla_attn(q, k, v, b, s, tq)
        x2 = _merge(x2, nw, w_g, ya, yb, yc, w_br_a[l].astype(BF16), w_br_b[l].astype(BF16),
                    w_br_c[l].astype(BF16), w_out[l].astype(BF16), tm)
        x2 = _ffn(x2, row(norm_ffn_w[l]), w_ffn_in[l].astype(BF16), w_ffn_out[l].astype(BF16),
                  row(final_norm_w), tm, final_norm=(l == depth - 1))
    return x2.reshape(b, s, d)
```

```python
import functools

import numpy as np
import jax
import jax.numpy as jnp
from jax import lax
from jax.experimental import pallas as pl
from jax.experimental.pallas import tpu as pltpu

F32 = jnp.float32
BF16 = jnp.bfloat16

D_MODEL = 1024
HEADS = 4
HG_K = 128
HG_V = 128
RET_K = 128
RET_V = 256
MLA_Q_RANK = 256
MLA_KV_RANK = 128
MLA_NOPE = 128
MLA_ROPE = 64
MLA_V = 128
MLA_QK = 256
D_FF = 2816
CHUNK = 64
CHUNKS_PER_ITER = 8
RET_CHUNK = 128
RET_CHUNKS_PER_ITER = 4
ROPE_BASE = 10000.0
EPS = 1e-6
EXP_CLIP = 60.0
MASK_NEG = -1e30
LOG2E = 1.4426950408889634

LANES = 128
VMEM_LIMIT = 56 * 1024 * 1024

LEVELS = (32, 16, 8, 4, 2, 1)


def _rms(x, w):
    return x * lax.rsqrt(jnp.mean(x * x, axis=-1, keepdims=True) + EPS) * w


def _sigmoid(x):
    return 0.5 + 0.5 * jnp.tanh(0.5 * x)


def _silu(x):
    return x * _sigmoid(x)


def _dot(a, b):
    return jnp.dot(a, b, preferred_element_type=F32)


def _dot_nt(a, b):
    return lax.dot_general(a, b, (((1,), (1,)), ((), ())), preferred_element_type=F32)


def _dot_tn(a, b):
    return lax.dot_general(a, b, (((0,), (0,)), ((), ())), preferred_element_type=F32)


def _params(sem):
    return pltpu.CompilerParams(dimension_semantics=sem, vmem_limit_bytes=VMEM_LIMIT)


def _rope_table_kernel(pos_ref, inv_ref, rc_ref, rs_ref, mc_ref, ms_ref):
    pos = pos_ref[...].astype(F32)
    ang = inv_ref[...] * pos
    ct = jnp.transpose(jnp.cos(ang))
    st = jnp.transpose(jnp.sin(ang))
    lo = lax.broadcasted_iota(jnp.int32, ct.shape, 1) < 64
    ct_r = pltpu.roll(ct, 64, 1)
    st_r = pltpu.roll(st, 64, 1)
    rc_ref[...] = jnp.where(lo, ct, ct_r)
    rs_ref[...] = jnp.where(lo, -st, st_r)
    mc_ref[...] = jnp.where(lo, ct_r, ct)
    ms_ref[...] = jnp.where(lo, -st_r, st)


def _rope_tables(positions, tt):
    t = positions.size
    pos = positions.reshape(1, t)
    inv64 = ROPE_BASE ** (-jnp.arange(64, dtype=F32) / 64)
    inv32 = ROPE_BASE ** (-jnp.arange(32, dtype=F32) / 32)
    inv = jnp.concatenate([inv64, inv32, inv32]).reshape(LANES, 1)
    tab = jax.ShapeDtypeStruct((t, LANES), F32)
    return pl.pallas_call(
        _rope_table_kernel,
        grid=(t // tt,),
        in_specs=[pl.BlockSpec((1, tt), lambda i: (0, i)),
                  pl.BlockSpec((LANES, 1), lambda i: (0, 0))],
        out_specs=[pl.BlockSpec((tt, LANES), lambda i: (i, 0))] * 4,
        out_shape=[tab] * 4,
        compiler_params=_params(("parallel",)),
        name="rope_tables",
    )(pos, inv)


def _rope(t, c, s):
    return t * c + pltpu.roll(t, 64, 1) * s


def _split3(x):
    a = x.astype(BF16)
    r = x - a.astype(F32)
    b = r.astype(BF16)
    c = (r - b.astype(F32)).astype(BF16)
    return a, b, c


def _level_ref(cum, m):
    w = cum.shape[1]
    if m >= 8:
        g = CHUNK // (2 * m)
        c3 = cum.reshape(g, 2 * m, w)
        return jnp.broadcast_to(c3[:, m - 1:m, :], (g, 2 * m, w)).reshape(CHUNK, w)
    c3 = cum.reshape(8, 8, w)

    def row(r):
        return jnp.broadcast_to(c3[:, r:r + 1, :], (8, 8, w))

    if m == 4:
        out = row(3)
    else:
        sub = lax.broadcasted_iota(jnp.int32, (8, 8, w), 1)
        out = jnp.where(sub < 4, row(1), row(5))
    return out.reshape(CHUNK, w)


def _upper_q_lower_k(q, k, m, upper):
    if m < 8:
        return jnp.where(upper[m], q, k)
    return jnp.concatenate([(q if blk % 2 else k)[blk * m:(blk + 1) * m] for blk in range(CHUNK // m)], axis=0)


def _hg_kernel(x_ref, nw_ref, w_ref, lb_ref, gw_ref, tril_ref, masks_ref, o_ref, p_scr, st_scr, *, ts):
    @pl.when(pl.program_id(1) == 0)
    def _():
        st_scr[...] = jnp.zeros_like(st_scr)

    h = _rms(x_ref[...], nw_ref[...]).astype(BF16)
    p_scr[...] = _dot(h, w_ref[...])

    width = HEADS * HG_K
    pw = 2 * HG_K
    tril = tril_ref[...]
    rows = lax.broadcasted_iota(jnp.int32, (CHUNK, pw), 0)
    upper = {m: (rows % (2 * m)) >= m for m in LEVELS}
    zk = jnp.zeros((CHUNK, HG_K), BF16)

    def blockdiag(t):
        return jnp.concatenate([jnp.concatenate([t[:, :HG_K], zk], axis=1),
                                jnp.concatenate([zk, t[:, HG_K:]], axis=1)], axis=0)


    def gates(r0, pair):
        c0 = pair * pw
        lb = lb_ref[:, c0:c0 + pw]
        q = _silu(p_scr[pl.ds(r0, CHUNK), c0:c0 + pw])
        hf = p_scr[pl.ds(r0, CHUNK), width + c0:width + c0 + pw]
        e = jnp.exp2(jnp.abs(hf) * -LOG2E)
        u = jnp.exp2(jnp.minimum(hf * -LOG2E, EXP_CLIP * LOG2E))
        r = 1.0 / (1.0 + e)
        logf = (jnp.minimum(hf, 0.0) + jnp.log((1.0 + lb * u) * r)) * LOG2E
        k = (1.0 - lb) * (jnp.where(hf >= 0.0, e, 1.0) * r)
        cums = _dot(tril, jnp.concatenate(_split3(logf), axis=1))
        cum = cums[:, :pw] + cums[:, pw:2 * pw] + cums[:, 2 * pw:]
        return q, k, logf, cum

    def level_dots(q, k, logf, cum):
        c_last = cum[CHUNK - 1:CHUNK, :]
        qd = (q * jnp.exp2(cum)).astype(BF16)
        kd = (k * jnp.exp2(c_last - cum)).astype(BF16)
        dec = jnp.exp2(c_last)
        prods = [_dot_nt(q.astype(BF16), blockdiag(k.astype(BF16)))]
        for m in LEVELS:
            if m == 1:
                xl = jnp.where(upper[m], q * jnp.exp2(logf), k)
            else:
                xl = _upper_q_lower_k(q, k, m, upper) * jnp.exp2(-jnp.abs(cum - _level_ref(cum, m)))
            xl = xl.astype(BF16)
            prods.append(_dot_nt(xl, blockdiag(xl)))
        return qd, kd, dec, prods

    def finish(r0, pair, qd, kd, dec, prods):
        c0 = pair * pw
        v = p_scr[pl.ds(r0, CHUNK), 2 * width + c0:2 * width + c0 + pw].astype(BF16)
        gate = _silu(p_scr[pl.ds(r0, CHUNK), 3 * width + c0:3 * width + c0 + pw])
        att = masks_ref[0] * prods[0]
        for li in range(1, len(prods)):
            att = att + masks_ref[li] * prods[li]
        o = _dot(att.astype(BF16), blockdiag(v))
        outs = []
        for sub in range(2):
            hs = slice(sub * HG_K, (sub + 1) * HG_K)
            st = st_scr[2 * pair + sub]
            oh = o[:, sub * HG_V:(sub + 1) * HG_V] + _dot_nt(qd[:, hs], st.astype(BF16))
            st_scr[2 * pair + sub] = st * dec[:, hs] + _dot_tn(v[:, hs], kd[:, hs])
            outs.append(oh * lax.rsqrt(jnp.mean(oh * oh, axis=-1, keepdims=True) + EPS))
        y = jnp.concatenate(outs, axis=1) * gw_ref[:, c0:c0 + pw] * gate
        o_ref[pl.ds(r0, CHUNK), c0:c0 + pw] = y.astype(o_ref.dtype)

    def body(it, carry):
        units = [(pl.multiple_of((it * CHUNKS_PER_ITER + ci) * CHUNK, CHUNK), pair)
                 for ci in range(CHUNKS_PER_ITER) for pair in range(HEADS // 2)]
        stage1 = [gates(r0, pair) for r0, pair in units]
        stage2 = [level_dots(*vals) for vals in stage1]
        for (r0, pair), vals in zip(units, stage2):
            finish(r0, pair, *vals)
        return carry

    lax.fori_loop(0, ts // (CHUNK * CHUNKS_PER_ITER), body, 0)


def _level_masks():
    t = np.arange(CHUNK)[:, None]
    s = np.arange(CHUNK)[None, :]
    masks = [(t == s)]
    for m in LEVELS:
        masks.append((t // (2 * m) == s // (2 * m)) & (t % (2 * m) >= m) & (s % (2 * m) < m))
    return jnp.asarray(np.tile(np.stack(masks).astype(np.float32), (1, 1, 2)))


def _hg_mixer(x2, nw, w, lb, gw, b, s, ts):
    width = HEADS * HG_K
    nblk = s // ts
    tril = jnp.asarray(np.tril(np.ones((CHUNK, CHUNK), np.float32)), BF16)
    masks = _level_masks()
    const = lambda shape: pl.BlockSpec(shape, lambda i, j: (0,) * len(shape))
    return pl.pallas_call(
        functools.partial(_hg_kernel, ts=ts),
        grid=(b, nblk),
        in_specs=[pl.BlockSpec((ts, D_MODEL), lambda i, j: (i * nblk + j, 0)),
                  const((1, D_MODEL)), const((D_MODEL, 4 * width)), const((1, width)), const((1, width)),
                  const((CHUNK, CHUNK)), const(masks.shape)],
        out_specs=pl.BlockSpec((ts, HEADS * HG_V), lambda i, j: (i * nblk + j, 0)),
        out_shape=jax.ShapeDtypeStruct((b * s, HEADS * HG_V), BF16),
        scratch_shapes=[pltpu.VMEM((ts, 4 * width), F32), pltpu.VMEM((HEADS, HG_V, HG_K), F32)],
        compiler_params=_params(("parallel", "arbitrary")),
        name="hgrn2_mixer",
    )(x2, nw, w, lb, gw, tril, masks)


def _ret_kernel(x_ref, nw_ref, w_ref, gw_ref, rc_ref, rs_ref, intra_ref, qdec_ref, kdec_ref, cdec_ref,
                wc_ref, qn_ref, wuq_ref, kvn_ref, wukv_ref, mc_ref, ms_ref,
                o_ref, q_ref, k_ref, v_ref, p_scr, st_scr, h_scr, kc_scr, ks_scr, *, ts):
    @pl.when(pl.program_id(1) == 0)
    def _():
        st_scr[...] = jnp.zeros_like(st_scr)

    h_scr[...] = _rms(x_ref[...], nw_ref[...]).astype(BF16)
    p_scr[...] = _dot(h_scr[...], w_ref[...])
    kc_scr[...] = rc_ref[...] * (RET_K ** -0.5)
    ks_scr[...] = rs_ref[...] * (RET_K ** -0.5)

    qk_w = HEADS * RET_K
    v_w = HEADS * RET_V


    def scores(r0, hd):
        rows = pl.ds(r0, RET_CHUNK)
        q = _rope(p_scr[rows, hd * RET_K:(hd + 1) * RET_K], rc_ref[rows, :], rs_ref[rows, :])
        k = _rope(p_scr[rows, qk_w + hd * RET_K:qk_w + (hd + 1) * RET_K], kc_scr[rows, :], ks_scr[rows, :])
        return q, k, _dot_nt(q.astype(BF16), k.astype(BF16))

    def finish(r0, hd, q, k, qk):
        rows = pl.ds(r0, RET_CHUNK)
        v = p_scr[rows, 2 * qk_w + hd * RET_V:2 * qk_w + (hd + 1) * RET_V].astype(BF16)
        gate = _silu(p_scr[rows, 2 * qk_w + v_w + hd * RET_V:2 * qk_w + v_w + (hd + 1) * RET_V])
        att = (qk * intra_ref[hd]).astype(BF16)
        st = st_scr[hd]
        o = _dot(jnp.concatenate([att, (q * qdec_ref[hd]).astype(BF16)], axis=1),
                 jnp.concatenate([v, st.astype(BF16)], axis=0))
        st_scr[hd] = st * cdec_ref[hd] + _dot_tn((k * kdec_ref[hd]).astype(BF16), v)
        o = o * lax.rsqrt(jnp.mean(o * o, axis=-1, keepdims=True) + EPS)
        y = o * gw_ref[:, hd * RET_V:(hd + 1) * RET_V] * gate
        o_ref[rows, hd * RET_V:(hd + 1) * RET_V] = y.astype(o_ref.dtype)

    def body(it, carry):
        units = [(pl.multiple_of((it * RET_CHUNKS_PER_ITER + ci) * RET_CHUNK, RET_CHUNK), hd)
                 for ci in range(RET_CHUNKS_PER_ITER) for hd in range(HEADS)]
        stage1 = [scores(r0, hd) for r0, hd in units]
        for (r0, hd), vals in zip(units, stage1):
            finish(r0, hd, *vals)
        return carry

    lax.fori_loop(0, ts // (RET_CHUNK * RET_CHUNKS_PER_ITER), body, 0)

    _mla_project(h_scr[...], wc_ref, qn_ref, wuq_ref, kvn_ref, wukv_ref, mc_ref[...], ms_ref[...],
                 q_ref, k_ref, v_ref)


def _ret_tables():
    log_gamma = jnp.log1p(-jnp.exp2(-5.0 - jnp.arange(HEADS, dtype=F32)))
    idx = jnp.arange(RET_CHUNK, dtype=F32)
    causal = jnp.tril(jnp.ones((RET_CHUNK, RET_CHUNK), dtype=bool))[None]
    lg = log_gamma[:, None, None]
    dist = jnp.where(causal, (idx[:, None] - idx[None, :])[None], 0.0)
    intra = jnp.where(causal, jnp.exp(dist * lg), 0.0)
    q_decay = jnp.exp((idx + 1.0)[None, :] * log_gamma[:, None])[..., None]
    k_decay = jnp.exp((RET_CHUNK - 1.0 - idx)[None, :] * log_gamma[:, None])[..., None]
    chunk_decay = jnp.exp(RET_CHUNK * log_gamma)[:, None, None]
    return (intra,
            jnp.broadcast_to(q_decay, (HEADS, RET_CHUNK, RET_K)),
            jnp.broadcast_to(k_decay, (HEADS, RET_CHUNK, RET_K)),
            jnp.broadcast_to(chunk_decay, (HEADS, 1, RET_V)))


def _ret_mixer(x2, nw, w, gw, rc, rs, wc, qn, wuq, kvn, wukv, mc, ms, b, s, ts):
    nblk = s // ts
    t = b * s
    intra, qdec, kdec, cdec = _ret_tables()
    n_proj = 2 * HEADS * RET_K + 2 * HEADS * RET_V
    const = lambda a: pl.BlockSpec(a.shape, lambda i, j: (0,) * a.ndim)
    rows = lambda width: pl.BlockSpec((ts, width), lambda i, j: (i * nblk + j, 0))
    out_widths = (HEADS * RET_V, HEADS * MLA_QK, HEADS * MLA_QK, HEADS * MLA_V)
    return pl.pallas_call(
        functools.partial(_ret_kernel, ts=ts),
        grid=(b, nblk),
        in_specs=[rows(D_MODEL), const(nw), const(w), const(gw), rows(LANES), rows(LANES),
                  const(intra), const(qdec), const(kdec), const(cdec),
                  const(wc), const(qn), const(wuq), const(kvn), const(wukv), rows(LANES), rows(LANES)],
        out_specs=[rows(wd) for wd in out_widths],
        out_shape=[jax.ShapeDtypeStruct((t, wd), BF16) for wd in out_widths],
        scratch_shapes=[pltpu.VMEM((ts, n_proj), F32), pltpu.VMEM((HEADS, RET_K, RET_V), F32),
                        pltpu.VMEM((ts, D_MODEL), BF16), pltpu.VMEM((ts, LANES), F32), pltpu.VMEM((ts, LANES), F32)],
        compiler_params=_params(("parallel", "arbitrary")),
        name="retention_mla_mixer",
    )(x2, nw, w, gw, rc, rs, intra, qdec, kdec, cdec, wc, qn, wuq, kvn, wukv, mc, ms)


def _mla_project(h, wc_ref, qn_ref, wuq_ref, kvn_ref, wukv_ref, cos, sin, q_ref, k_ref, v_ref):
    c = _dot(h, wc_ref[...])
    cq = _rms(c[:, 0:MLA_Q_RANK], qn_ref[...]).astype(BF16)
    qm = _dot(cq, wuq_ref[...]) * ((MLA_NOPE + MLA_ROPE) ** -0.5 * LOG2E)
    ckv = _rms(c[:, MLA_Q_RANK:MLA_Q_RANK + MLA_KV_RANK], kvn_ref[...]).astype(BF16)
    kv = _dot(ckv, wukv_ref[...])
    k_pe = _rope(c[:, MLA_Q_RANK + MLA_KV_RANK:], cos, sin).astype(BF16)
    lane = lax.broadcasted_iota(jnp.int32, k_pe.shape, 1)
    first = (lane % 64) < 32
    nope_w = HEADS * MLA_NOPE
    for pair in range(HEADS // 2):
        q_pe = _rope(qm[:, nope_w + pair * LANES:nope_w + (pair + 1) * LANES], cos, sin)
        for sub in range(2):
            hd = 2 * pair + sub
            keep = first if sub == 0 else jnp.logical_not(first)
            q_ref[:, hd * MLA_QK:hd * MLA_QK + MLA_NOPE] = qm[:, hd * MLA_NOPE:(hd + 1) * MLA_NOPE].astype(BF16)
            q_ref[:, hd * MLA_QK + MLA_NOPE:(hd + 1) * MLA_QK] = jnp.where(keep, q_pe, 0.0).astype(BF16)
    for hd in range(HEADS):
        k_ref[:, hd * MLA_QK:hd * MLA_QK + MLA_NOPE] = kv[:, hd * MLA_NOPE:(hd + 1) * MLA_NOPE].astype(BF16)
        k_ref[:, hd * MLA_QK + MLA_NOPE:(hd + 1) * MLA_QK] = k_pe
    v_ref[...] = kv[:, nope_w:].astype(BF16)


def _mla_attn_kernel(q_ref, k_ref, v_ref, o_ref, acc_scr, *, tq, nq):
    key = lax.broadcasted_iota(jnp.int32, (tq, tq), 0)
    qry = lax.broadcasted_iota(jnp.int32, (tq, tq), 1)
    causal = key <= qry

    def visit(qi, kj, carry):
        ms, ls = carry
        q0, k0 = qi * tq, kj * tq
        new_m, new_l = [], []
        score = lambda hd: _dot_nt(k_ref[k0:k0 + tq, hd * MLA_QK:(hd + 1) * MLA_QK],
                                   q_ref[q0:q0 + tq, hd * MLA_QK:(hd + 1) * MLA_QK])
        scores = {0: score(0), 1: score(1)}
        for hd in range(HEADS):
            if hd + 2 < HEADS:
                scores[hd + 2] = score(hd + 2)
            s_t = scores.pop(hd)
            if kj == qi:
                s_t = jnp.where(causal, s_t, MASK_NEG)
            pv_in = v_ref[k0:k0 + tq, hd * MLA_V:(hd + 1) * MLA_V]
            if kj == 0:
                m_new = jnp.max(s_t, axis=0, keepdims=True)
                p = jnp.exp2(s_t - m_new)
                new_l.append(jnp.sum(p, axis=0, keepdims=True))
                acc_scr[hd] = _dot_tn(pv_in, p.astype(BF16))
            else:
                m_new = jnp.maximum(ms[hd], jnp.max(s_t, axis=0, keepdims=True))
                alpha = jnp.exp2(ms[hd] - m_new)
                p = jnp.exp2(s_t - m_new)
                new_l.append(ls[hd] * alpha + jnp.sum(p, axis=0, keepdims=True))
                acc_scr[hd] = acc_scr[hd] * alpha + _dot_tn(pv_in, p.astype(BF16))
            new_m.append(m_new)
        return tuple(new_m), tuple(new_l)

    for qi in range(nq):
        carry = (None, None)
        for kj in range(qi + 1):
            carry = visit(qi, kj, carry)
        for hd in range(HEADS):
            o_ref[qi * tq:(qi + 1) * tq, hd * MLA_V:(hd + 1) * MLA_V] = jnp.transpose(
                acc_scr[hd] / carry[1][hd]).astype(o_ref.dtype)


def _mla_attn(q, k, v, b, s, tq):
    nq = s // tq
    return pl.pallas_call(
        functools.partial(_mla_attn_kernel, tq=tq, nq=nq),
        grid=(b,),
        in_specs=[pl.BlockSpec((s, HEADS * MLA_QK), lambda i: (i, 0)),
                  pl.BlockSpec((s, HEADS * MLA_QK), lambda i: (i, 0)),
                  pl.BlockSpec((s, HEADS * MLA_V), lambda i: (i, 0))],
        out_specs=pl.BlockSpec((s, HEADS * MLA_V), lambda i: (i, 0)),
        out_shape=jax.ShapeDtypeStruct((b * s, HEADS * MLA_V), BF16),
        scratch_shapes=[pltpu.VMEM((HEADS, MLA_V, tq), F32)],
        compiler_params=_params(("parallel",)),
        name="mla_attention",
    )(q, k, v)


def _merge_kernel(x_ref, nw_ref, wg_ref, ya_ref, yb_ref, yc_ref, wa_ref, wb_ref, wc_ref, wo_ref, o_ref):
    x = x_ref[...]
    h = _rms(x, nw_ref[...]).astype(BF16)
    merged = (_sigmoid(_dot(h, wg_ref[:, 0:D_MODEL])) * _dot(ya_ref[...], wa_ref[...])
              + _sigmoid(_dot(h, wg_ref[:, D_MODEL:2 * D_MODEL])) * _dot(yb_ref[...], wb_ref[...])
              + _sigmoid(_dot(h, wg_ref[:, 2 * D_MODEL:3 * D_MODEL])) * _dot(yc_ref[...], wc_ref[...]))
    o_ref[...] = x + _dot(merged.astype(BF16), wo_ref[...])


def _merge(x2, nw, wg, ya, yb, yc, wa, wb, wc, wo, tm):
    t = x2.shape[0]
    const = lambda a: pl.BlockSpec(a.shape, lambda i: (0,) * a.ndim)
    rows = lambda a: pl.BlockSpec((tm, a.shape[1]), lambda i: (i, 0))
    return pl.pallas_call(
        _merge_kernel,
        grid=(t // tm,),
        in_specs=[rows(x2), const(nw), const(wg), rows(ya), rows(yb), rows(yc),
                  const(wa), const(wb), const(wc), const(wo)],
        out_specs=rows(x2),
        out_shape=jax.ShapeDtypeStruct(x2.shape, F32),
        compiler_params=_params(("parallel",)),
        name="gated_merge",
    )(x2, nw, wg, ya, yb, yc, wa, wb, wc, wo)


def _ffn_kernel(x_ref, nw_ref, wi_ref, wo_ref, fw_ref, o_ref, *, final_norm):
    x = x_ref[...]
    h = _rms(x, nw_ref[...]).astype(BF16)
    g = _dot(h, wi_ref[:, 0:D_FF])
    u = _dot(h, wi_ref[:, D_FF:2 * D_FF])
    y = x + _dot((_silu(g) * u).astype(BF16), wo_ref[...])
    if final_norm:
        y = _rms(y, fw_ref[...])
    o_ref[...] = y


def _ffn(x2, nw, wi, wo, fw, tm, final_norm):
    t = x2.shape[0]
    const = lambda a: pl.BlockSpec(a.shape, lambda i: (0,) * a.ndim)
    rows = lambda a: pl.BlockSpec((tm, a.shape[1]), lambda i: (i, 0))
    return pl.pallas_call(
        functools.partial(_ffn_kernel, final_norm=final_norm),
        grid=(t // tm,),
        in_specs=[rows(x2), const(nw), const(wi), const(wo), const(fw)],
        out_specs=rows(x2),
        out_shape=jax.ShapeDtypeStruct(x2.shape, F32),
        compiler_params=_params(("parallel",)),
        name="swiglu_ffn",
    )(x2, nw, wi, wo, fw)


def _in_proj_slices(w_in_l):
    widths = (512, 512, 512, 512, 512, 512, 1024, 1024, MLA_Q_RANK, MLA_KV_RANK, MLA_ROPE, 3 * D_MODEL)
    offs = np.concatenate([[0], np.cumsum(widths)])
    cols = [w_in_l[:, int(offs[i]):int(offs[i + 1])] for i in range(len(widths))]
    w_hg = jnp.concatenate(cols[0:4], axis=1)
    w_ret = jnp.concatenate(cols[4:8], axis=1)
    kr = cols[10]
    kr1, kr2 = kr[:, :32], kr[:, 32:]
    w_c = jnp.concatenate([cols[8], cols[9], kr1, kr1, kr2, kr2], axis=1)
    return w_hg.astype(BF16), w_ret.astype(BF16), w_c.astype(BF16), cols[11].astype(BF16)


def _uq_layout(w_uq_l):
    per = MLA_NOPE + MLA_ROPE
    heads = [w_uq_l[:, hd * per:(hd + 1) * per] for hd in range(HEADS)]
    nope = [hh[:, :MLA_NOPE] for hh in heads]
    pe = [hh[:, MLA_NOPE:] for hh in heads]
    pairs = []
    for p in range(HEADS // 2):
        a, b = pe[2 * p], pe[2 * p + 1]
        pairs += [a[:, :32], b[:, :32], a[:, 32:], b[:, 32:]]
    return jnp.concatenate(nope + pairs, axis=1).astype(BF16)


def _ukv_layout(w_ukv_l):
    per = MLA_NOPE + MLA_V
    heads = [w_ukv_l[:, hd * per:(hd + 1) * per] for hd in range(HEADS)]
    return jnp.concatenate([hh[:, :MLA_NOPE] for hh in heads] + [hh[:, MLA_NOPE:] for hh in heads],
                           axis=1).astype(BF16)


def _block(n, want):
    return want if n % want == 0 else n


def kernel(x, positions, norm_mix_w, w_in, hg_lower_bounds, hg_norm_w, ret_norm_w, mla_q_norm_w, mla_w_uq, mla_kv_norm_w, mla_w_ukv, w_br_a, w_br_b, w_br_c, w_out, norm_ffn_w, w_ffn_in, w_ffn_out, final_norm_w):
    b, s, d = x.shape
    depth = w_in.shape[0]
    assert d == D_MODEL and s % (CHUNKS_PER_ITER * CHUNK) == 0
    t = b * s
    ts = _block(s, 512)
    tm = _block(t, 512)
    tq = _block(s, 512)

    lb_p = jax.nn.softmax(hg_lower_bounds.astype(F32), axis=0)
    lb_all = jnp.cumsum(lb_p, axis=0) - lb_p[0]

    rc, rs, mc, ms = _rope_tables(positions, _block(t, 512))
    row = lambda a: a.reshape(1, -1).astype(F32)

    x2 = x.reshape(t, d)
    for l in range(depth):
        w_hg, w_ret, w_c, w_g = _in_proj_slices(w_in[l])
        nw = row(norm_mix_w[l])
        ya = _hg_mixer(x2, nw, w_hg, row(lb_all[l]), row(hg_norm_w[l]), b, s, ts)
        yb, q, k, v = _ret_mixer(x2, nw, w_ret, row(ret_norm_w[l]), rc, rs, w_c, row(mla_q_norm_w[l]),
                                 _uq_layout(mla_w_uq[l]), row(mla_kv_norm_w[l]), _ukv_layout(mla_w_ukv[l]),
                                 mc, ms, b, s, ts)
        yc = _mla_attn(q, k, v, b, s, tq)
        x2 = _merge(x2, nw, w_g, ya, yb, yc, w_br_a[l].astype(BF16), w_br_b[l].astype(BF16),
                    w_br_c[l].astype(BF16), w_out[l].astype(BF16), tm)
        x2 = _ffn(x2, row(norm_ffn_w[l]), w_ffn_in[l].astype(BF16), w_ffn_out[l].astype(BF16),
                  row(final_norm_w), tm, final_norm=(l == depth - 1))
    return x2.reshape(b, s, d)
```

```python
import functools

import numpy as np
import jax
import jax.numpy as jnp
from jax import lax
from jax.experimental import pallas as pl
from jax.experimental.pallas import tpu as pltpu

F32 = jnp.float32
BF16 = jnp.bfloat16

D_MODEL = 1024
HEADS = 4
HG_K = 128
HG_V = 128
RET_K = 128
RET_V = 256
MLA_Q_RANK = 256
MLA_KV_RANK = 128
MLA_NOPE = 128
MLA_ROPE = 64
MLA_V = 128
MLA_QK = 256
D_FF = 2816
CHUNK = 64
CHUNKS_PER_ITER = 4
RET_CHUNK = 128
RET_CHUNKS_PER_ITER = 4
ROPE_BASE = 10000.0
EPS = 1e-6
EXP_CLIP = 60.0
MASK_NEG = -1e30
LOG2E = 1.4426950408889634

ROW_GROUP = 512
LANES = 128
VMEM_LIMIT = 56 * 1024 * 1024

LEVELS = (32, 16, 8, 4, 2, 1)


def _rms(x, w):
    return x * lax.rsqrt(jnp.mean(x * x, axis=-1, keepdims=True) + EPS) * w


def _sigmoid(x):
    return 0.5 + 0.5 * jnp.tanh(0.5 * x)


def _silu(x):
    return x * _sigmoid(x)


def _dot(a, b):
    return jnp.dot(a, b, preferred_element_type=F32)


def _dot_nt(a, b):
    return lax.dot_general(a, b, (((1,), (1,)), ((), ())), preferred_element_type=F32)


def _dot_tn(a, b):
    return lax.dot_general(a, b, (((0,), (0,)), ((), ())), preferred_element_type=F32)


def _params(sem):
    return pltpu.CompilerParams(dimension_semantics=sem, vmem_limit_bytes=VMEM_LIMIT)


def _rope_table_kernel(pos_ref, inv_ref, rc_ref, rs_ref, mc_ref, ms_ref):
    pos = pos_ref[...].astype(F32)
    ang = inv_ref[...] * pos
    ct = jnp.transpose(jnp.cos(ang))
    st = jnp.transpose(jnp.sin(ang))
    lo = lax.broadcasted_iota(jnp.int32, ct.shape, 1) < 64
    ct_r = pltpu.roll(ct, 64, 1)
    st_r = pltpu.roll(st, 64, 1)
    rc_ref[...] = jnp.where(lo, ct, ct_r)
    rs_ref[...] = jnp.where(lo, -st, st_r)
    mc_ref[...] = jnp.where(lo, ct_r, ct)
    ms_ref[...] = jnp.where(lo, -st_r, st)


def _rope_tables(positions, tt):
    t = positions.size
    pos = positions.reshape(1, t)
    inv64 = ROPE_BASE ** (-jnp.arange(64, dtype=F32) / 64)
    inv32 = ROPE_BASE ** (-jnp.arange(32, dtype=F32) / 32)
    inv = jnp.concatenate([inv64, inv32, inv32]).reshape(LANES, 1)
    tab = jax.ShapeDtypeStruct((t, LANES), F32)
    return pl.pallas_call(
        _rope_table_kernel,
        grid=(t // tt,),
        in_specs=[pl.BlockSpec((1, tt), lambda i: (0, i)),
                  pl.BlockSpec((LANES, 1), lambda i: (0, 0))],
        out_specs=[pl.BlockSpec((tt, LANES), lambda i: (i, 0))] * 4,
        out_shape=[tab] * 4,
        compiler_params=_params(("parallel",)),
        name="rope_tables",
    )(pos, inv)


def _rope(t, c, s):
    return t * c + pltpu.roll(t, 64, 1) * s


def _split3(x):
    a = x.astype(BF16)
    r = x - a.astype(F32)
    b = r.astype(BF16)
    c = (r - b.astype(F32)).astype(BF16)
    return a, b, c


def _level_ref(cum, m):
    w = cum.shape[1]
    if m >= 8:
        g = CHUNK // (2 * m)
        c3 = cum.reshape(g, 2 * m, w)
        return jnp.broadcast_to(c3[:, m - 1:m, :], (g, 2 * m, w)).reshape(CHUNK, w)
    c3 = cum.reshape(8, 8, w)

    def row(r):
        return jnp.broadcast_to(c3[:, r:r + 1, :], (8, 8, w))

    if m == 4:
        out = row(3)
    else:
        sub = lax.broadcasted_iota(jnp.int32, (8, 8, w), 1)
        out = jnp.where(sub < 4, row(1), row(5))
    return out.reshape(CHUNK, w)


def _upper_q_lower_k(q, k, m, upper):
    if m < 8:
        return jnp.where(upper[m], q, k)
    return jnp.concatenate([(q if blk % 2 else k)[blk * m:(blk + 1) * m] for blk in range(CHUNK // m)], axis=0)


def _hg_kernel(x_ref, nw_ref, w_ref, lb_ref, gw_ref, tril_ref, masks_ref, o_ref, p_scr, st_scr, *, ts):
    @pl.when(pl.program_id(1) == 0)
    def _():
        st_scr[...] = jnp.zeros_like(st_scr)

    h = _rms(x_ref[...], nw_ref[...]).astype(BF16)
    p_scr[...] = _dot(h, w_ref[...])

    width = HEADS * HG_K
    pw = 2 * HG_K
    tril = tril_ref[...]
    rows = lax.broadcasted_iota(jnp.int32, (CHUNK, pw), 0)
    upper = {m: (rows % (2 * m)) >= m for m in LEVELS}
    zk = jnp.zeros((CHUNK, HG_K), BF16)

    def blockdiag(t):
        return jnp.concatenate([jnp.concatenate([t[:, :HG_K], zk], axis=1),
                                jnp.concatenate([zk, t[:, HG_K:]], axis=1)], axis=0)


    def gates(r0, pair):
        c0 = pair * pw
        lb = lb_ref[:, c0:c0 + pw]
        q = _silu(p_scr[pl.ds(r0, CHUNK), c0:c0 + pw])
        hf = p_scr[pl.ds(r0, CHUNK), width + c0:width + c0 + pw]
        e = jnp.exp2(jnp.abs(hf) * -LOG2E)
        u = jnp.exp2(jnp.minimum(hf * -LOG2E, EXP_CLIP * LOG2E))
        r = 1.0 / (1.0 + e)
        logf = (jnp.minimum(hf, 0.0) + jnp.log((1.0 + lb * u) * r)) * LOG2E
        k = (1.0 - lb) * (jnp.where(hf >= 0.0, e, 1.0) * r)
        cums = _dot(tril, jnp.concatenate(_split3(logf), axis=1))
        cum = cums[:, :pw] + cums[:, pw:2 * pw] + cums[:, 2 * pw:]
        return q, k, logf, cum

    def level_dots(q, k, logf, cum):
        c_last = cum[CHUNK - 1:CHUNK, :]
        qd = (q * jnp.exp2(cum)).astype(BF16)
        kd = (k * jnp.exp2(c_last - cum)).astype(BF16)
        dec = jnp.exp2(c_last)
        prods = [_dot_nt(q.astype(BF16), blockdiag(k.astype(BF16)))]
        for m in LEVELS:
            if m == 1:
                xl = jnp.where(upper[m], q * jnp.exp2(logf), k)
            else:
                xl = _upper_q_lower_k(q, k, m, upper) * jnp.exp2(-jnp.abs(cum - _level_ref(cum, m)))
            xl = xl.astype(BF16)
            prods.append(_dot_nt(xl, blockdiag(xl)))
        return qd, kd, dec, prods

    def finish(r0, pair, qd, kd, dec, prods):
        c0 = pair * pw
        v = p_scr[pl.ds(r0, CHUNK), 2 * width + c0:2 * width + c0 + pw].astype(BF16)
        gate = _silu(p_scr[pl.ds(r0, CHUNK), 3 * width + c0:3 * width + c0 + pw])
        att = masks_ref[0] * prods[0]
        for li in range(1, len(prods)):
            att = att + masks_ref[li] * prods[li]
        o = _dot(att.astype(BF16), blockdiag(v))
        outs = []
        for sub in range(2):
            hs = slice(sub * HG_K, (sub + 1) * HG_K)
            st = st_scr[2 * pair + sub]
            oh = o[:, sub * HG_V:(sub + 1) * HG_V] + _dot_nt(qd[:, hs], st.astype(BF16))
            st_scr[2 * pair + sub] = st * dec[:, hs] + _dot_tn(v[:, hs], kd[:, hs])
            outs.append(oh * lax.rsqrt(jnp.mean(oh * oh, axis=-1, keepdims=True) + EPS))
        y = jnp.concatenate(outs, axis=1) * gw_ref[:, c0:c0 + pw] * gate
        o_ref[pl.ds(r0, CHUNK), c0:c0 + pw] = y.astype(o_ref.dtype)

    for c0 in range(0, ts // CHUNK, CHUNKS_PER_ITER):
        units = [((c0 + ci) * CHUNK, pair) for ci in range(CHUNKS_PER_ITER) for pair in range(HEADS // 2)]
        stage1 = [gates(r0, pair) for r0, pair in units]
        stage2 = [level_dots(*vals) for vals in stage1]
        for (r0, pair), vals in zip(units, stage2):
            finish(r0, pair, *vals)


def _level_masks():
    t = np.arange(CHUNK)[:, None]
    s = np.arange(CHUNK)[None, :]
    masks = [(t == s)]
    for m in LEVELS:
        masks.append((t // (2 * m) == s // (2 * m)) & (t % (2 * m) >= m) & (s % (2 * m) < m))
    return jnp.asarray(np.tile(np.stack(masks).astype(np.float32), (1, 1, 2)))


def _hg_mixer(x2, nw, w, lb, gw, b, s, ts):
    width = HEADS * HG_K
    nblk = s // ts
    tril = jnp.asarray(np.tril(np.ones((CHUNK, CHUNK), np.float32)), BF16)
    masks = _level_masks()
    const = lambda shape: pl.BlockSpec(shape, lambda i, j: (0,) * len(shape))
    return pl.pallas_call(
        functools.partial(_hg_kernel, ts=ts),
        grid=(b, nblk),
        in_specs=[pl.BlockSpec((ts, D_MODEL), lambda i, j: (i * nblk + j, 0)),
                  const((1, D_MODEL)), const((D_MODEL, 4 * width)), const((1, width)), const((1, width)),
                  const((CHUNK, CHUNK)), const(masks.shape)],
        out_specs=pl.BlockSpec((ts, HEADS * HG_V), lambda i, j: (i * nblk + j, 0)),
        out_shape=jax.ShapeDtypeStruct((b * s, HEADS * HG_V), BF16),
        scratch_shapes=[pltpu.VMEM((ts, 4 * width), F32), pltpu.VMEM((HEADS, HG_V, HG_K), F32)],
        compiler_params=_params(("parallel", "arbitrary")),
        name="hgrn2_mixer",
    )(x2, nw, w, lb, gw, tril, masks)


def _ret_kernel(x_ref, nw_ref, w_ref, gw_ref, rc_ref, rs_ref, intra_ref, qdec_ref, kdec_ref, cdec_ref,
                wc_ref, qn_ref, wuq_ref, kvn_ref, wukv_ref, mc_ref, ms_ref,
                o_ref, q_ref, k_ref, v_ref, p_scr, st_scr, h_scr, kc_scr, ks_scr, *, ts):
    @pl.when(pl.program_id(1) == 0)
    def _():
        st_scr[...] = jnp.zeros_like(st_scr)

    h_scr[...] = _rms(x_ref[...], nw_ref[...]).astype(BF16)
    p_scr[...] = _dot(h_scr[...], w_ref[...])
    kc_scr[...] = rc_ref[...] * (RET_K ** -0.5)
    ks_scr[...] = rs_ref[...] * (RET_K ** -0.5)

    qk_w = HEADS * RET_K
    v_w = HEADS * RET_V


    def scores(r0, hd):
        rows = pl.ds(r0, RET_CHUNK)
        q = _rope(p_scr[rows, hd * RET_K:(hd + 1) * RET_K], rc_ref[rows, :], rs_ref[rows, :])
        k = _rope(p_scr[rows, qk_w + hd * RET_K:qk_w + (hd + 1) * RET_K], kc_scr[rows, :], ks_scr[rows, :])
        return q, k, _dot_nt(q.astype(BF16), k.astype(BF16))

    def finish(r0, hd, q, k, qk):
        rows = pl.ds(r0, RET_CHUNK)
        v = p_scr[rows, 2 * qk_w + hd * RET_V:2 * qk_w + (hd + 1) * RET_V].astype(BF16)
        gate = _silu(p_scr[rows, 2 * qk_w + v_w + hd * RET_V:2 * qk_w + v_w + (hd + 1) * RET_V])
        att = (qk * intra_ref[hd]).astype(BF16)
        st = st_scr[hd]
        o = _dot(jnp.concatenate([att, (q * qdec_ref[hd]).astype(BF16)], axis=1),
                 jnp.concatenate([v, st.astype(BF16)], axis=0))
        st_scr[hd] = st * cdec_ref[hd] + _dot_tn((k * kdec_ref[hd]).astype(BF16), v)
        o = o * lax.rsqrt(jnp.mean(o * o, axis=-1, keepdims=True) + EPS)
        y = o * gw_ref[:, hd * RET_V:(hd + 1) * RET_V] * gate
        o_ref[rows, hd * RET_V:(hd + 1) * RET_V] = y.astype(o_ref.dtype)

    for c0 in range(0, ts // RET_CHUNK, RET_CHUNKS_PER_ITER):
        units = [((c0 + ci) * RET_CHUNK, hd) for ci in range(RET_CHUNKS_PER_ITER) for hd in range(HEADS)]
        stage1 = [scores(r0, hd) for r0, hd in units]
        for (r0, hd), vals in zip(units, stage1):
            finish(r0, hd, *vals)

    _mla_project(h_scr[...], wc_ref, qn_ref, wuq_ref, kvn_ref, wukv_ref, mc_ref[...], ms_ref[...],
                 q_ref, k_ref, v_ref)


def _ret_tables():
    log_gamma = jnp.log1p(-jnp.exp2(-5.0 - jnp.arange(HEADS, dtype=F32)))
    idx = jnp.arange(RET_CHUNK, dtype=F32)
    causal = jnp.tril(jnp.ones((RET_CHUNK, RET_CHUNK), dtype=bool))[None]
    lg = log_gamma[:, None, None]
    dist = jnp.where(causal, (idx[:, None] - idx[None, :])[None], 0.0)
    intra = jnp.where(causal, jnp.exp(dist * lg), 0.0)
    q_decay = jnp.exp((idx + 1.0)[None, :] * log_gamma[:, None])[..., None]
    k_decay = jnp.exp((RET_CHUNK - 1.0 - idx)[None, :] * log_gamma[:, None])[..., None]
    chunk_decay = jnp.exp(RET_CHUNK * log_gamma)[:, None, None]
    return (intra,
            jnp.broadcast_to(q_decay, (HEADS, RET_CHUNK, RET_K)),
            jnp.broadcast_to(k_decay, (HEADS, RET_CHUNK, RET_K)),
            jnp.broadcast_to(chunk_decay, (HEADS, 1, RET_V)))


def _ret_mixer(x2, nw, w, gw, rc, rs, wc, qn, wuq, kvn, wukv, mc, ms, b, s, ts):
    nblk = s // ts
    t = b * s
    intra, qdec, kdec, cdec = _ret_tables()
    n_proj = 2 * HEADS * RET_K + 2 * HEADS * RET_V
    const = lambda a: pl.BlockSpec(a.shape, lambda i, j: (0,) * a.ndim)
    rows = lambda width: pl.BlockSpec((ts, width), lambda i, j: (i * nblk + j, 0))
    out_widths = (HEADS * RET_V, HEADS * MLA_QK, HEADS * MLA_QK, HEADS * MLA_V)
    return pl.pallas_call(
        functools.partial(_ret_kernel, ts=ts),
        grid=(b, nblk),
        in_specs=[rows(D_MODEL), const(nw), const(w), const(gw), rows(LANES), rows(LANES),
                  const(intra), const(qdec), const(kdec), const(cdec),
                  const(wc), const(qn), const(wuq), const(kvn), const(wukv), rows(LANES), rows(LANES)],
        out_specs=[rows(wd) for wd in out_widths],
        out_shape=[jax.ShapeDtypeStruct((t, wd), BF16) for wd in out_widths],
        scratch_shapes=[pltpu.VMEM((ts, n_proj), F32), pltpu.VMEM((HEADS, RET_K, RET_V), F32),
                        pltpu.VMEM((ts, D_MODEL), BF16), pltpu.VMEM((ts, LANES), F32), pltpu.VMEM((ts, LANES), F32)],
        compiler_params=_params(("parallel", "arbitrary")),
        name="retention_mla_mixer",
    )(x2, nw, w, gw, rc, rs, intra, qdec, kdec, cdec, wc, qn, wuq, kvn, wukv, mc, ms)


def _mla_project(h, wc_ref, qn_ref, wuq_ref, kvn_ref, wukv_ref, cos, sin, q_ref, k_ref, v_ref):
    c = _dot(h, wc_ref[...])
    cq = _rms(c[:, 0:MLA_Q_RANK], qn_ref[...]).astype(BF16)
    qm = _dot(cq, wuq_ref[...]) * ((MLA_NOPE + MLA_ROPE) ** -0.5 * LOG2E)
    ckv = _rms(c[:, MLA_Q_RANK:MLA_Q_RANK + MLA_KV_RANK], kvn_ref[...]).astype(BF16)
    kv = _dot(ckv, wukv_ref[...])
    k_pe = _rope(c[:, MLA_Q_RANK + MLA_KV_RANK:], cos, sin).astype(BF16)
    lane = lax.broadcasted_iota(jnp.int32, k_pe.shape, 1)
    first = (lane % 64) < 32
    nope_w = HEADS * MLA_NOPE
    for pair in range(HEADS // 2):
        q_pe = _rope(qm[:, nope_w + pair * LANES:nope_w + (pair + 1) * LANES], cos, sin)
        for sub in range(2):
            hd = 2 * pair + sub
            keep = first if sub == 0 else jnp.logical_not(first)
            q_ref[:, hd * MLA_QK:hd * MLA_QK + MLA_NOPE] = qm[:, hd * MLA_NOPE:(hd + 1) * MLA_NOPE].astype(BF16)
            q_ref[:, hd * MLA_QK + MLA_NOPE:(hd + 1) * MLA_QK] = jnp.where(keep, q_pe, 0.0).astype(BF16)
    for hd in range(HEADS):
        k_ref[:, hd * MLA_QK:hd * MLA_QK + MLA_NOPE] = kv[:, hd * MLA_NOPE:(hd + 1) * MLA_NOPE].astype(BF16)
        k_ref[:, hd * MLA_QK + MLA_NOPE:(hd + 1) * MLA_QK] = k_pe
    v_ref[...] = kv[:, nope_w:].astype(BF16)


def _mla_attn_kernel(q_ref, k_ref, v_ref, o_ref, acc_scr, *, tq, nq):
    key = lax.broadcasted_iota(jnp.int32, (tq, tq), 0)
    qry = lax.broadcasted_iota(jnp.int32, (tq, tq), 1)
    causal = key <= qry

    def visit(qi, kj, carry):
        ms, ls = carry
        q0, k0 = qi * tq, kj * tq
        new_m, new_l = [], []
        score = lambda hd: _dot_nt(k_ref[k0:k0 + tq, hd * MLA_QK:(hd + 1) * MLA_QK],
                                   q_ref[q0:q0 + tq, hd * MLA_QK:(hd + 1) * MLA_QK])
        scores = {0: score(0), 1: score(1)}
        for hd in range(HEADS):
            if hd + 2 < HEADS:
                scores[hd + 2] = score(hd + 2)
            s_t = scores.pop(hd)
            if kj == qi:
                s_t = jnp.where(causal, s_t, MASK_NEG)
            pv_in = v_ref[k0:k0 + tq, hd * MLA_V:(hd + 1) * MLA_V]
            if kj == 0:
                m_new = jnp.max(s_t, axis=0, keepdims=True)
                p = jnp.exp2(s_t - m_new)
                new_l.append(jnp.sum(p, axis=0, keepdims=True))
                acc_scr[hd] = _dot_tn(pv_in, p.astype(BF16))
            else:
                m_new = jnp.maximum(ms[hd], jnp.max(s_t, axis=0, keepdims=True))
                alpha = jnp.exp2(ms[hd] - m_new)
                p = jnp.exp2(s_t - m_new)
                new_l.append(ls[hd] * alpha + jnp.sum(p, axis=0, keepdims=True))
                acc_scr[hd] = acc_scr[hd] * alpha + _dot_tn(pv_in, p.astype(BF16))
            new_m.append(m_new)
        return tuple(new_m), tuple(new_l)

    for qi in range(nq):
        carry = (None, None)
        for kj in range(qi + 1):
            carry = visit(qi, kj, carry)
        for hd in range(HEADS):
            o_ref[qi * tq:(qi + 1) * tq, hd * MLA_V:(hd + 1) * MLA_V] = jnp.transpose(
                acc_scr[hd] / carry[1][hd]).astype(o_ref.dtype)


def _mla_attn(q, k, v, b, s, tq):
    nq = s // tq
    return pl.pallas_call(
        functools.partial(_mla_attn_kernel, tq=tq, nq=nq),
        grid=(b,),
        in_specs=[pl.BlockSpec((s, HEADS * MLA_QK), lambda i: (i, 0)),
                  pl.BlockSpec((s, HEADS * MLA_QK), lambda i: (i, 0)),
                  pl.BlockSpec((s, HEADS * MLA_V), lambda i: (i, 0))],
        out_specs=pl.BlockSpec((s, HEADS * MLA_V), lambda i: (i, 0)),
        out_shape=jax.ShapeDtypeStruct((b * s, HEADS * MLA_V), BF16),
        scratch_shapes=[pltpu.VMEM((HEADS, MLA_V, tq), F32)],
        compiler_params=_params(("parallel",)),
        name="mla_attention",
    )(q, k, v)


def _merge_kernel(x_ref, nw_ref, wg_ref, ya_ref, yb_ref, yc_ref, wa_ref, wb_ref, wc_ref, wo_ref, o_ref):
    for r0 in range(0, x_ref.shape[0], ROW_GROUP):
        rows = slice(r0, r0 + ROW_GROUP)
        x = x_ref[rows, :]
        h = _rms(x, nw_ref[...]).astype(BF16)
        merged = (_sigmoid(_dot(h, wg_ref[:, 0:D_MODEL])) * _dot(ya_ref[rows, :], wa_ref[...])
                  + _sigmoid(_dot(h, wg_ref[:, D_MODEL:2 * D_MODEL])) * _dot(yb_ref[rows, :], wb_ref[...])
                  + _sigmoid(_dot(h, wg_ref[:, 2 * D_MODEL:3 * D_MODEL])) * _dot(yc_ref[rows, :], wc_ref[...]))
        o_ref[rows, :] = x + _dot(merged.astype(BF16), wo_ref[...])


def _merge(x2, nw, wg, ya, yb, yc, wa, wb, wc, wo, tm):
    t = x2.shape[0]
    const = lambda a: pl.BlockSpec(a.shape, lambda i: (0,) * a.ndim)
    rows = lambda a: pl.BlockSpec((tm, a.shape[1]), lambda i: (i, 0))
    return pl.pallas_call(
        _merge_kernel,
        grid=(t // tm,),
        in_specs=[rows(x2), const(nw), const(wg), rows(ya), rows(yb), rows(yc),
                  const(wa), const(wb), const(wc), const(wo)],
        out_specs=rows(x2),
        out_shape=jax.ShapeDtypeStruct(x2.shape, F32),
        compiler_params=_params(("parallel",)),
        name="gated_merge",
    )(x2, nw, wg, ya, yb, yc, wa, wb, wc, wo)


def _ffn_kernel(x_ref, nw_ref, wi_ref, wo_ref, fw_ref, o_ref, *, final_norm):
    for r0 in range(0, x_ref.shape[0], ROW_GROUP):
        rows = slice(r0, r0 + ROW_GROUP)
        x = x_ref[rows, :]
        h = _rms(x, nw_ref[...]).astype(BF16)
        g = _dot(h, wi_ref[:, 0:D_FF])
        u = _dot(h, wi_ref[:, D_FF:2 * D_FF])
        y = x + _dot((_silu(g) * u).astype(BF16), wo_ref[...])
        if final_norm:
            y = _rms(y, fw_ref[...])
        o_ref[rows, :] = y


def _ffn(x2, nw, wi, wo, fw, tm, final_norm):
    t = x2.shape[0]
    const = lambda a: pl.BlockSpec(a.shape, lambda i: (0,) * a.ndim)
    rows = lambda a: pl.BlockSpec((tm, a.shape[1]), lambda i: (i, 0))
    return pl.pallas_call(
        functools.partial(_ffn_kernel, final_norm=final_norm),
        grid=(t // tm,),
        in_specs=[rows(x2), const(nw), const(wi), const(wo), const(fw)],
        out_specs=rows(x2),
        out_shape=jax.ShapeDtypeStruct(x2.shape, F32),
        compiler_params=_params(("parallel",)),
        name="swiglu_ffn",
    )(x2, nw, wi, wo, fw)


def _in_proj_slices(w_in_l):
    widths = (512, 512, 512, 512, 512, 512, 1024, 1024, MLA_Q_RANK, MLA_KV_RANK, MLA_ROPE, 3 * D_MODEL)
    offs = np.concatenate([[0], np.cumsum(widths)])
    cols = [w_in_l[:, int(offs[i]):int(offs[i + 1])] for i in range(len(widths))]
    w_hg = jnp.concatenate(cols[0:4], axis=1)
    w_ret = jnp.concatenate(cols[4:8], axis=1)
    kr = cols[10]
    kr1, kr2 = kr[:, :32], kr[:, 32:]
    w_c = jnp.concatenate([cols[8], cols[9], kr1, kr1, kr2, kr2], axis=1)
    return w_hg.astype(BF16), w_ret.astype(BF16), w_c.astype(BF16), cols[11].astype(BF16)


def _uq_layout(w_uq_l):
    per = MLA_NOPE + MLA_ROPE
    heads = [w_uq_l[:, hd * per:(hd + 1) * per] for hd in range(HEADS)]
    nope = [hh[:, :MLA_NOPE] for hh in heads]
    pe = [hh[:, MLA_NOPE:] for hh in heads]
    pairs = []
    for p in range(HEADS // 2):
        a, b = pe[2 * p], pe[2 * p + 1]
        pairs += [a[:, :32], b[:, :32], a[:, 32:], b[:, 32:]]
    return jnp.concatenate(nope + pairs, axis=1).astype(BF16)


def _ukv_layout(w_ukv_l):
    per = MLA_NOPE + MLA_V
    heads = [w_ukv_l[:, hd * per:(hd + 1) * per] for hd in range(HEADS)]
    return jnp.concatenate([hh[:, :MLA_NOPE] for hh in heads] + [hh[:, MLA_NOPE:] for hh in heads],
                           axis=1).astype(BF16)


def _block(n, want):
    return want if n % want == 0 else n


def kernel(x, positions, norm_mix_w, w_in, hg_lower_bounds, hg_norm_w, ret_norm_w, mla_q_norm_w, mla_w_uq, mla_kv_norm_w, mla_w_ukv, w_br_a, w_br_b, w_br_c, w_out, norm_ffn_w, w_ffn_in, w_ffn_out, final_norm_w):
    b, s, d = x.shape
    depth = w_in.shape[0]
    assert d == D_MODEL and s % (CHUNKS_PER_ITER * CHUNK) == 0
    t = b * s
    ts = _block(s, 512)
    tm = _block(t, 2 * ROW_GROUP)
    tq = _block(s, 512)
    assert tm % ROW_GROUP == 0

    lb_p = jax.nn.softmax(hg_lower_bounds.astype(F32), axis=0)
    lb_all = jnp.cumsum(lb_p, axis=0) - lb_p[0]

    rc, rs, mc, ms = _rope_tables(positions, _block(t, 512))
    row = lambda a: a.reshape(1, -1).astype(F32)

    x2 = x.reshape(t, d)
    for l in range(depth):
        w_hg, w_ret, w_c, w_g = _in_proj_slices(w_in[l])
        nw = row(norm_mix_w[l])
        ya = _hg_mixer(x2, nw, w_hg, row(lb_all[l]), row(hg_norm_w[l]), b, s, ts)
        yb, q, k, v = _ret_mixer(x2, nw, w_ret, row(ret_norm_w[l]), rc, rs, w_c, row(mla_q_norm_w[l]),
                                 _uq_layout(mla_w_uq[l]), row(mla_kv_norm_w[l]), _ukv_layout(mla_w_ukv[l]),
                                 mc, ms, b, s, ts)
        yc = _mla_attn(q, k, v, b, s, tq)
        x2 = _merge(x2, nw, w_g, ya, yb, yc, w_br_a[l].astype(BF16), w_br_b[l].astype(BF16),
                    w_br_c[l].astype(BF16), w_out[l].astype(BF16), tm)
        x2 = _ffn(x2, row(norm_ffn_w[l]), w_ffn_in[l].astype(BF16), w_ffn_out[l].astype(BF16),
                  row(final_norm_w), tm, final_norm=(l == depth - 1))
    return x2.reshape(b, s, d)
```

```python
import functools

import numpy as np
import jax
import jax.numpy as jnp
from jax import lax
from jax.experimental import pallas as pl
from jax.experimental.pallas import tpu as pltpu

F32 = jnp.float32
BF16 = jnp.bfloat16

D_MODEL = 1024
HEADS = 4
HG_K = 128
HG_V = 128
RET_K = 128
RET_V = 256
MLA_Q_RANK = 256
MLA_KV_RANK = 128
MLA_NOPE = 128
MLA_ROPE = 64
MLA_V = 128
MLA_QK = 256
D_FF = 2816
CHUNK = 64
CHUNKS_PER_ITER = 4
RET_CHUNK = 128
RET_CHUNKS_PER_ITER = 4
ROPE_BASE = 10000.0
EPS = 1e-6
EXP_CLIP = 60.0
MASK_NEG = -1e30
LOG2E = 1.4426950408889634

ONES_ROWS = 16
ROW_GROUP = 512
LANES = 128
VMEM_LIMIT = 56 * 1024 * 1024

LEVELS = (32, 16, 8, 4, 2, 1)


def _rms(x, w):
    return x * lax.rsqrt(jnp.mean(x * x, axis=-1, keepdims=True) + EPS) * w


def _sigmoid(x):
    return 0.5 + 0.5 * jnp.tanh(0.5 * x)


def _silu(x):
    return x * _sigmoid(x)


def _dot(a, b):
    return jnp.dot(a, b, preferred_element_type=F32)


def _dot_nt(a, b):
    return lax.dot_general(a, b, (((1,), (1,)), ((), ())), preferred_element_type=F32)


def _dot_tn(a, b):
    return lax.dot_general(a, b, (((0,), (0,)), ((), ())), preferred_element_type=F32)


def _params(sem):
    return pltpu.CompilerParams(dimension_semantics=sem, vmem_limit_bytes=VMEM_LIMIT)


RET_FREQ_ORDER = np.concatenate([np.arange(0, 64, 2), np.arange(1, 64, 2)])


def _rope_table_kernel(pos_ref, inv_ref, rc_ref, rs_ref, mc_ref, ms_ref):
    pos = pos_ref[...].astype(F32)
    ang = inv_ref[...] * pos
    c = jnp.cos(ang)
    s = jnp.sin(ang)
    c32, s32 = c[:32], s[:32]
    rc_ref[...] = jnp.transpose(jnp.concatenate([c, c], axis=0))
    rs_ref[...] = jnp.transpose(jnp.concatenate([-s, s], axis=0))
    mc_ref[...] = jnp.transpose(jnp.concatenate([c32, c32, c32, c32], axis=0))
    ms_ref[...] = jnp.transpose(jnp.concatenate([-s32, -s32, s32, s32], axis=0))


def _rope_tables(positions, tt):
    t = positions.size
    pos = positions.reshape(1, t)
    inv64 = ROPE_BASE ** (-jnp.arange(64, dtype=F32) / 64)
    inv = inv64[RET_FREQ_ORDER].reshape(64, 1)
    tab = jax.ShapeDtypeStruct((t, LANES), F32)
    return pl.pallas_call(
        _rope_table_kernel,
        grid=(t // tt,),
        in_specs=[pl.BlockSpec((1, tt), lambda i: (0, i)),
                  pl.BlockSpec((64, 1), lambda i: (0, 0))],
        out_specs=[pl.BlockSpec((tt, LANES), lambda i: (i, 0))] * 4,
        out_shape=[tab] * 4,
        compiler_params=_params(("parallel",)),
        name="rope_tables",
    )(pos, inv)


def _rope(t, c, s):
    return t * c + pltpu.roll(t, 64, 1) * s


def _split3(x):
    a = x.astype(BF16)
    r = x - a.astype(F32)
    b = r.astype(BF16)
    c = (r - b.astype(F32)).astype(BF16)
    return a, b, c


def _level_ref(cum, m):
    w = cum.shape[1]
    if m >= 8:
        g = CHUNK // (2 * m)
        c3 = cum.reshape(g, 2 * m, w)
        return jnp.broadcast_to(c3[:, m - 1:m, :], (g, 2 * m, w)).reshape(CHUNK, w)
    c3 = cum.reshape(8, 8, w)

    def row(r):
        return jnp.broadcast_to(c3[:, r:r + 1, :], (8, 8, w))

    if m == 4:
        out = row(3)
    else:
        sub = lax.broadcasted_iota(jnp.int32, (8, 8, w), 1)
        out = jnp.where(sub < 4, row(1), row(5))
    return out.reshape(CHUNK, w)


def _upper_q_lower_k(q, k, m, upper):
    if m < 8:
        return jnp.where(upper[m], q, k)
    return jnp.concatenate([(q if blk % 2 else k)[blk * m:(blk + 1) * m] for blk in range(CHUNK // m)], axis=0)


def _hg_kernel(x_ref, nw_ref, w_ref, lb_ref, gw_ref, tril_ref, masks_ref, o_ref, p_scr, st_scr, *, ts):
    @pl.when(pl.program_id(1) == 0)
    def _():
        st_scr[...] = jnp.zeros_like(st_scr)

    h = _rms(x_ref[...], nw_ref[...]).astype(BF16)
    p_scr[...] = _dot(h, w_ref[...])

    width = HEADS * HG_K
    pw = 2 * HG_K
    tril = tril_ref[...]
    rows = lax.broadcasted_iota(jnp.int32, (CHUNK, pw), 0)
    upper = {m: (rows % (2 * m)) >= m for m in LEVELS}
    zk = jnp.zeros((CHUNK, HG_K), BF16)

    def blockdiag(t):
        return jnp.concatenate([jnp.concatenate([t[:, :HG_K], zk], axis=1),
                                jnp.concatenate([zk, t[:, HG_K:]], axis=1)], axis=0)


    def gates(r0, pair):
        c0 = pair * pw
        lb = lb_ref[:, c0:c0 + pw]
        q = _silu(p_scr[pl.ds(r0, CHUNK), c0:c0 + pw])
        hf = p_scr[pl.ds(r0, CHUNK), width + c0:width + c0 + pw]
        e = jnp.exp2(jnp.abs(hf) * -LOG2E)
        u = jnp.exp2(jnp.minimum(hf * -LOG2E, EXP_CLIP * LOG2E))
        r = 1.0 / (1.0 + e)
        logf = (jnp.minimum(hf, 0.0) + jnp.log((1.0 + lb * u) * r)) * LOG2E
        k = (1.0 - lb) * (jnp.where(hf >= 0.0, e, 1.0) * r)
        cums = _dot(tril, jnp.concatenate(_split3(logf), axis=1))
        cum = cums[:, :pw] + cums[:, pw:2 * pw] + cums[:, 2 * pw:]
        return q, k, logf, cum

    def level_dots(q, k, logf, cum):
        c_last = cum[CHUNK - 1:CHUNK, :]
        qd = (q * jnp.exp2(cum)).astype(BF16)
        kd = (k * jnp.exp2(c_last - cum)).astype(BF16)
        dec = jnp.exp2(c_last)
        prods = [_dot_nt(q.astype(BF16), blockdiag(k.astype(BF16)))]
        for m in LEVELS:
            if m == 1:
                xl = jnp.where(upper[m], q * jnp.exp2(logf), k)
            else:
                xl = _upper_q_lower_k(q, k, m, upper) * jnp.exp2(-jnp.abs(cum - _level_ref(cum, m)))
            xl = xl.astype(BF16)
            prods.append(_dot_nt(xl, blockdiag(xl)))
        return qd, kd, dec, prods

    def finish(r0, pair, qd, kd, dec, prods):
        c0 = pair * pw
        v = p_scr[pl.ds(r0, CHUNK), 2 * width + c0:2 * width + c0 + pw].astype(BF16)
        gate = _silu(p_scr[pl.ds(r0, CHUNK), 3 * width + c0:3 * width + c0 + pw])
        att = masks_ref[0] * prods[0]
        for li in range(1, len(prods)):
            att = att + masks_ref[li] * prods[li]
        o = _dot(att.astype(BF16), blockdiag(v))
        outs = []
        for sub in range(2):
            hs = slice(sub * HG_K, (sub + 1) * HG_K)
            st = st_scr[2 * pair + sub]
            oh = o[:, sub * HG_V:(sub + 1) * HG_V] + _dot_nt(qd[:, hs], st.astype(BF16))
            st_scr[2 * pair + sub] = st * dec[:, hs] + _dot_tn(v[:, hs], kd[:, hs])
            outs.append(oh * lax.rsqrt(jnp.mean(oh * oh, axis=-1, keepdims=True) + EPS))
        y = jnp.concatenate(outs, axis=1) * gw_ref[:, c0:c0 + pw] * gate
        o_ref[pl.ds(r0, CHUNK), c0:c0 + pw] = y.astype(o_ref.dtype)

    for c0 in range(0, ts // CHUNK, CHUNKS_PER_ITER):
        units = [((c0 + ci) * CHUNK, pair) for ci in range(CHUNKS_PER_ITER) for pair in range(HEADS // 2)]
        stage1 = [gates(r0, pair) for r0, pair in units]
        stage2 = [level_dots(*vals) for vals in stage1]
        for (r0, pair), vals in zip(units, stage2):
            finish(r0, pair, *vals)


def _level_masks():
    t = np.arange(CHUNK)[:, None]
    s = np.arange(CHUNK)[None, :]
    masks = [(t == s)]
    for m in LEVELS:
        masks.append((t // (2 * m) == s // (2 * m)) & (t % (2 * m) >= m) & (s % (2 * m) < m))
    return jnp.asarray(np.tile(np.stack(masks).astype(np.float32), (1, 1, 2)))


def _hg_mixer(x2, nw, w, lb, gw, b, s, ts):
    width = HEADS * HG_K
    nblk = s // ts
    tril = jnp.asarray(np.tril(np.ones((CHUNK, CHUNK), np.float32)), BF16)
    masks = _level_masks()
    const = lambda shape: pl.BlockSpec(shape, lambda i, j: (0,) * len(shape))
    return pl.pallas_call(
        functools.partial(_hg_kernel, ts=ts),
        grid=(b, nblk),
        in_specs=[pl.BlockSpec((ts, D_MODEL), lambda i, j: (i * nblk + j, 0)),
                  const((1, D_MODEL)), const((D_MODEL, 4 * width)), const((1, width)), const((1, width)),
                  const((CHUNK, CHUNK)), const(masks.shape)],
        out_specs=pl.BlockSpec((ts, HEADS * HG_V), lambda i, j: (i * nblk + j, 0)),
        out_shape=jax.ShapeDtypeStruct((b * s, HEADS * HG_V), BF16),
        scratch_shapes=[pltpu.VMEM((ts, 4 * width), F32), pltpu.VMEM((HEADS, HG_V, HG_K), F32)],
        compiler_params=_params(("parallel", "arbitrary")),
        name="hgrn2_mixer",
    )(x2, nw, w, lb, gw, tril, masks)


def _ret_kernel(x_ref, nw_ref, w_ref, gw_ref, rc_ref, rs_ref, intra_ref, qdec_ref, kdec_ref, cdec_ref,
                wc_ref, qn_ref, wuq_ref, kvn_ref, wukv_ref, mc_ref, ms_ref,
                o_ref, q_ref, k_ref, vt_ref, p_scr, st_scr, h_scr, kc_scr, ks_scr, *, ts):
    @pl.when(pl.program_id(1) == 0)
    def _():
        st_scr[...] = jnp.zeros_like(st_scr)

    h_scr[...] = _rms(x_ref[...], nw_ref[...]).astype(BF16)
    p_scr[...] = _dot(h_scr[...], w_ref[...])
    kc_scr[...] = rc_ref[...] * (RET_K ** -0.5)
    ks_scr[...] = rs_ref[...] * (RET_K ** -0.5)

    qk_w = HEADS * RET_K
    v_w = HEADS * RET_V


    def scores(r0, hd):
        rows = pl.ds(r0, RET_CHUNK)
        q = _rope(p_scr[rows, hd * RET_K:(hd + 1) * RET_K], rc_ref[rows, :], rs_ref[rows, :])
        k = _rope(p_scr[rows, qk_w + hd * RET_K:qk_w + (hd + 1) * RET_K], kc_scr[rows, :], ks_scr[rows, :])
        return q, k, _dot_nt(q.astype(BF16), k.astype(BF16))

    def finish(r0, hd, q, k, qk):
        rows = pl.ds(r0, RET_CHUNK)
        v = p_scr[rows, 2 * qk_w + hd * RET_V:2 * qk_w + (hd + 1) * RET_V].astype(BF16)
        gate = _silu(p_scr[rows, 2 * qk_w + v_w + hd * RET_V:2 * qk_w + v_w + (hd + 1) * RET_V])
        att = (qk * intra_ref[hd]).astype(BF16)
        st = st_scr[hd]
        o = _dot(jnp.concatenate([att, (q * qdec_ref[hd]).astype(BF16)], axis=1),
                 jnp.concatenate([v, st.astype(BF16)], axis=0))
        st_scr[hd] = st * cdec_ref[hd] + _dot_tn((k * kdec_ref[hd]).astype(BF16), v)
        o = o * lax.rsqrt(jnp.mean(o * o, axis=-1, keepdims=True) + EPS)
        y = o * gw_ref[:, hd * RET_V:(hd + 1) * RET_V] * gate
        o_ref[rows, hd * RET_V:(hd + 1) * RET_V] = y.astype(o_ref.dtype)

    for c0 in range(0, ts // RET_CHUNK, RET_CHUNKS_PER_ITER):
        units = [((c0 + ci) * RET_CHUNK, hd) for ci in range(RET_CHUNKS_PER_ITER) for hd in range(HEADS)]
        stage1 = [scores(r0, hd) for r0, hd in units]
        for (r0, hd), vals in zip(units, stage1):
            finish(r0, hd, *vals)

    _mla_project(h_scr[...], wc_ref, qn_ref, wuq_ref, kvn_ref, wukv_ref, mc_ref[...], ms_ref[...],
                 q_ref, k_ref, vt_ref)


def _ret_tables():
    log_gamma = jnp.log1p(-jnp.exp2(-5.0 - jnp.arange(HEADS, dtype=F32)))
    idx = jnp.arange(RET_CHUNK, dtype=F32)
    causal = jnp.tril(jnp.ones((RET_CHUNK, RET_CHUNK), dtype=bool))[None]
    lg = log_gamma[:, None, None]
    dist = jnp.where(causal, (idx[:, None] - idx[None, :])[None], 0.0)
    intra = jnp.where(causal, jnp.exp(dist * lg), 0.0)
    q_decay = jnp.exp((idx + 1.0)[None, :] * log_gamma[:, None])[..., None]
    k_decay = jnp.exp((RET_CHUNK - 1.0 - idx)[None, :] * log_gamma[:, None])[..., None]
    chunk_decay = jnp.exp(RET_CHUNK * log_gamma)[:, None, None]
    return (intra,
            jnp.broadcast_to(q_decay, (HEADS, RET_CHUNK, RET_K)),
            jnp.broadcast_to(k_decay, (HEADS, RET_CHUNK, RET_K)),
            jnp.broadcast_to(chunk_decay, (HEADS, 1, RET_V)))


def _ret_mixer(x2, nw, w, gw, rc, rs, wc, qn, wuq, kvn, wukv, mc, ms, b, s, ts):
    nblk = s // ts
    t = b * s
    intra, qdec, kdec, cdec = _ret_tables()
    n_proj = 2 * HEADS * RET_K + 2 * HEADS * RET_V
    const = lambda a: pl.BlockSpec(a.shape, lambda i, j: (0,) * a.ndim)
    rows = lambda width: pl.BlockSpec((ts, width), lambda i, j: (i * nblk + j, 0))
    out_widths = (HEADS * RET_V, HEADS * MLA_QK, HEADS * MLA_QK)
    vt_spec = pl.BlockSpec((HEADS * MLA_V, ts), lambda i, j: (i, j))
    return pl.pallas_call(
        functools.partial(_ret_kernel, ts=ts),
        grid=(b, nblk),
        in_specs=[rows(D_MODEL), const(nw), const(w), const(gw), rows(LANES), rows(LANES),
                  const(intra), const(qdec), const(kdec), const(cdec),
                  const(wc), const(qn), const(wuq), const(kvn), const(wukv), rows(LANES), rows(LANES)],
        out_specs=[rows(wd) for wd in out_widths] + [vt_spec],
        out_shape=([jax.ShapeDtypeStruct((t, wd), BF16) for wd in out_widths]
                   + [jax.ShapeDtypeStruct((b * HEADS * MLA_V, s), BF16)]),
        scratch_shapes=[pltpu.VMEM((ts, n_proj), F32), pltpu.VMEM((HEADS, RET_K, RET_V), F32),
                        pltpu.VMEM((ts, D_MODEL), BF16), pltpu.VMEM((ts, LANES), F32), pltpu.VMEM((ts, LANES), F32)],
        compiler_params=_params(("parallel", "arbitrary")),
        name="retention_mla_mixer",
    )(x2, nw, w, gw, rc, rs, intra, qdec, kdec, cdec, wc, qn, wuq, kvn, wukv, mc, ms)


def _mla_project(h, wc_ref, qn_ref, wuq_ref, kvn_ref, wukv_ref, cos, sin, q_ref, k_ref, vt_ref):
    c = _dot(h, wc_ref[...])
    cq = _rms(c[:, 0:MLA_Q_RANK], qn_ref[...]).astype(BF16)
    qm = _dot(cq, wuq_ref[...]) * ((MLA_NOPE + MLA_ROPE) ** -0.5 * LOG2E)
    ckv = _rms(c[:, MLA_Q_RANK:MLA_Q_RANK + MLA_KV_RANK], kvn_ref[...]).astype(BF16)
    kv = _dot(ckv, wukv_ref[...])
    k_pe = _rope(c[:, MLA_Q_RANK + MLA_KV_RANK:], cos, sin).astype(BF16)
    lane = lax.broadcasted_iota(jnp.int32, k_pe.shape, 1)
    first = (lane % 64) < 32
    nope_w = HEADS * MLA_NOPE
    for pair in range(HEADS // 2):
        q_pe = _rope(qm[:, nope_w + pair * LANES:nope_w + (pair + 1) * LANES], cos, sin)
        for sub in range(2):
            hd = 2 * pair + sub
            keep = first if sub == 0 else jnp.logical_not(first)
            q_ref[:, hd * MLA_QK:hd * MLA_QK + MLA_NOPE] = qm[:, hd * MLA_NOPE:(hd + 1) * MLA_NOPE].astype(BF16)
            q_ref[:, hd * MLA_QK + MLA_NOPE:(hd + 1) * MLA_QK] = jnp.where(keep, q_pe, 0.0).astype(BF16)
    for hd in range(HEADS):
        k_ref[:, hd * MLA_QK:hd * MLA_QK + MLA_NOPE] = kv[:, hd * MLA_NOPE:(hd + 1) * MLA_NOPE].astype(BF16)
        k_ref[:, hd * MLA_QK + MLA_NOPE:(hd + 1) * MLA_QK] = k_pe
    vt_ref[...] = jnp.transpose(kv[:, nope_w:]).astype(BF16)


def _mla_attn_kernel(q_ref, k_ref, vt_ref, o_ref, acc_scr, *, tq, nq):
    key = lax.broadcasted_iota(jnp.int32, (tq, tq), 0)
    qry = lax.broadcasted_iota(jnp.int32, (tq, tq), 1)
    causal = key <= qry

    ones = jnp.ones((ONES_ROWS, tq), BF16)

    def visit(qi, kj, ms):
        q0, k0 = qi * tq, kj * tq
        new_m = []
        score = lambda hd: _dot_nt(k_ref[k0:k0 + tq, hd * MLA_QK:(hd + 1) * MLA_QK],
                                   q_ref[q0:q0 + tq, hd * MLA_QK:(hd + 1) * MLA_QK])
        scores = {0: score(0), 1: score(1)}
        for hd in range(HEADS):
            if hd + 2 < HEADS:
                scores[hd + 2] = score(hd + 2)
            s_t = scores.pop(hd)
            if kj == qi:
                s_t = jnp.where(causal, s_t, MASK_NEG)
            v_aug = jnp.concatenate([vt_ref[hd * MLA_V:(hd + 1) * MLA_V, k0:k0 + tq], ones], axis=0)
            if kj == 0:
                m_new = jnp.max(s_t, axis=0, keepdims=True)
                acc_scr[hd] = _dot(v_aug, jnp.exp2(s_t - m_new).astype(BF16))
            else:
                m_new = jnp.maximum(ms[hd], jnp.max(s_t, axis=0, keepdims=True))
                alpha = jnp.exp2(ms[hd] - m_new)
                acc_scr[hd] = acc_scr[hd] * alpha + _dot(v_aug, jnp.exp2(s_t - m_new).astype(BF16))
            new_m.append(m_new)
        return tuple(new_m)

    for qi in range(nq):
        ms = None
        for kj in range(qi + 1):
            ms = visit(qi, kj, ms)
        for hd in range(HEADS):
            acc = acc_scr[hd]
            o_ref[qi * tq:(qi + 1) * tq, hd * MLA_V:(hd + 1) * MLA_V] = jnp.transpose(
                acc[:MLA_V] / acc[MLA_V:MLA_V + 1]).astype(o_ref.dtype)


def _mla_attn(q, k, vt, b, s, tq):
    nq = s // tq
    return pl.pallas_call(
        functools.partial(_mla_attn_kernel, tq=tq, nq=nq),
        grid=(b,),
        in_specs=[pl.BlockSpec((s, HEADS * MLA_QK), lambda i: (i, 0)),
                  pl.BlockSpec((s, HEADS * MLA_QK), lambda i: (i, 0)),
                  pl.BlockSpec((HEADS * MLA_V, s), lambda i: (i, 0))],
        out_specs=pl.BlockSpec((s, HEADS * MLA_V), lambda i: (i, 0)),
        out_shape=jax.ShapeDtypeStruct((b * s, HEADS * MLA_V), BF16),
        scratch_shapes=[pltpu.VMEM((HEADS, MLA_V + ONES_ROWS, tq), F32)],
        compiler_params=_params(("parallel",)),
        name="mla_attention",
    )(q, k, vt)


def _merge_kernel(x_ref, nw_ref, wg_ref, ya_ref, yb_ref, yc_ref, wa_ref, wb_ref, wc_ref, wo_ref, o_ref):
    for r0 in range(0, x_ref.shape[0], ROW_GROUP):
        rows = slice(r0, r0 + ROW_GROUP)
        x = x_ref[rows, :]
        h = _rms(x, nw_ref[...]).astype(BF16)
        merged = (_sigmoid(_dot(h, wg_ref[:, 0:D_MODEL])) * _dot(ya_ref[rows, :], wa_ref[...])
                  + _sigmoid(_dot(h, wg_ref[:, D_MODEL:2 * D_MODEL])) * _dot(yb_ref[rows, :], wb_ref[...])
                  + _sigmoid(_dot(h, wg_ref[:, 2 * D_MODEL:3 * D_MODEL])) * _dot(yc_ref[rows, :], wc_ref[...]))
        o_ref[rows, :] = x + _dot(merged.astype(BF16), wo_ref[...])


def _merge(x2, nw, wg, ya, yb, yc, wa, wb, wc, wo, tm):
    t = x2.shape[0]
    const = lambda a: pl.BlockSpec(a.shape, lambda i: (0,) * a.ndim)
    rows = lambda a: pl.BlockSpec((tm, a.shape[1]), lambda i: (i, 0))
    return pl.pallas_call(
        _merge_kernel,
        grid=(t // tm,),
        in_specs=[rows(x2), const(nw), const(wg), rows(ya), rows(yb), rows(yc),
                  const(wa), const(wb), const(wc), const(wo)],
        out_specs=rows(x2),
        out_shape=jax.ShapeDtypeStruct(x2.shape, F32),
        compiler_params=_params(("parallel",)),
        name="gated_merge",
    )(x2, nw, wg, ya, yb, yc, wa, wb, wc, wo)


def _ffn_kernel(x_ref, nw_ref, wi_ref, wo_ref, fw_ref, o_ref, *, final_norm):
    for r0 in range(0, x_ref.shape[0], ROW_GROUP):
        rows = slice(r0, r0 + ROW_GROUP)
        x = x_ref[rows, :]
        h = _rms(x, nw_ref[...]).astype(BF16)
        g = _dot(h, wi_ref[:, 0:D_FF])
        u = _dot(h, wi_ref[:, D_FF:2 * D_FF])
        y = x + _dot((_silu(g) * u).astype(BF16), wo_ref[...])
        if final_norm:
            y = _rms(y, fw_ref[...])
        o_ref[rows, :] = y


def _ffn(x2, nw, wi, wo, fw, tm, final_norm):
    t = x2.shape[0]
    const = lambda a: pl.BlockSpec(a.shape, lambda i: (0,) * a.ndim)
    rows = lambda a: pl.BlockSpec((tm, a.shape[1]), lambda i: (i, 0))
    return pl.pallas_call(
        functools.partial(_ffn_kernel, final_norm=final_norm),
        grid=(t // tm,),
        in_specs=[rows(x2), const(nw), const(wi), const(wo), const(fw)],
        out_specs=rows(x2),
        out_shape=jax.ShapeDtypeStruct(x2.shape, F32),
        compiler_params=_params(("parallel",)),
        name="swiglu_ffn",
    )(x2, nw, wi, wo, fw)


def _in_proj_slices(w_in_l):
    widths = (512, 512, 512, 512, 512, 512, 1024, 1024, MLA_Q_RANK, MLA_KV_RANK, MLA_ROPE, 3 * D_MODEL)
    offs = np.concatenate([[0], np.cumsum(widths)])
    cols = [w_in_l[:, int(offs[i]):int(offs[i + 1])] for i in range(len(widths))]
    w_hg = jnp.concatenate(cols[0:4], axis=1)

    def freq_order(w):
        return w.reshape(D_MODEL, HEADS, 2, RET_K // 2)[..., RET_FREQ_ORDER].reshape(D_MODEL, HEADS * RET_K)

    w_ret = jnp.concatenate([freq_order(cols[4]), freq_order(cols[5]), cols[6], cols[7]], axis=1)
    kr = cols[10]
    kr1, kr2 = kr[:, :32], kr[:, 32:]
    w_c = jnp.concatenate([cols[8], cols[9], kr1, kr1, kr2, kr2], axis=1)
    return w_hg, w_ret, w_c, cols[11]


def _uq_layout(w_uq_l):
    per = MLA_NOPE + MLA_ROPE
    heads = [w_uq_l[:, hd * per:(hd + 1) * per] for hd in range(HEADS)]
    nope = [hh[:, :MLA_NOPE] for hh in heads]
    pe = [hh[:, MLA_NOPE:] for hh in heads]
    pairs = []
    for p in range(HEADS // 2):
        a, b = pe[2 * p], pe[2 * p + 1]
        pairs += [a[:, :32], b[:, :32], a[:, 32:], b[:, 32:]]
    return jnp.concatenate(nope + pairs, axis=1).astype(BF16)


def _ukv_layout(w_ukv_l):
    per = MLA_NOPE + MLA_V
    heads = [w_ukv_l[:, hd * per:(hd + 1) * per] for hd in range(HEADS)]
    return jnp.concatenate([hh[:, :MLA_NOPE] for hh in heads] + [hh[:, MLA_NOPE:] for hh in heads],
                           axis=1).astype(BF16)


def _block(n, want):
    return want if n % want == 0 else n


def kernel(x, positions, norm_mix_w, w_in, hg_lower_bounds, hg_norm_w, ret_norm_w, mla_q_norm_w, mla_w_uq, mla_kv_norm_w, mla_w_ukv, w_br_a, w_br_b, w_br_c, w_out, norm_ffn_w, w_ffn_in, w_ffn_out, final_norm_w):
    b, s, d = x.shape
    depth = w_in.shape[0]
    assert d == D_MODEL and s % (CHUNKS_PER_ITER * CHUNK) == 0
    t = b * s
    ts = _block(s, 512)
    tm = _block(t, 2 * ROW_GROUP)
    tq = _block(s, 512)
    assert tm % ROW_GROUP == 0

    lb_p = jax.nn.softmax(hg_lower_bounds.astype(F32), axis=0)
    lb_all = jnp.cumsum(lb_p, axis=0) - lb_p[0]

    rc, rs, mc, ms = _rope_tables(positions, _block(t, 512))
    row = lambda a: a.reshape(1, -1).astype(F32)

    w_in_bf = w_in.astype(BF16)
    x2 = x.reshape(t, d)
    for l in range(depth):
        w_hg, w_ret, w_c, w_g = _in_proj_slices(w_in_bf[l])
        nw = row(norm_mix_w[l])
        ya = _hg_mixer(x2, nw, w_hg, row(lb_all[l]), row(hg_norm_w[l]), b, s, ts)
        yb, q, k, vt = _ret_mixer(x2, nw, w_ret, row(ret_norm_w[l]), rc, rs, w_c, row(mla_q_norm_w[l]),
                                 _uq_layout(mla_w_uq[l]), row(mla_kv_norm_w[l]), _ukv_layout(mla_w_ukv[l]),
                                 mc, ms, b, s, ts)
        yc = _mla_attn(q, k, vt, b, s, tq)
        x2 = _merge(x2, nw, w_g, ya, yb, yc, w_br_a[l].astype(BF16), w_br_b[l].astype(BF16),
                    w_br_c[l].astype(BF16), w_out[l].astype(BF16), tm)
        x2 = _ffn(x2, row(norm_ffn_w[l]), w_ffn_in[l].astype(BF16), w_ffn_out[l].astype(BF16),
                  row(final_norm_w), tm, final_norm=(l == depth - 1))
    return x2.reshape(b, s, d)
```

```python
import functools

import numpy as np
import jax
import jax.numpy as jnp
from jax import lax
from jax.experimental import pallas as pl
from jax.experimental.pallas import tpu as pltpu

F32 = jnp.float32
BF16 = jnp.bfloat16

D_MODEL = 1024
HEADS = 4
HG_K = 128
HG_V = 128
RET_K = 128
RET_V = 256
MLA_Q_RANK = 256
MLA_KV_RANK = 128
MLA_NOPE = 128
MLA_ROPE = 64
MLA_V = 128
MLA_QK = 256
D_FF = 2816
CHUNK = 64
CHUNKS_PER_ITER = 4
RET_CHUNK = 128
RET_CHUNKS_PER_ITER = 4
ROPE_BASE = 10000.0
EPS = 1e-6
EXP_CLIP = 60.0
MASK_NEG = -1e30
LOG2E = 1.4426950408889634

ROW_GROUP = 512
LANES = 128
VMEM_LIMIT = 56 * 1024 * 1024

LEVELS = (32, 16, 8, 4, 2, 1)


def _rms(x, w):
    return x * lax.rsqrt(jnp.mean(x * x, axis=-1, keepdims=True) + EPS) * w


def _sigmoid(x):
    return 0.5 + 0.5 * jnp.tanh(0.5 * x)


def _silu(x):
    return x * _sigmoid(x)


def _dot(a, b):
    return jnp.dot(a, b, preferred_element_type=F32)


def _dot_nt(a, b):
    return lax.dot_general(a, b, (((1,), (1,)), ((), ())), preferred_element_type=F32)


def _dot_tn(a, b):
    return lax.dot_general(a, b, (((0,), (0,)), ((), ())), preferred_element_type=F32)


def _params(sem):
    return pltpu.CompilerParams(dimension_semantics=sem, vmem_limit_bytes=VMEM_LIMIT)


RET_FREQ_ORDER = np.concatenate([np.arange(0, 64, 2), np.arange(1, 64, 2)])


def _rope_table_kernel(pos_ref, inv_ref, rc_ref, rs_ref, mc_ref, ms_ref):
    pos = pos_ref[...].astype(F32)
    ang = inv_ref[...] * pos
    c = jnp.cos(ang)
    s = jnp.sin(ang)
    c32, s32 = c[:32], s[:32]
    rc_ref[...] = jnp.transpose(jnp.concatenate([c, c], axis=0))
    rs_ref[...] = jnp.transpose(jnp.concatenate([-s, s], axis=0))
    mc_ref[...] = jnp.transpose(jnp.concatenate([c32, c32, c32, c32], axis=0))
    ms_ref[...] = jnp.transpose(jnp.concatenate([-s32, -s32, s32, s32], axis=0))


def _rope_tables(positions, tt):
    t = positions.size
    pos = positions.reshape(1, t)
    inv64 = ROPE_BASE ** (-jnp.arange(64, dtype=F32) / 64)
    inv = inv64[RET_FREQ_ORDER].reshape(64, 1)
    tab = jax.ShapeDtypeStruct((t, LANES), F32)
    return pl.pallas_call(
        _rope_table_kernel,
        grid=(t // tt,),
        in_specs=[pl.BlockSpec((1, tt), lambda i: (0, i)),
                  pl.BlockSpec((64, 1), lambda i: (0, 0))],
        out_specs=[pl.BlockSpec((tt, LANES), lambda i: (i, 0))] * 4,
        out_shape=[tab] * 4,
        compiler_params=_params(("parallel",)),
        name="rope_tables",
    )(pos, inv)


def _rope(t, c, s):
    return t * c + pltpu.roll(t, 64, 1) * s


def _split3(x):
    a = x.astype(BF16)
    r = x - a.astype(F32)
    b = r.astype(BF16)
    c = (r - b.astype(F32)).astype(BF16)
    return a, b, c


def _level_ref(cum, m):
    w = cum.shape[1]
    if m >= 8:
        g = CHUNK // (2 * m)
        c3 = cum.reshape(g, 2 * m, w)
        return jnp.broadcast_to(c3[:, m - 1:m, :], (g, 2 * m, w)).reshape(CHUNK, w)
    c3 = cum.reshape(8, 8, w)

    def row(r):
        return jnp.broadcast_to(c3[:, r:r + 1, :], (8, 8, w))

    if m == 4:
        out = row(3)
    else:
        sub = lax.broadcasted_iota(jnp.int32, (8, 8, w), 1)
        out = jnp.where(sub < 4, row(1), row(5))
    return out.reshape(CHUNK, w)


def _upper_q_lower_k(q, k, m, upper):
    if m < 8:
        return jnp.where(upper[m], q, k)
    return jnp.concatenate([(q if blk % 2 else k)[blk * m:(blk + 1) * m] for blk in range(CHUNK // m)], axis=0)


def _hg_kernel(x_ref, nw_ref, w_ref, lb_ref, gw_ref, tril_ref, masks_ref, o_ref, p_scr, st_scr, *, ts):
    @pl.when(pl.program_id(1) == 0)
    def _():
        st_scr[...] = jnp.zeros_like(st_scr)

    def project(r0):
        rows = slice(r0, r0 + ROW_GROUP)
        h = _rms(x_ref[rows, :], nw_ref[...]).astype(BF16)
        p_scr[rows, :] = _dot(h, w_ref[...])

    width = HEADS * HG_K
    pw = 2 * HG_K
    tril = tril_ref[...]
    rows = lax.broadcasted_iota(jnp.int32, (CHUNK, pw), 0)
    upper = {m: (rows % (2 * m)) >= m for m in LEVELS}
    zk = jnp.zeros((CHUNK, HG_K), BF16)

    def blockdiag(t):
        return jnp.concatenate([jnp.concatenate([t[:, :HG_K], zk], axis=1),
                                jnp.concatenate([zk, t[:, HG_K:]], axis=1)], axis=0)


    def gates(r0, pair):
        c0 = pair * pw
        lb = lb_ref[:, c0:c0 + pw]
        q = _silu(p_scr[pl.ds(r0, CHUNK), c0:c0 + pw])
        hf = p_scr[pl.ds(r0, CHUNK), width + c0:width + c0 + pw]
        e = jnp.exp2(jnp.abs(hf) * -LOG2E)
        u = jnp.exp2(jnp.minimum(hf * -LOG2E, EXP_CLIP * LOG2E))
        r = 1.0 / (1.0 + e)
        logf = (jnp.minimum(hf, 0.0) + jnp.log((1.0 + lb * u) * r)) * LOG2E
        k = (1.0 - lb) * (jnp.where(hf >= 0.0, e, 1.0) * r)
        cums = _dot(tril, jnp.concatenate(_split3(logf), axis=1))
        cum = cums[:, :pw] + cums[:, pw:2 * pw] + cums[:, 2 * pw:]
        return q, k, logf, cum

    def level_dots(q, k, logf, cum):
        c_last = cum[CHUNK - 1:CHUNK, :]
        qd = (q * jnp.exp2(cum)).astype(BF16)
        kd = (k * jnp.exp2(c_last - cum)).astype(BF16)
        dec = jnp.exp2(c_last)
        prods = [_dot_nt(q.astype(BF16), blockdiag(k.astype(BF16)))]
        for m in LEVELS:
            if m == 1:
                xl = jnp.where(upper[m], q * jnp.exp2(logf), k)
            else:
                xl = _upper_q_lower_k(q, k, m, upper) * jnp.exp2(-jnp.abs(cum - _level_ref(cum, m)))
            xl = xl.astype(BF16)
            prods.append(_dot_nt(xl, blockdiag(xl)))
        return qd, kd, dec, prods

    def finish(r0, pair, qd, kd, dec, prods):
        c0 = pair * pw
        v = p_scr[pl.ds(r0, CHUNK), 2 * width + c0:2 * width + c0 + pw].astype(BF16)
        gate = _silu(p_scr[pl.ds(r0, CHUNK), 3 * width + c0:3 * width + c0 + pw])
        att = masks_ref[0] * prods[0]
        for li in range(1, len(prods)):
            att = att + masks_ref[li] * prods[li]
        o = _dot(att.astype(BF16), blockdiag(v))
        outs = []
        for sub in range(2):
            hs = slice(sub * HG_K, (sub + 1) * HG_K)
            st = st_scr[2 * pair + sub]
            oh = o[:, sub * HG_V:(sub + 1) * HG_V] + _dot_nt(qd[:, hs], st.astype(BF16))
            st_scr[2 * pair + sub] = st * dec[:, hs] + _dot_tn(v[:, hs], kd[:, hs])
            outs.append(oh * lax.rsqrt(jnp.mean(oh * oh, axis=-1, keepdims=True) + EPS))
        y = jnp.concatenate(outs, axis=1) * gw_ref[:, c0:c0 + pw] * gate
        o_ref[pl.ds(r0, CHUNK), c0:c0 + pw] = y.astype(o_ref.dtype)

    for c0 in range(0, ts // CHUNK, CHUNKS_PER_ITER):
        if (c0 * CHUNK) % ROW_GROUP == 0:
            project(c0 * CHUNK)
        units = [((c0 + ci) * CHUNK, pair) for ci in range(CHUNKS_PER_ITER) for pair in range(HEADS // 2)]
        stage1 = [gates(r0, pair) for r0, pair in units]
        stage2 = [level_dots(*vals) for vals in stage1]
        for (r0, pair), vals in zip(units, stage2):
            finish(r0, pair, *vals)


def _level_masks():
    t = np.arange(CHUNK)[:, None]
    s = np.arange(CHUNK)[None, :]
    masks = [(t == s)]
    for m in LEVELS:
        masks.append((t // (2 * m) == s // (2 * m)) & (t % (2 * m) >= m) & (s % (2 * m) < m))
    return jnp.asarray(np.tile(np.stack(masks).astype(np.float32), (1, 1, 2)))


def _hg_mixer(x2, nw, w, lb, gw, b, s, ts):
    width = HEADS * HG_K
    nblk = s // ts
    tril = jnp.asarray(np.tril(np.ones((CHUNK, CHUNK), np.float32)), BF16)
    masks = _level_masks()
    const = lambda shape: pl.BlockSpec(shape, lambda i, j: (0,) * len(shape))
    return pl.pallas_call(
        functools.partial(_hg_kernel, ts=ts),
        grid=(b, nblk),
        in_specs=[pl.BlockSpec((ts, D_MODEL), lambda i, j: (i * nblk + j, 0)),
                  const((1, D_MODEL)), const((D_MODEL, 4 * width)), const((1, width)), const((1, width)),
                  const((CHUNK, CHUNK)), const(masks.shape)],
        out_specs=pl.BlockSpec((ts, HEADS * HG_V), lambda i, j: (i * nblk + j, 0)),
        out_shape=jax.ShapeDtypeStruct((b * s, HEADS * HG_V), BF16),
        scratch_shapes=[pltpu.VMEM((ts, 4 * width), F32), pltpu.VMEM((HEADS, HG_V, HG_K), F32)],
        compiler_params=_params(("parallel", "arbitrary")),
        name="hgrn2_mixer",
    )(x2, nw, w, lb, gw, tril, masks)


def _ret_kernel(x_ref, nw_ref, w_ref, gw_ref, rc_ref, rs_ref, intra_ref, qdec_ref, kdec_ref, cdec_ref,
                wc_ref, qn_ref, wuq_ref, kvn_ref, wukv_ref, mc_ref, ms_ref,
                o_ref, q_ref, k_ref, v_ref, p_scr, st_scr, h_scr, kc_scr, ks_scr, *, ts):
    @pl.when(pl.program_id(1) == 0)
    def _():
        st_scr[...] = jnp.zeros_like(st_scr)

    h_scr[...] = _rms(x_ref[...], nw_ref[...]).astype(BF16)
    p_scr[...] = _dot(h_scr[...], w_ref[...])
    kc_scr[...] = rc_ref[...] * (RET_K ** -0.5)
    ks_scr[...] = rs_ref[...] * (RET_K ** -0.5)

    qk_w = HEADS * RET_K
    v_w = HEADS * RET_V


    def scores(r0, hd):
        rows = pl.ds(r0, RET_CHUNK)
        q = _rope(p_scr[rows, hd * RET_K:(hd + 1) * RET_K], rc_ref[rows, :], rs_ref[rows, :])
        k = _rope(p_scr[rows, qk_w + hd * RET_K:qk_w + (hd + 1) * RET_K], kc_scr[rows, :], ks_scr[rows, :])
        return q, k, _dot_nt(q.astype(BF16), k.astype(BF16))

    def finish(r0, hd, q, k, qk):
        rows = pl.ds(r0, RET_CHUNK)
        v = p_scr[rows, 2 * qk_w + hd * RET_V:2 * qk_w + (hd + 1) * RET_V].astype(BF16)
        gate = _silu(p_scr[rows, 2 * qk_w + v_w + hd * RET_V:2 * qk_w + v_w + (hd + 1) * RET_V])
        att = (qk * intra_ref[hd]).astype(BF16)
        st = st_scr[hd]
        o = _dot(jnp.concatenate([att, (q * qdec_ref[hd]).astype(BF16)], axis=1),
                 jnp.concatenate([v, st.astype(BF16)], axis=0))
        st_scr[hd] = st * cdec_ref[hd] + _dot_tn((k * kdec_ref[hd]).astype(BF16), v)
        o = o * lax.rsqrt(jnp.mean(o * o, axis=-1, keepdims=True) + EPS)
        y = o * gw_ref[:, hd * RET_V:(hd + 1) * RET_V] * gate
        o_ref[rows, hd * RET_V:(hd + 1) * RET_V] = y.astype(o_ref.dtype)

    for c0 in range(0, ts // RET_CHUNK, RET_CHUNKS_PER_ITER):
        units = [((c0 + ci) * RET_CHUNK, hd) for ci in range(RET_CHUNKS_PER_ITER) for hd in range(HEADS)]
        stage1 = [scores(r0, hd) for r0, hd in units]
        for (r0, hd), vals in zip(units, stage1):
            finish(r0, hd, *vals)

    _mla_project(h_scr[...], wc_ref, qn_ref, wuq_ref, kvn_ref, wukv_ref, mc_ref[...], ms_ref[...],
                 q_ref, k_ref, v_ref)


def _ret_tables():
    log_gamma = jnp.log1p(-jnp.exp2(-5.0 - jnp.arange(HEADS, dtype=F32)))
    idx = jnp.arange(RET_CHUNK, dtype=F32)
    causal = jnp.tril(jnp.ones((RET_CHUNK, RET_CHUNK), dtype=bool))[None]
    lg = log_gamma[:, None, None]
    dist = jnp.where(causal, (idx[:, None] - idx[None, :])[None], 0.0)
    intra = jnp.where(causal, jnp.exp(dist * lg), 0.0)
    q_decay = jnp.exp((idx + 1.0)[None, :] * log_gamma[:, None])[..., None]
    k_decay = jnp.exp((RET_CHUNK - 1.0 - idx)[None, :] * log_gamma[:, None])[..., None]
    chunk_decay = jnp.exp(RET_CHUNK * log_gamma)[:, None, None]
    return (intra,
            jnp.broadcast_to(q_decay, (HEADS, RET_CHUNK, RET_K)),
            jnp.broadcast_to(k_decay, (HEADS, RET_CHUNK, RET_K)),
            jnp.broadcast_to(chunk_decay, (HEADS, 1, RET_V)))


def _ret_mixer(x2, nw, w, gw, rc, rs, wc, qn, wuq, kvn, wukv, mc, ms, b, s, ts):
    nblk = s // ts
    t = b * s
    intra, qdec, kdec, cdec = _ret_tables()
    n_proj = 2 * HEADS * RET_K + 2 * HEADS * RET_V
    const = lambda a: pl.BlockSpec(a.shape, lambda i, j: (0,) * a.ndim)
    rows = lambda width: pl.BlockSpec((ts, width), lambda i, j: (i * nblk + j, 0))
    out_widths = (HEADS * RET_V, HEADS * MLA_QK, HEADS * MLA_QK, HEADS * MLA_V)
    return pl.pallas_call(
        functools.partial(_ret_kernel, ts=ts),
        grid=(b, nblk),
        in_specs=[rows(D_MODEL), const(nw), const(w), const(gw), rows(LANES), rows(LANES),
                  const(intra), const(qdec), const(kdec), const(cdec),
                  const(wc), const(qn), const(wuq), const(kvn), const(wukv), rows(LANES), rows(LANES)],
        out_specs=[rows(wd) for wd in out_widths],
        out_shape=[jax.ShapeDtypeStruct((t, wd), BF16) for wd in out_widths],
        scratch_shapes=[pltpu.VMEM((ts, n_proj), F32), pltpu.VMEM((HEADS, RET_K, RET_V), F32),
                        pltpu.VMEM((ts, D_MODEL), BF16), pltpu.VMEM((ts, LANES), F32), pltpu.VMEM((ts, LANES), F32)],
        compiler_params=_params(("parallel", "arbitrary")),
        name="retention_mla_mixer",
    )(x2, nw, w, gw, rc, rs, intra, qdec, kdec, cdec, wc, qn, wuq, kvn, wukv, mc, ms)


def _mla_project(h, wc_ref, qn_ref, wuq_ref, kvn_ref, wukv_ref, cos, sin, q_ref, k_ref, v_ref):
    c = _dot(h, wc_ref[...])
    cq = _rms(c[:, 0:MLA_Q_RANK], qn_ref[...]).astype(BF16)
    qm = _dot(cq, wuq_ref[...]) * ((MLA_NOPE + MLA_ROPE) ** -0.5 * LOG2E)
    ckv = _rms(c[:, MLA_Q_RANK:MLA_Q_RANK + MLA_KV_RANK], kvn_ref[...]).astype(BF16)
    kv = _dot(ckv, wukv_ref[...])
    k_pe = _rope(c[:, MLA_Q_RANK + MLA_KV_RANK:], cos, sin).astype(BF16)
    lane = lax.broadcasted_iota(jnp.int32, k_pe.shape, 1)
    first = (lane % 64) < 32
    nope_w = HEADS * MLA_NOPE
    for pair in range(HEADS // 2):
        q_pe = _rope(qm[:, nope_w + pair * LANES:nope_w + (pair + 1) * LANES], cos, sin)
        for sub in range(2):
            hd = 2 * pair + sub
            keep = first if sub == 0 else jnp.logical_not(first)
            q_ref[:, hd * MLA_QK:hd * MLA_QK + MLA_NOPE] = qm[:, hd * MLA_NOPE:(hd + 1) * MLA_NOPE].astype(BF16)
            q_ref[:, hd * MLA_QK + MLA_NOPE:(hd + 1) * MLA_QK] = jnp.where(keep, q_pe, 0.0).astype(BF16)
    for hd in range(HEADS):
        k_ref[:, hd * MLA_QK:hd * MLA_QK + MLA_NOPE] = kv[:, hd * MLA_NOPE:(hd + 1) * MLA_NOPE].astype(BF16)
        k_ref[:, hd * MLA_QK + MLA_NOPE:(hd + 1) * MLA_QK] = k_pe
    v_ref[...] = kv[:, nope_w:].astype(BF16)


def _mla_attn_kernel(q_ref, k_ref, v_ref, o_ref, acc_scr, *, tq, nq):
    key = lax.broadcasted_iota(jnp.int32, (tq, tq), 0)
    qry = lax.broadcasted_iota(jnp.int32, (tq, tq), 1)
    causal = key <= qry

    def visit(qi, kj, carry):
        ms, ls = carry
        q0, k0 = qi * tq, kj * tq
        new_m, new_l = [], []
        score = lambda hd: _dot_nt(k_ref[k0:k0 + tq, hd * MLA_QK:(hd + 1) * MLA_QK],
                                   q_ref[q0:q0 + tq, hd * MLA_QK:(hd + 1) * MLA_QK])
        scores = {0: score(0), 1: score(1)}
        for hd in range(HEADS):
            if hd + 2 < HEADS:
                scores[hd + 2] = score(hd + 2)
            s_t = scores.pop(hd)
            if kj == qi:
                s_t = jnp.where(causal, s_t, MASK_NEG)
            pv_in = v_ref[k0:k0 + tq, hd * MLA_V:(hd + 1) * MLA_V]
            if kj == 0:
                m_new = jnp.max(s_t, axis=0, keepdims=True)
                p = jnp.exp2(s_t - m_new)
                new_l.append(jnp.sum(p, axis=0, keepdims=True))
                acc_scr[hd] = _dot_tn(pv_in, p.astype(BF16))
            else:
                m_new = jnp.maximum(ms[hd], jnp.max(s_t, axis=0, keepdims=True))
                alpha = jnp.exp2(ms[hd] - m_new)
                p = jnp.exp2(s_t - m_new)
                new_l.append(ls[hd] * alpha + jnp.sum(p, axis=0, keepdims=True))
                acc_scr[hd] = acc_scr[hd] * alpha + _dot_tn(pv_in, p.astype(BF16))
            new_m.append(m_new)
        return tuple(new_m), tuple(new_l)

    for qi in range(nq):
        carry = (None, None)
        for kj in range(qi + 1):
            carry = visit(qi, kj, carry)
        for hd in range(HEADS):
            o_ref[qi * tq:(qi + 1) * tq, hd * MLA_V:(hd + 1) * MLA_V] = jnp.transpose(
                acc_scr[hd] / carry[1][hd]).astype(o_ref.dtype)


def _mla_attn(q, k, v, b, s, tq):
    nq = s // tq
    return pl.pallas_call(
        functools.partial(_mla_attn_kernel, tq=tq, nq=nq),
        grid=(b,),
        in_specs=[pl.BlockSpec((s, HEADS * MLA_QK), lambda i: (i, 0)),
                  pl.BlockSpec((s, HEADS * MLA_QK), lambda i: (i, 0)),
                  pl.BlockSpec((s, HEADS * MLA_V), lambda i: (i, 0))],
        out_specs=pl.BlockSpec((s, HEADS * MLA_V), lambda i: (i, 0)),
        out_shape=jax.ShapeDtypeStruct((b * s, HEADS * MLA_V), BF16),
        scratch_shapes=[pltpu.VMEM((HEADS, MLA_V, tq), F32)],
        compiler_params=_params(("parallel",)),
        name="mla_attention",
    )(q, k, v)


def _merge_kernel(x_ref, nw_ref, wg_ref, ya_ref, yb_ref, yc_ref, wa_ref, wb_ref, wc_ref, wo_ref, o_ref):
    for r0 in range(0, x_ref.shape[0], ROW_GROUP):
        rows = slice(r0, r0 + ROW_GROUP)
        x = x_ref[rows, :]
        h = _rms(x, nw_ref[...]).astype(BF16)
        merged = (_sigmoid(_dot(h, wg_ref[:, 0:D_MODEL])) * _dot(ya_ref[rows, :], wa_ref[...])
                  + _sigmoid(_dot(h, wg_ref[:, D_MODEL:2 * D_MODEL])) * _dot(yb_ref[rows, :], wb_ref[...])
                  + _sigmoid(_dot(h, wg_ref[:, 2 * D_MODEL:3 * D_MODEL])) * _dot(yc_ref[rows, :], wc_ref[...]))
        o_ref[rows, :] = x + _dot(merged.astype(BF16), wo_ref[...])


def _merge(x2, nw, wg, ya, yb, yc, wa, wb, wc, wo, tm):
    t = x2.shape[0]
    const = lambda a: pl.BlockSpec(a.shape, lambda i: (0,) * a.ndim)
    rows = lambda a: pl.BlockSpec((tm, a.shape[1]), lambda i: (i, 0))
    return pl.pallas_call(
        _merge_kernel,
        grid=(t // tm,),
        in_specs=[rows(x2), const(nw), const(wg), rows(ya), rows(yb), rows(yc),
                  const(wa), const(wb), const(wc), const(wo)],
        out_specs=rows(x2),
        out_shape=jax.ShapeDtypeStruct(x2.shape, F32),
        compiler_params=_params(("parallel",)),
        name="gated_merge",
    )(x2, nw, wg, ya, yb, yc, wa, wb, wc, wo)


def _ffn_kernel(x_ref, nw_ref, wi_ref, wo_ref, fw_ref, o_ref, *, final_norm):
    for r0 in range(0, x_ref.shape[0], ROW_GROUP):
        rows = slice(r0, r0 + ROW_GROUP)
        x = x_ref[rows, :]
        h = _rms(x, nw_ref[...]).astype(BF16)
        g = _dot(h, wi_ref[:, 0:D_FF])
        u = _dot(h, wi_ref[:, D_FF:2 * D_FF])
        y = x + _dot((_silu(g) * u).astype(BF16), wo_ref[...])
        if final_norm:
            y = _rms(y, fw_ref[...])
        o_ref[rows, :] = y


def _ffn(x2, nw, wi, wo, fw, tm, final_norm):
    t = x2.shape[0]
    const = lambda a: pl.BlockSpec(a.shape, lambda i: (0,) * a.ndim)
    rows = lambda a: pl.BlockSpec((tm, a.shape[1]), lambda i: (i, 0))
    return pl.pallas_call(
        functools.partial(_ffn_kernel, final_norm=final_norm),
        grid=(t // tm,),
        in_specs=[rows(x2), const(nw), const(wi), const(wo), const(fw)],
        out_specs=rows(x2),
        out_shape=jax.ShapeDtypeStruct(x2.shape, F32),
        compiler_params=_params(("parallel",)),
        name="swiglu_ffn",
    )(x2, nw, wi, wo, fw)


def _in_proj_slices(w_in_l):
    widths = (512, 512, 512, 512, 512, 512, 1024, 1024, MLA_Q_RANK, MLA_KV_RANK, MLA_ROPE, 3 * D_MODEL)
    offs = np.concatenate([[0], np.cumsum(widths)])
    cols = [w_in_l[:, int(offs[i]):int(offs[i + 1])] for i in range(len(widths))]
    w_hg = jnp.concatenate(cols[0:4], axis=1)

    def freq_order(w):
        return w.reshape(D_MODEL, HEADS, 2, RET_K // 2)[..., RET_FREQ_ORDER].reshape(D_MODEL, HEADS * RET_K)

    w_ret = jnp.concatenate([freq_order(cols[4]), freq_order(cols[5]), cols[6], cols[7]], axis=1)
    kr = cols[10]
    kr1, kr2 = kr[:, :32], kr[:, 32:]
    w_c = jnp.concatenate([cols[8], cols[9], kr1, kr1, kr2, kr2], axis=1)
    return w_hg.astype(BF16), w_ret.astype(BF16), w_c.astype(BF16), cols[11].astype(BF16)


def _uq_layout(w_uq_l):
    per = MLA_NOPE + MLA_ROPE
    heads = [w_uq_l[:, hd * per:(hd + 1) * per] for hd in range(HEADS)]
    nope = [hh[:, :MLA_NOPE] for hh in heads]
    pe = [hh[:, MLA_NOPE:] for hh in heads]
    pairs = []
    for p in range(HEADS // 2):
        a, b = pe[2 * p], pe[2 * p + 1]
        pairs += [a[:, :32], b[:, :32], a[:, 32:], b[:, 32:]]
    return jnp.concatenate(nope + pairs, axis=1).astype(BF16)


def _ukv_layout(w_ukv_l):
    per = MLA_NOPE + MLA_V
    heads = [w_ukv_l[:, hd * per:(hd + 1) * per] for hd in range(HEADS)]
    return jnp.concatenate([hh[:, :MLA_NOPE] for hh in heads] + [hh[:, MLA_NOPE:] for hh in heads],
                           axis=1).astype(BF16)


def _block(n, want):
    return want if n % want == 0 else n


def kernel(x, positions, norm_mix_w, w_in, hg_lower_bounds, hg_norm_w, ret_norm_w, mla_q_norm_w, mla_w_uq, mla_kv_norm_w, mla_w_ukv, w_br_a, w_br_b, w_br_c, w_out, norm_ffn_w, w_ffn_in, w_ffn_out, final_norm_w):
    b, s, d = x.shape
    depth = w_in.shape[0]
    assert d == D_MODEL and s % (CHUNKS_PER_ITER * CHUNK) == 0
    t = b * s
    ts = _block(s, ROW_GROUP)
    ts_hg = _block(s, 2 * ROW_GROUP)
    assert ts % ROW_GROUP == 0 and ts_hg % ROW_GROUP == 0
    tm = _block(t, 2 * ROW_GROUP)
    tq = _block(s, 512)
    assert tm % ROW_GROUP == 0

    lb_p = jax.nn.softmax(hg_lower_bounds.astype(F32), axis=0)
    lb_all = jnp.cumsum(lb_p, axis=0) - lb_p[0]

    rc, rs, mc, ms = _rope_tables(positions, _block(t, 512))
    row = lambda a: a.reshape(1, -1).astype(F32)

    x2 = x.reshape(t, d)
    for l in range(depth):
        w_hg, w_ret, w_c, w_g = _in_proj_slices(w_in[l])
        nw = row(norm_mix_w[l])
        ya = _hg_mixer(x2, nw, w_hg, row(lb_all[l]), row(hg_norm_w[l]), b, s, ts_hg)
        yb, q, k, v = _ret_mixer(x2, nw, w_ret, row(ret_norm_w[l]), rc, rs, w_c, row(mla_q_norm_w[l]),
                                 _uq_layout(mla_w_uq[l]), row(mla_kv_norm_w[l]), _ukv_layout(mla_w_ukv[l]),
                                 mc, ms, b, s, ts)
        yc = _mla_attn(q, k, v, b, s, tq)
        x2 = _merge(x2, nw, w_g, ya, yb, yc, w_br_a[l].astype(BF16), w_br_b[l].astype(BF16),
                    w_br_c[l].astype(BF16), w_out[l].astype(BF16), tm)
        x2 = _ffn(x2, row(norm_ffn_w[l]), w_ffn_in[l].astype(BF16), w_ffn_out[l].astype(BF16),
                  row(final_norm_w), tm, final_norm=(l == depth - 1))
    return x2.reshape(b, s, d)
```

```python
import functools

import numpy as np
import jax
import jax.numpy as jnp
from jax import lax
from jax.experimental import pallas as pl
from jax.experimental.pallas import tpu as pltpu

F32 = jnp.float32
BF16 = jnp.bfloat16

D_MODEL = 1024
HEADS = 4
HG_K = 128
HG_V = 128
RET_K = 128
RET_V = 256
MLA_Q_RANK = 256
MLA_KV_RANK = 128
MLA_NOPE = 128
MLA_ROPE = 64
MLA_V = 128
MLA_QK = 256
D_FF = 2816
CHUNK = 64
CHUNKS_PER_ITER = 4
RET_CHUNK = 128
RET_CHUNKS_PER_ITER = 4
ROPE_BASE = 10000.0
EPS = 1e-6
EXP_CLIP = 60.0
MASK_NEG = -1e30
LOG2E = 1.4426950408889634

ROW_GROUP = 512
LANES = 128
VMEM_LIMIT = 56 * 1024 * 1024

LEVELS = (32, 16, 8, 4, 2, 1)


def _rms(x, w):
    return x * lax.rsqrt(jnp.mean(x * x, axis=-1, keepdims=True) + EPS) * w


def _sigmoid(x):
    return 0.5 + 0.5 * jnp.tanh(0.5 * x)


def _silu(x):
    return x * _sigmoid(x)


def _dot(a, b):
    return jnp.dot(a, b, preferred_element_type=F32)


def _dot_nt(a, b):
    return lax.dot_general(a, b, (((1,), (1,)), ((), ())), preferred_element_type=F32)


def _dot_tn(a, b):
    return lax.dot_general(a, b, (((0,), (0,)), ((), ())), preferred_element_type=F32)


def _params(sem):
    return pltpu.CompilerParams(dimension_semantics=sem, vmem_limit_bytes=VMEM_LIMIT)


RET_FREQ_ORDER = np.concatenate([np.arange(0, 64, 2), np.arange(1, 64, 2)])


def _rope_table_kernel(pos_ref, inv_ref, rc_ref, rs_ref, mc_ref, ms_ref):
    pos = pos_ref[...].astype(F32)
    ang = inv_ref[...] * pos
    c = jnp.cos(ang)
    s = jnp.sin(ang)
    c32, s32 = c[:32], s[:32]
    rc_ref[...] = jnp.transpose(jnp.concatenate([c, c], axis=0))
    rs_ref[...] = jnp.transpose(jnp.concatenate([-s, s], axis=0))
    mc_ref[...] = jnp.transpose(jnp.concatenate([c32, c32, c32, c32], axis=0))
    ms_ref[...] = jnp.transpose(jnp.concatenate([-s32, -s32, s32, s32], axis=0))


def _rope_tables(positions, tt):
    t = positions.size
    pos = positions.reshape(1, t)
    inv64 = ROPE_BASE ** (-jnp.arange(64, dtype=F32) / 64)
    inv = inv64[RET_FREQ_ORDER].reshape(64, 1)
    tab = jax.ShapeDtypeStruct((t, LANES), F32)
    return pl.pallas_call(
        _rope_table_kernel,
        grid=(t // tt,),
        in_specs=[pl.BlockSpec((1, tt), lambda i: (0, i)),
                  pl.BlockSpec((64, 1), lambda i: (0, 0))],
        out_specs=[pl.BlockSpec((tt, LANES), lambda i: (i, 0))] * 4,
        out_shape=[tab] * 4,
        compiler_params=_params(("parallel",)),
        name="rope_tables",
    )(pos, inv)


def _rope(t, c, s):
    return t * c + pltpu.roll(t, 64, 1) * s


def _split3(x):
    a = x.astype(BF16)
    r = x - a.astype(F32)
    b = r.astype(BF16)
    c = (r - b.astype(F32)).astype(BF16)
    return a, b, c


def _level_ref(cum, m):
    w = cum.shape[1]
    if m >= 8:
        g = CHUNK // (2 * m)
        c3 = cum.reshape(g, 2 * m, w)
        return jnp.broadcast_to(c3[:, m - 1:m, :], (g, 2 * m, w)).reshape(CHUNK, w)
    c3 = cum.reshape(8, 8, w)

    def row(r):
        return jnp.broadcast_to(c3[:, r:r + 1, :], (8, 8, w))

    if m == 4:
        out = row(3)
    else:
        sub = lax.broadcasted_iota(jnp.int32, (8, 8, w), 1)
        out = jnp.where(sub < 4, row(1), row(5))
    return out.reshape(CHUNK, w)


def _upper_q_lower_k(q, k, m, upper):
    if m < 8:
        return jnp.where(upper[m], q, k)
    return jnp.concatenate([(q if blk % 2 else k)[blk * m:(blk + 1) * m] for blk in range(CHUNK // m)], axis=0)


def _hg_kernel(x_ref, nw_ref, w_ref, lb_ref, gw_ref, tril_ref, masks_ref, o_ref, h_ref, p_scr, st_scr, *, ts):
    @pl.when(pl.program_id(1) == 0)
    def _():
        st_scr[...] = jnp.zeros_like(st_scr)

    def project(r0):
        rows = slice(r0, r0 + ROW_GROUP)
        h = _rms(x_ref[rows, :], nw_ref[...]).astype(BF16)
        h_ref[rows, :] = h
        p_scr[rows, :] = _dot(h, w_ref[...])

    width = HEADS * HG_K
    pw = 2 * HG_K
    tril = tril_ref[...]
    rows = lax.broadcasted_iota(jnp.int32, (CHUNK, pw), 0)
    upper = {m: (rows % (2 * m)) >= m for m in LEVELS}
    zk = jnp.zeros((CHUNK, HG_K), BF16)

    def blockdiag(t):
        return jnp.concatenate([jnp.concatenate([t[:, :HG_K], zk], axis=1),
                                jnp.concatenate([zk, t[:, HG_K:]], axis=1)], axis=0)


    def gates(r0, pair):
        c0 = pair * pw
        lb = lb_ref[:, c0:c0 + pw]
        q = _silu(p_scr[pl.ds(r0, CHUNK), c0:c0 + pw])
        hf = p_scr[pl.ds(r0, CHUNK), width + c0:width + c0 + pw]
        e = jnp.exp2(jnp.abs(hf) * -LOG2E)
        u = jnp.exp2(jnp.minimum(hf * -LOG2E, EXP_CLIP * LOG2E))
        r = 1.0 / (1.0 + e)
        logf = (jnp.minimum(hf, 0.0) + jnp.log((1.0 + lb * u) * r)) * LOG2E
        k = (1.0 - lb) * (jnp.where(hf >= 0.0, e, 1.0) * r)
        cums = _dot(tril, jnp.concatenate(_split3(logf), axis=1))
        cum = cums[:, :pw] + cums[:, pw:2 * pw] + cums[:, 2 * pw:]
        return q, k, logf, cum

    def level_dots(q, k, logf, cum):
        c_last = cum[CHUNK - 1:CHUNK, :]
        qd = (q * jnp.exp2(cum)).astype(BF16)
        kd = (k * jnp.exp2(c_last - cum)).astype(BF16)
        dec = jnp.exp2(c_last)
        prods = [_dot_nt(q.astype(BF16), blockdiag(k.astype(BF16)))]
        for m in LEVELS:
            if m == 1:
                xl = jnp.where(upper[m], q * jnp.exp2(logf), k)
            else:
                xl = _upper_q_lower_k(q, k, m, upper) * jnp.exp2(-jnp.abs(cum - _level_ref(cum, m)))
            xl = xl.astype(BF16)
            prods.append(_dot_nt(xl, blockdiag(xl)))
        return qd, kd, dec, prods

    def finish(r0, pair, qd, kd, dec, prods):
        c0 = pair * pw
        v = p_scr[pl.ds(r0, CHUNK), 2 * width + c0:2 * width + c0 + pw].astype(BF16)
        gate = _silu(p_scr[pl.ds(r0, CHUNK), 3 * width + c0:3 * width + c0 + pw])
        att = masks_ref[0] * prods[0]
        for li in range(1, len(prods)):
            att = att + masks_ref[li] * prods[li]
        o = _dot(att.astype(BF16), blockdiag(v))
        outs = []
        for sub in range(2):
            hs = slice(sub * HG_K, (sub + 1) * HG_K)
            st = st_scr[2 * pair + sub]
            oh = o[:, sub * HG_V:(sub + 1) * HG_V] + _dot_nt(qd[:, hs], st.astype(BF16))
            st_scr[2 * pair + sub] = st * dec[:, hs] + _dot_tn(v[:, hs], kd[:, hs])
            outs.append(oh * lax.rsqrt(jnp.mean(oh * oh, axis=-1, keepdims=True) + EPS))
        y = jnp.concatenate(outs, axis=1) * gw_ref[:, c0:c0 + pw] * gate
        o_ref[pl.ds(r0, CHUNK), c0:c0 + pw] = y.astype(o_ref.dtype)

    for c0 in range(0, ts // CHUNK, CHUNKS_PER_ITER):
        if (c0 * CHUNK) % ROW_GROUP == 0:
            project(c0 * CHUNK)
        units = [((c0 + ci) * CHUNK, pair) for ci in range(CHUNKS_PER_ITER) for pair in range(HEADS // 2)]
        stage1 = [gates(r0, pair) for r0, pair in units]
        stage2 = [level_dots(*vals) for vals in stage1]
        for (r0, pair), vals in zip(units, stage2):
            finish(r0, pair, *vals)


def _level_masks():
    t = np.arange(CHUNK)[:, None]
    s = np.arange(CHUNK)[None, :]
    masks = [(t == s)]
    for m in LEVELS:
        masks.append((t // (2 * m) == s // (2 * m)) & (t % (2 * m) >= m) & (s % (2 * m) < m))
    return jnp.asarray(np.tile(np.stack(masks).astype(np.float32), (1, 1, 2)))


def _hg_mixer(x2, nw, w, lb, gw, b, s, ts):
    width = HEADS * HG_K
    nblk = s // ts
    tril = jnp.asarray(np.tril(np.ones((CHUNK, CHUNK), np.float32)), BF16)
    masks = _level_masks()
    const = lambda shape: pl.BlockSpec(shape, lambda i, j: (0,) * len(shape))
    return pl.pallas_call(
        functools.partial(_hg_kernel, ts=ts),
        grid=(b, nblk),
        in_specs=[pl.BlockSpec((ts, D_MODEL), lambda i, j: (i * nblk + j, 0)),
                  const((1, D_MODEL)), const((D_MODEL, 4 * width)), const((1, width)), const((1, width)),
                  const((CHUNK, CHUNK)), const(masks.shape)],
        out_specs=[pl.BlockSpec((ts, HEADS * HG_V), lambda i, j: (i * nblk + j, 0)),
                   pl.BlockSpec((ts, D_MODEL), lambda i, j: (i * nblk + j, 0))],
        out_shape=[jax.ShapeDtypeStruct((b * s, HEADS * HG_V), BF16), jax.ShapeDtypeStruct((b * s, D_MODEL), BF16)],
        scratch_shapes=[pltpu.VMEM((ts, 4 * width), F32), pltpu.VMEM((HEADS, HG_V, HG_K), F32)],
        compiler_params=_params(("parallel", "arbitrary")),
        name="hgrn2_mixer",
    )(x2, nw, w, lb, gw, tril, masks)


def _ret_kernel(h_ref, w_ref, gw_ref, rc_ref, rs_ref, intra_ref, qdec_ref, kdec_ref, cdec_ref,
                wc_ref, qn_ref, wuq_ref, kvn_ref, wukv_ref, mc_ref, ms_ref,
                o_ref, q_ref, k_ref, v_ref, p_scr, st_scr, kc_scr, ks_scr, *, ts):
    @pl.when(pl.program_id(1) == 0)
    def _():
        st_scr[...] = jnp.zeros_like(st_scr)

    p_scr[...] = _dot(h_ref[...], w_ref[...])
    kc_scr[...] = rc_ref[...] * (RET_K ** -0.5)
    ks_scr[...] = rs_ref[...] * (RET_K ** -0.5)

    qk_w = HEADS * RET_K
    v_w = HEADS * RET_V


    def scores(r0, hd):
        rows = pl.ds(r0, RET_CHUNK)
        q = _rope(p_scr[rows, hd * RET_K:(hd + 1) * RET_K], rc_ref[rows, :], rs_ref[rows, :])
        k = _rope(p_scr[rows, qk_w + hd * RET_K:qk_w + (hd + 1) * RET_K], kc_scr[rows, :], ks_scr[rows, :])
        return q, k, _dot_nt(q.astype(BF16), k.astype(BF16))

    def finish(r0, hd, q, k, qk):
        rows = pl.ds(r0, RET_CHUNK)
        v = p_scr[rows, 2 * qk_w + hd * RET_V:2 * qk_w + (hd + 1) * RET_V].astype(BF16)
        gate = _silu(p_scr[rows, 2 * qk_w + v_w + hd * RET_V:2 * qk_w + v_w + (hd + 1) * RET_V])
        att = (qk * intra_ref[hd]).astype(BF16)
        st = st_scr[hd]
        o = _dot(jnp.concatenate([att, (q * qdec_ref[hd]).astype(BF16)], axis=1),
                 jnp.concatenate([v, st.astype(BF16)], axis=0))
        st_scr[hd] = st * cdec_ref[hd] + _dot_tn((k * kdec_ref[hd]).astype(BF16), v)
        o = o * lax.rsqrt(jnp.mean(o * o, axis=-1, keepdims=True) + EPS)
        y = o * gw_ref[:, hd * RET_V:(hd + 1) * RET_V] * gate
        o_ref[rows, hd * RET_V:(hd + 1) * RET_V] = y.astype(o_ref.dtype)

    for c0 in range(0, ts // RET_CHUNK, RET_CHUNKS_PER_ITER):
        units = [((c0 + ci) * RET_CHUNK, hd) for ci in range(RET_CHUNKS_PER_ITER) for hd in range(HEADS)]
        stage1 = [scores(r0, hd) for r0, hd in units]
        for (r0, hd), vals in zip(units, stage1):
            finish(r0, hd, *vals)

    _mla_project(h_ref[...], wc_ref, qn_ref, wuq_ref, kvn_ref, wukv_ref, mc_ref[...], ms_ref[...],
                 q_ref, k_ref, v_ref)


def _ret_tables():
    log_gamma = jnp.log1p(-jnp.exp2(-5.0 - jnp.arange(HEADS, dtype=F32)))
    idx = jnp.arange(RET_CHUNK, dtype=F32)
    causal = jnp.tril(jnp.ones((RET_CHUNK, RET_CHUNK), dtype=bool))[None]
    lg = log_gamma[:, None, None]
    dist = jnp.where(causal, (idx[:, None] - idx[None, :])[None], 0.0)
    intra = jnp.where(causal, jnp.exp(dist * lg), 0.0)
    q_decay = jnp.exp((idx + 1.0)[None, :] * log_gamma[:, None])[..., None]
    k_decay = jnp.exp((RET_CHUNK - 1.0 - idx)[None, :] * log_gamma[:, None])[..., None]
    chunk_decay = jnp.exp(RET_CHUNK * log_gamma)[:, None, None]
    return (intra,
            jnp.broadcast_to(q_decay, (HEADS, RET_CHUNK, RET_K)),
            jnp.broadcast_to(k_decay, (HEADS, RET_CHUNK, RET_K)),
            jnp.broadcast_to(chunk_decay, (HEADS, 1, RET_V)))


def _ret_mixer(h, w, gw, rc, rs, wc, qn, wuq, kvn, wukv, mc, ms, b, s, ts):
    nblk = s // ts
    t = b * s
    intra, qdec, kdec, cdec = _ret_tables()
    n_proj = 2 * HEADS * RET_K + 2 * HEADS * RET_V
    const = lambda a: pl.BlockSpec(a.shape, lambda i, j: (0,) * a.ndim)
    rows = lambda width: pl.BlockSpec((ts, width), lambda i, j: (i * nblk + j, 0))
    out_widths = (HEADS * RET_V, HEADS * MLA_QK, HEADS * MLA_QK, HEADS * MLA_V)
    return pl.pallas_call(
        functools.partial(_ret_kernel, ts=ts),
        grid=(b, nblk),
        in_specs=[rows(D_MODEL), const(w), const(gw), rows(LANES), rows(LANES),
                  const(intra), const(qdec), const(kdec), const(cdec),
                  const(wc), const(qn), const(wuq), const(kvn), const(wukv), rows(LANES), rows(LANES)],
        out_specs=[rows(wd) for wd in out_widths],
        out_shape=[jax.ShapeDtypeStruct((t, wd), BF16) for wd in out_widths],
        scratch_shapes=[pltpu.VMEM((ts, n_proj), F32), pltpu.VMEM((HEADS, RET_K, RET_V), F32),
                        pltpu.VMEM((ts, LANES), F32), pltpu.VMEM((ts, LANES), F32)],
        compiler_params=_params(("parallel", "arbitrary")),
        name="retention_mla_mixer",
    )(h, w, gw, rc, rs, intra, qdec, kdec, cdec, wc, qn, wuq, kvn, wukv, mc, ms)


def _mla_project(h, wc_ref, qn_ref, wuq_ref, kvn_ref, wukv_ref, cos, sin, q_ref, k_ref, v_ref):
    c = _dot(h, wc_ref[...])
    cq = _rms(c[:, 0:MLA_Q_RANK], qn_ref[...]).astype(BF16)
    qm = _dot(cq, wuq_ref[...]) * ((MLA_NOPE + MLA_ROPE) ** -0.5 * LOG2E)
    ckv = _rms(c[:, MLA_Q_RANK:MLA_Q_RANK + MLA_KV_RANK], kvn_ref[...]).astype(BF16)
    kv = _dot(ckv, wukv_ref[...])
    k_pe = _rope(c[:, MLA_Q_RANK + MLA_KV_RANK:], cos, sin).astype(BF16)
    lane = lax.broadcasted_iota(jnp.int32, k_pe.shape, 1)
    first = (lane % 64) < 32
    nope_w = HEADS * MLA_NOPE
    for pair in range(HEADS // 2):
        q_pe = _rope(qm[:, nope_w + pair * LANES:nope_w + (pair + 1) * LANES], cos, sin)
        for sub in range(2):
            hd = 2 * pair + sub
            keep = first if sub == 0 else jnp.logical_not(first)
            q_ref[:, hd * MLA_QK:hd * MLA_QK + MLA_NOPE] = qm[:, hd * MLA_NOPE:(hd + 1) * MLA_NOPE].astype(BF16)
            q_ref[:, hd * MLA_QK + MLA_NOPE:(hd + 1) * MLA_QK] = jnp.where(keep, q_pe, 0.0).astype(BF16)
    for hd in range(HEADS):
        k_ref[:, hd * MLA_QK:hd * MLA_QK + MLA_NOPE] = kv[:, hd * MLA_NOPE:(hd + 1) * MLA_NOPE].astype(BF16)
        k_ref[:, hd * MLA_QK + MLA_NOPE:(hd + 1) * MLA_QK] = k_pe
    v_ref[...] = kv[:, nope_w:].astype(BF16)


def _mla_attn_kernel(q_ref, k_ref, v_ref, o_ref, acc_scr, *, tq, nq):
    key = lax.broadcasted_iota(jnp.int32, (tq, tq), 0)
    qry = lax.broadcasted_iota(jnp.int32, (tq, tq), 1)
    causal = key <= qry

    def visit(qi, kj, carry):
        ms, ls = carry
        q0, k0 = qi * tq, kj * tq
        new_m, new_l = [], []
        score = lambda hd: _dot_nt(k_ref[k0:k0 + tq, hd * MLA_QK:(hd + 1) * MLA_QK],
                                   q_ref[q0:q0 + tq, hd * MLA_QK:(hd + 1) * MLA_QK])
        scores = {0: score(0), 1: score(1)}
        for hd in range(HEADS):
            if hd + 2 < HEADS:
                scores[hd + 2] = score(hd + 2)
            s_t = scores.pop(hd)
            if kj == qi:
                s_t = jnp.where(causal, s_t, MASK_NEG)
            pv_in = v_ref[k0:k0 + tq, hd * MLA_V:(hd + 1) * MLA_V]
            if kj == 0:
                m_new = jnp.max(s_t, axis=0, keepdims=True)
                p = jnp.exp2(s_t - m_new)
                new_l.append(jnp.sum(p, axis=0, keepdims=True))
                acc_scr[hd] = _dot_tn(pv_in, p.astype(BF16))
            else:
                m_new = jnp.maximum(ms[hd], jnp.max(s_t, axis=0, keepdims=True))
                alpha = jnp.exp2(ms[hd] - m_new)
                p = jnp.exp2(s_t - m_new)
                new_l.append(ls[hd] * alpha + jnp.sum(p, axis=0, keepdims=True))
                acc_scr[hd] = acc_scr[hd] * alpha + _dot_tn(pv_in, p.astype(BF16))
            new_m.append(m_new)
        return tuple(new_m), tuple(new_l)

    for qi in range(nq):
        carry = (None, None)
        for kj in range(qi + 1):
            carry = visit(qi, kj, carry)
        for hd in range(HEADS):
            o_ref[qi * tq:(qi + 1) * tq, hd * MLA_V:(hd + 1) * MLA_V] = jnp.transpose(
                acc_scr[hd] / carry[1][hd]).astype(o_ref.dtype)


def _mla_attn(q, k, v, b, s, tq):
    nq = s // tq
    return pl.pallas_call(
        functools.partial(_mla_attn_kernel, tq=tq, nq=nq),
        grid=(b,),
        in_specs=[pl.BlockSpec((s, HEADS * MLA_QK), lambda i: (i, 0)),
                  pl.BlockSpec((s, HEADS * MLA_QK), lambda i: (i, 0)),
                  pl.BlockSpec((s, HEADS * MLA_V), lambda i: (i, 0))],
        out_specs=pl.BlockSpec((s, HEADS * MLA_V), lambda i: (i, 0)),
        out_shape=jax.ShapeDtypeStruct((b * s, HEADS * MLA_V), BF16),
        scratch_shapes=[pltpu.VMEM((HEADS, MLA_V, tq), F32)],
        compiler_params=_params(("parallel",)),
        name="mla_attention",
    )(q, k, v)


def _merge_kernel(x_ref, h_ref, wg_ref, ya_ref, yb_ref, yc_ref, wa_ref, wb_ref, wc_ref, wo_ref, o_ref):
    for r0 in range(0, x_ref.shape[0], ROW_GROUP):
        rows = slice(r0, r0 + ROW_GROUP)
        h = h_ref[rows, :]
        merged = (_sigmoid(_dot(h, wg_ref[:, 0:D_MODEL])) * _dot(ya_ref[rows, :], wa_ref[...])
                  + _sigmoid(_dot(h, wg_ref[:, D_MODEL:2 * D_MODEL])) * _dot(yb_ref[rows, :], wb_ref[...])
                  + _sigmoid(_dot(h, wg_ref[:, 2 * D_MODEL:3 * D_MODEL])) * _dot(yc_ref[rows, :], wc_ref[...]))
        o_ref[rows, :] = x_ref[rows, :] + _dot(merged.astype(BF16), wo_ref[...])


def _merge(x2, h, wg, ya, yb, yc, wa, wb, wc, wo, tm):
    t = x2.shape[0]
    const = lambda a: pl.BlockSpec(a.shape, lambda i: (0,) * a.ndim)
    rows = lambda a: pl.BlockSpec((tm, a.shape[1]), lambda i: (i, 0))
    return pl.pallas_call(
        _merge_kernel,
        grid=(t // tm,),
        in_specs=[rows(x2), rows(h), const(wg), rows(ya), rows(yb), rows(yc),
                  const(wa), const(wb), const(wc), const(wo)],
        out_specs=rows(x2),
        out_shape=jax.ShapeDtypeStruct(x2.shape, F32),
        compiler_params=_params(("parallel",)),
        name="gated_merge",
    )(x2, h, wg, ya, yb, yc, wa, wb, wc, wo)


def _ffn_kernel(x_ref, nw_ref, wi_ref, wo_ref, fw_ref, o_ref, *, final_norm):
    for r0 in range(0, x_ref.shape[0], ROW_GROUP):
        rows = slice(r0, r0 + ROW_GROUP)
        x = x_ref[rows, :]
        h = _rms(x, nw_ref[...]).astype(BF16)
        g = _dot(h, wi_ref[:, 0:D_FF])
        u = _dot(h, wi_ref[:, D_FF:2 * D_FF])
        y = x + _dot((_silu(g) * u).astype(BF16), wo_ref[...])
        if final_norm:
            y = _rms(y, fw_ref[...])
        o_ref[rows, :] = y


def _ffn(x2, nw, wi, wo, fw, tm, final_norm):
    t = x2.shape[0]
    const = lambda a: pl.BlockSpec(a.shape, lambda i: (0,) * a.ndim)
    rows = lambda a: pl.BlockSpec((tm, a.shape[1]), lambda i: (i, 0))
    return pl.pallas_call(
        functools.partial(_ffn_kernel, final_norm=final_norm),
        grid=(t // tm,),
        in_specs=[rows(x2), const(nw), const(wi), const(wo), const(fw)],
        out_specs=rows(x2),
        out_shape=jax.ShapeDtypeStruct(x2.shape, F32),
        compiler_params=_params(("parallel",)),
        name="swiglu_ffn",
    )(x2, nw, wi, wo, fw)


def _in_proj_slices(w_in_l):
    widths = (512, 512, 512, 512, 512, 512, 1024, 1024, MLA_Q_RANK, MLA_KV_RANK, MLA_ROPE, 3 * D_MODEL)
    offs = np.concatenate([[0], np.cumsum(widths)])
    cols = [w_in_l[:, int(offs[i]):int(offs[i + 1])] for i in range(len(widths))]
    w_hg = jnp.concatenate(cols[0:4], axis=1)

    def freq_order(w):
        return w.reshape(D_MODEL, HEADS, 2, RET_K // 2)[..., RET_FREQ_ORDER].reshape(D_MODEL, HEADS * RET_K)

    w_ret = jnp.concatenate([freq_order(cols[4]), freq_order(cols[5]), cols[6], cols[7]], axis=1)
    kr = cols[10]
    kr1, kr2 = kr[:, :32], kr[:, 32:]
    w_c = jnp.concatenate([cols[8], cols[9], kr1, kr1, kr2, kr2], axis=1)
    return w_hg.astype(BF16), w_ret.astype(BF16), w_c.astype(BF16), cols[11].astype(BF16)


def _uq_layout(w_uq_l):
    per = MLA_NOPE + MLA_ROPE
    heads = [w_uq_l[:, hd * per:(hd + 1) * per] for hd in range(HEADS)]
    nope = [hh[:, :MLA_NOPE] for hh in heads]
    pe = [hh[:, MLA_NOPE:] for hh in heads]
    pairs = []
    for p in range(HEADS // 2):
        a, b = pe[2 * p], pe[2 * p + 1]
        pairs += [a[:, :32], b[:, :32], a[:, 32:], b[:, 32:]]
    return jnp.concatenate(nope + pairs, axis=1).astype(BF16)


def _ukv_layout(w_ukv_l):
    per = MLA_NOPE + MLA_V
    heads = [w_ukv_l[:, hd * per:(hd + 1) * per] for hd in range(HEADS)]
    return jnp.concatenate([hh[:, :MLA_NOPE] for hh in heads] + [hh[:, MLA_NOPE:] for hh in heads],
                           axis=1).astype(BF16)


def _block(n, want):
    return want if n % want == 0 else n


def kernel(x, positions, norm_mix_w, w_in, hg_lower_bounds, hg_norm_w, ret_norm_w, mla_q_norm_w, mla_w_uq, mla_kv_norm_w, mla_w_ukv, w_br_a, w_br_b, w_br_c, w_out, norm_ffn_w, w_ffn_in, w_ffn_out, final_norm_w):
    b, s, d = x.shape
    depth = w_in.shape[0]
    assert d == D_MODEL and s % (CHUNKS_PER_ITER * CHUNK) == 0
    t = b * s
    ts = _block(s, ROW_GROUP)
    ts_hg = _block(s, 2 * ROW_GROUP)
    assert ts % ROW_GROUP == 0 and ts_hg % ROW_GROUP == 0
    tm = _block(t, 2 * ROW_GROUP)
    tq = _block(s, 512)
    assert tm % ROW_GROUP == 0

    lb_p = jax.nn.softmax(hg_lower_bounds.astype(F32), axis=0)
    lb_all = jnp.cumsum(lb_p, axis=0) - lb_p[0]

    rc, rs, mc, ms = _rope_tables(positions, _block(t, 512))
    row = lambda a: a.reshape(1, -1).astype(F32)

    x2 = x.reshape(t, d)
    for l in range(depth):
        w_hg, w_ret, w_c, w_g = _in_proj_slices(w_in[l])
        nw = row(norm_mix_w[l])
        ya, h = _hg_mixer(x2, nw, w_hg, row(lb_all[l]), row(hg_norm_w[l]), b, s, ts_hg)
        yb, q, k, v = _ret_mixer(h, w_ret, row(ret_norm_w[l]), rc, rs, w_c, row(mla_q_norm_w[l]),
                                 _uq_layout(mla_w_uq[l]), row(mla_kv_norm_w[l]), _ukv_layout(mla_w_ukv[l]),
                                 mc, ms, b, s, ts)
        yc = _mla_attn(q, k, v, b, s, tq)
        x2 = _merge(x2, h, w_g, ya, yb, yc, w_br_a[l].astype(BF16), w_br_b[l].astype(BF16),
                    w_br_c[l].astype(BF16), w_out[l].astype(BF16), tm)
        x2 = _ffn(x2, row(norm_ffn_w[l]), w_ffn_in[l].astype(BF16), w_ffn_out[l].astype(BF16),
                  row(final_norm_w), tm, final_norm=(l == depth - 1))
    return x2.reshape(b, s, d)
```

```python
import functools

import numpy as np
import jax
import jax.numpy as jnp
from jax import lax
from jax.experimental import pallas as pl
from jax.experimental.pallas import tpu as pltpu

F32 = jnp.float32
BF16 = jnp.bfloat16

D_MODEL = 1024
HEADS = 4
HG_K = 128
HG_V = 128
RET_K = 128
RET_V = 256
MLA_Q_RANK = 256
MLA_KV_RANK = 128
MLA_NOPE = 128
MLA_ROPE = 64
MLA_V = 128
MLA_QK = 256
D_FF = 2816
CHUNK = 64
CHUNKS_PER_ITER = 4
RET_CHUNK = 128
ROPE_BASE = 10000.0
EPS = 1e-6
EXP_CLIP = 60.0
MASK_NEG = -1e30
LOG2E = 1.4426950408889634

ROW_GROUP = 512
LANES = 128
VMEM_LIMIT = 56 * 1024 * 1024

LEVELS = (32, 16, 8, 4, 2, 1)


def _rms(x, w):
    return x * lax.rsqrt(jnp.mean(x * x, axis=-1, keepdims=True) + EPS) * w


def _sigmoid(x):
    return 0.5 + 0.5 * jnp.tanh(0.5 * x)


def _silu(x):
    return x * _sigmoid(x)


def _dot(a, b):
    return jnp.dot(a, b, preferred_element_type=F32)


def _dot_nt(a, b):
    return lax.dot_general(a, b, (((1,), (1,)), ((), ())), preferred_element_type=F32)


def _dot_tn(a, b):
    return lax.dot_general(a, b, (((0,), (0,)), ((), ())), preferred_element_type=F32)


def _params(sem):
    return pltpu.CompilerParams(dimension_semantics=sem, vmem_limit_bytes=VMEM_LIMIT)


RET_FREQ_ORDER = np.concatenate([np.arange(0, 64, 2), np.arange(1, 64, 2)])


def _rope_table_kernel(pos_ref, inv_ref, rc_ref, rs_ref, mc_ref, ms_ref):
    pos = pos_ref[...].astype(F32)
    ang = inv_ref[...] * pos
    c = jnp.cos(ang)
    s = jnp.sin(ang)
    c32, s32 = c[:32], s[:32]
    rc_ref[...] = jnp.transpose(jnp.concatenate([c, c], axis=0))
    rs_ref[...] = jnp.transpose(jnp.concatenate([-s, s], axis=0))
    mc_ref[...] = jnp.transpose(jnp.concatenate([c32, c32, c32, c32], axis=0))
    ms_ref[...] = jnp.transpose(jnp.concatenate([-s32, -s32, s32, s32], axis=0))


def _rope_tables(positions, tt):
    t = positions.size
    pos = positions.reshape(1, t)
    inv64 = ROPE_BASE ** (-jnp.arange(64, dtype=F32) / 64)
    inv = inv64[RET_FREQ_ORDER].reshape(64, 1)
    tab = jax.ShapeDtypeStruct((t, LANES), F32)
    return pl.pallas_call(
        _rope_table_kernel,
        grid=(t // tt,),
        in_specs=[pl.BlockSpec((1, tt), lambda i: (0, i)),
                  pl.BlockSpec((64, 1), lambda i: (0, 0))],
        out_specs=[pl.BlockSpec((tt, LANES), lambda i: (i, 0))] * 4,
        out_shape=[tab] * 4,
        compiler_params=_params(("parallel",)),
        name="rope_tables",
    )(pos, inv)


def _rope(t, c, s):
    return t * c + pltpu.roll(t, 64, 1) * s


def _split3(x):
    a = x.astype(BF16)
    r = x - a.astype(F32)
    b = r.astype(BF16)
    c = (r - b.astype(F32)).astype(BF16)
    return a, b, c


def _level_ref(cum, m):
    w = cum.shape[1]
    if m >= 8:
        g = CHUNK // (2 * m)
        c3 = cum.reshape(g, 2 * m, w)
        return jnp.broadcast_to(c3[:, m - 1:m, :], (g, 2 * m, w)).reshape(CHUNK, w)
    c3 = cum.reshape(8, 8, w)

    def row(r):
        return jnp.broadcast_to(c3[:, r:r + 1, :], (8, 8, w))

    if m == 4:
        out = row(3)
    else:
        sub = lax.broadcasted_iota(jnp.int32, (8, 8, w), 1)
        out = jnp.where(sub < 4, row(1), row(5))
    return out.reshape(CHUNK, w)


def _upper_q_lower_k(q, k, m, upper):
    if m < 8:
        return jnp.where(upper[m], q, k)
    return jnp.concatenate([(q if blk % 2 else k)[blk * m:(blk + 1) * m] for blk in range(CHUNK // m)], axis=0)


def _hg_kernel(x_ref, nw_ref, w_ref, lb_ref, gw_ref, tril_ref, masks_ref, o_ref, h_ref, p_scr, st_scr, *, ts):
    @pl.when(pl.program_id(1) == 0)
    def _():
        st_scr[...] = jnp.zeros_like(st_scr)

    def project(r0):
        rows = slice(r0, r0 + ROW_GROUP)
        h = _rms(x_ref[rows, :], nw_ref[...]).astype(BF16)
        h_ref[rows, :] = h
        p_scr[rows, :] = _dot(h, w_ref[...])

    width = HEADS * HG_K
    pw = 2 * HG_K
    tril = tril_ref[...]
    rows = lax.broadcasted_iota(jnp.int32, (CHUNK, pw), 0)
    upper = {m: (rows % (2 * m)) >= m for m in LEVELS}
    zk = jnp.zeros((CHUNK, HG_K), BF16)

    def blockdiag(t):
        return jnp.concatenate([jnp.concatenate([t[:, :HG_K], zk], axis=1),
                                jnp.concatenate([zk, t[:, HG_K:]], axis=1)], axis=0)


    def gates(r0, pair):
        c0 = pair * pw
        lb = lb_ref[:, c0:c0 + pw]
        q = _silu(p_scr[pl.ds(r0, CHUNK), c0:c0 + pw])
        hf = p_scr[pl.ds(r0, CHUNK), width + c0:width + c0 + pw]
        e = jnp.exp2(jnp.abs(hf) * -LOG2E)
        u = jnp.exp2(jnp.minimum(hf * -LOG2E, EXP_CLIP * LOG2E))
        r = 1.0 / (1.0 + e)
        logf = (jnp.minimum(hf, 0.0) + jnp.log((1.0 + lb * u) * r)) * LOG2E
        k = (1.0 - lb) * (jnp.where(hf >= 0.0, e, 1.0) * r)
        cums = _dot(tril, jnp.concatenate(_split3(logf), axis=1))
        cum = cums[:, :pw] + cums[:, pw:2 * pw] + cums[:, 2 * pw:]
        return q, k, logf, cum

    def level_dots(q, k, logf, cum):
        c_last = cum[CHUNK - 1:CHUNK, :]
        qd = (q * jnp.exp2(cum)).astype(BF16)
        kd = (k * jnp.exp2(c_last - cum)).astype(BF16)
        dec = jnp.exp2(c_last)
        prods = [_dot_nt(q.astype(BF16), blockdiag(k.astype(BF16)))]
        for m in LEVELS:
            if m == 1:
                xl = jnp.where(upper[m], q * jnp.exp2(logf), k)
            else:
                xl = _upper_q_lower_k(q, k, m, upper) * jnp.exp2(-jnp.abs(cum - _level_ref(cum, m)))
            xl = xl.astype(BF16)
            prods.append(_dot_nt(xl, blockdiag(xl)))
        return qd, kd, dec, prods

    def finish(r0, pair, qd, kd, dec, prods):
        c0 = pair * pw
        v = p_scr[pl.ds(r0, CHUNK), 2 * width + c0:2 * width + c0 + pw].astype(BF16)
        gate = _silu(p_scr[pl.ds(r0, CHUNK), 3 * width + c0:3 * width + c0 + pw])
        att = masks_ref[0] * prods[0]
        for li in range(1, len(prods)):
            att = att + masks_ref[li] * prods[li]
        o = _dot(att.astype(BF16), blockdiag(v))
        outs = []
        for sub in range(2):
            hs = slice(sub * HG_K, (sub + 1) * HG_K)
            st = st_scr[2 * pair + sub]
            oh = o[:, sub * HG_V:(sub + 1) * HG_V] + _dot_nt(qd[:, hs], st.astype(BF16))
            st_scr[2 * pair + sub] = st * dec[:, hs] + _dot_tn(v[:, hs], kd[:, hs])
            outs.append(oh * lax.rsqrt(jnp.mean(oh * oh, axis=-1, keepdims=True) + EPS))
        y = jnp.concatenate(outs, axis=1) * gw_ref[:, c0:c0 + pw] * gate
        o_ref[pl.ds(r0, CHUNK), c0:c0 + pw] = y.astype(o_ref.dtype)

    for c0 in range(0, ts // CHUNK, CHUNKS_PER_ITER):
        if (c0 * CHUNK) % ROW_GROUP == 0:
            project(c0 * CHUNK)
        units = [((c0 + ci) * CHUNK, pair) for ci in range(CHUNKS_PER_ITER) for pair in range(HEADS // 2)]
        stage1 = [gates(r0, pair) for r0, pair in units]
        stage2 = [level_dots(*vals) for vals in stage1]
        for (r0, pair), vals in zip(units, stage2):
            finish(r0, pair, *vals)


def _level_masks():
    t = np.arange(CHUNK)[:, None]
    s = np.arange(CHUNK)[None, :]
    masks = [(t == s)]
    for m in LEVELS:
        masks.append((t // (2 * m) == s // (2 * m)) & (t % (2 * m) >= m) & (s % (2 * m) < m))
    return jnp.asarray(np.tile(np.stack(masks).astype(np.float32), (1, 1, 2)))


def _hg_mixer(x2, nw, w, lb, gw, b, s, ts):
    width = HEADS * HG_K
    nblk = s // ts
    tril = jnp.asarray(np.tril(np.ones((CHUNK, CHUNK), np.float32)), BF16)
    masks = _level_masks()
    const = lambda shape: pl.BlockSpec(shape, lambda i, j: (0,) * len(shape))
    return pl.pallas_call(
        functools.partial(_hg_kernel, ts=ts),
        grid=(b, nblk),
        in_specs=[pl.BlockSpec((ts, D_MODEL), lambda i, j: (i * nblk + j, 0)),
                  const((1, D_MODEL)), const((D_MODEL, 4 * width)), const((1, width)), const((1, width)),
                  const((CHUNK, CHUNK)), const(masks.shape)],
        out_specs=[pl.BlockSpec((ts, HEADS * HG_V), lambda i, j: (i * nblk + j, 0)),
                   pl.BlockSpec((ts, D_MODEL), lambda i, j: (i * nblk + j, 0))],
        out_shape=[jax.ShapeDtypeStruct((b * s, HEADS * HG_V), BF16), jax.ShapeDtypeStruct((b * s, D_MODEL), BF16)],
        scratch_shapes=[pltpu.VMEM((ts, 4 * width), F32), pltpu.VMEM((HEADS, HG_V, HG_K), F32)],
        compiler_params=_params(("parallel", "arbitrary")),
        name="hgrn2_mixer",
    )(x2, nw, w, lb, gw, tril, masks)


def _ret_kernel(h_ref, w_ref, gw_ref, rc_ref, rs_ref, intra_ref, qdec_ref, kdec_ref, cdec_ref,
                wc_ref, qn_ref, wuq_ref, kvn_ref, wukv_ref, mc_ref, ms_ref,
                o_ref, q_ref, k_ref, v_ref, p_scr, st_scr, kc_scr, ks_scr, *, ts):
    @pl.when(pl.program_id(1) == 0)
    def _():
        st_scr[...] = jnp.zeros_like(st_scr)

    def project(rows):
        p_scr[rows, :] = _dot(h_ref[rows, :], w_ref[...])
        kc_scr[rows, :] = rc_ref[rows, :] * (RET_K ** -0.5)
        ks_scr[rows, :] = rs_ref[rows, :] * (RET_K ** -0.5)

    qk_w = HEADS * RET_K
    v_w = HEADS * RET_V


    def scores(r0, hd):
        rows = pl.ds(r0, RET_CHUNK)
        q = _rope(p_scr[rows, hd * RET_K:(hd + 1) * RET_K], rc_ref[rows, :], rs_ref[rows, :])
        k = _rope(p_scr[rows, qk_w + hd * RET_K:qk_w + (hd + 1) * RET_K], kc_scr[rows, :], ks_scr[rows, :])
        return q, k, _dot_nt(q.astype(BF16), k.astype(BF16))

    def finish(r0, hd, q, k, qk):
        rows = pl.ds(r0, RET_CHUNK)
        v = p_scr[rows, 2 * qk_w + hd * RET_V:2 * qk_w + (hd + 1) * RET_V].astype(BF16)
        gate = _silu(p_scr[rows, 2 * qk_w + v_w + hd * RET_V:2 * qk_w + v_w + (hd + 1) * RET_V])
        att = (qk * intra_ref[hd]).astype(BF16)
        st = st_scr[hd]
        o = _dot(jnp.concatenate([att, (q * qdec_ref[hd]).astype(BF16)], axis=1),
                 jnp.concatenate([v, st.astype(BF16)], axis=0))
        st_scr[hd] = st * cdec_ref[hd] + _dot_tn((k * kdec_ref[hd]).astype(BF16), v)
        o = o * lax.rsqrt(jnp.mean(o * o, axis=-1, keepdims=True) + EPS)
        y = o * gw_ref[:, hd * RET_V:(hd + 1) * RET_V] * gate
        o_ref[rows, hd * RET_V:(hd + 1) * RET_V] = y.astype(o_ref.dtype)

    for g0 in range(0, ts, ROW_GROUP):
        rows = slice(g0, g0 + ROW_GROUP)
        project(rows)
        units = [(g0 + ci * RET_CHUNK, hd) for ci in range(ROW_GROUP // RET_CHUNK) for hd in range(HEADS)]
        stage1 = [scores(r0, hd) for r0, hd in units]
        for (r0, hd), vals in zip(units, stage1):
            finish(r0, hd, *vals)
        _mla_project(h_ref[rows, :], wc_ref, qn_ref, wuq_ref, kvn_ref, wukv_ref, mc_ref[rows, :], ms_ref[rows, :],
                     q_ref, k_ref, v_ref, rows)


def _ret_tables():
    log_gamma = jnp.log1p(-jnp.exp2(-5.0 - jnp.arange(HEADS, dtype=F32)))
    idx = jnp.arange(RET_CHUNK, dtype=F32)
    causal = jnp.tril(jnp.ones((RET_CHUNK, RET_CHUNK), dtype=bool))[None]
    lg = log_gamma[:, None, None]
    dist = jnp.where(causal, (idx[:, None] - idx[None, :])[None], 0.0)
    intra = jnp.where(causal, jnp.exp(dist * lg), 0.0)
    q_decay = jnp.exp((idx + 1.0)[None, :] * log_gamma[:, None])[..., None]
    k_decay = jnp.exp((RET_CHUNK - 1.0 - idx)[None, :] * log_gamma[:, None])[..., None]
    chunk_decay = jnp.exp(RET_CHUNK * log_gamma)[:, None, None]
    return (intra,
            jnp.broadcast_to(q_decay, (HEADS, RET_CHUNK, RET_K)),
            jnp.broadcast_to(k_decay, (HEADS, RET_CHUNK, RET_K)),
            jnp.broadcast_to(chunk_decay, (HEADS, 1, RET_V)))


def _ret_mixer(h, w, gw, rc, rs, wc, qn, wuq, kvn, wukv, mc, ms, b, s, ts):
    nblk = s // ts
    t = b * s
    intra, qdec, kdec, cdec = _ret_tables()
    n_proj = 2 * HEADS * RET_K + 2 * HEADS * RET_V
    const = lambda a: pl.BlockSpec(a.shape, lambda i, j: (0,) * a.ndim)
    rows = lambda width: pl.BlockSpec((ts, width), lambda i, j: (i * nblk + j, 0))
    out_widths = (HEADS * RET_V, HEADS * MLA_QK, HEADS * MLA_QK, HEADS * MLA_V)
    return pl.pallas_call(
        functools.partial(_ret_kernel, ts=ts),
        grid=(b, nblk),
        in_specs=[rows(D_MODEL), const(w), const(gw), rows(LANES), rows(LANES),
                  const(intra), const(qdec), const(kdec), const(cdec),
                  const(wc), const(qn), const(wuq), const(kvn), const(wukv), rows(LANES), rows(LANES)],
        out_specs=[rows(wd) for wd in out_widths],
        out_shape=[jax.ShapeDtypeStruct((t, wd), BF16) for wd in out_widths],
        scratch_shapes=[pltpu.VMEM((ts, n_proj), F32), pltpu.VMEM((HEADS, RET_K, RET_V), F32),
                        pltpu.VMEM((ts, LANES), F32), pltpu.VMEM((ts, LANES), F32)],
        compiler_params=_params(("parallel", "arbitrary")),
        name="retention_mla_mixer",
    )(h, w, gw, rc, rs, intra, qdec, kdec, cdec, wc, qn, wuq, kvn, wukv, mc, ms)


def _mla_project(h, wc_ref, qn_ref, wuq_ref, kvn_ref, wukv_ref, cos, sin, q_ref, k_ref, v_ref, rows):
    c = _dot(h, wc_ref[...])
    cq = _rms(c[:, 0:MLA_Q_RANK], qn_ref[...]).astype(BF16)
    qm = _dot(cq, wuq_ref[...]) * ((MLA_NOPE + MLA_ROPE) ** -0.5 * LOG2E)
    ckv = _rms(c[:, MLA_Q_RANK:MLA_Q_RANK + MLA_KV_RANK], kvn_ref[...]).astype(BF16)
    kv = _dot(ckv, wukv_ref[...])
    k_pe = _rope(c[:, MLA_Q_RANK + MLA_KV_RANK:], cos, sin).astype(BF16)
    lane = lax.broadcasted_iota(jnp.int32, k_pe.shape, 1)
    first = (lane % 64) < 32
    nope_w = HEADS * MLA_NOPE
    for pair in range(HEADS // 2):
        q_pe = _rope(qm[:, nope_w + pair * LANES:nope_w + (pair + 1) * LANES], cos, sin)
        for sub in range(2):
            hd = 2 * pair + sub
            keep = first if sub == 0 else jnp.logical_not(first)
            q_ref[rows, hd * MLA_QK:hd * MLA_QK + MLA_NOPE] = qm[:, hd * MLA_NOPE:(hd + 1) * MLA_NOPE].astype(BF16)
            q_ref[rows, hd * MLA_QK + MLA_NOPE:(hd + 1) * MLA_QK] = jnp.where(keep, q_pe, 0.0).astype(BF16)
    for hd in range(HEADS):
        k_ref[rows, hd * MLA_QK:hd * MLA_QK + MLA_NOPE] = kv[:, hd * MLA_NOPE:(hd + 1) * MLA_NOPE].astype(BF16)
        k_ref[rows, hd * MLA_QK + MLA_NOPE:(hd + 1) * MLA_QK] = k_pe
    v_ref[rows, :] = kv[:, nope_w:].astype(BF16)


def _mla_attn_kernel(q_ref, k_ref, v_ref, o_ref, acc_scr, *, tq, nq):
    key = lax.broadcasted_iota(jnp.int32, (tq, tq), 0)
    qry = lax.broadcasted_iota(jnp.int32, (tq, tq), 1)
    causal = key <= qry

    def visit(qi, kj, carry):
        ms, ls = carry
        q0, k0 = qi * tq, kj * tq
        new_m, new_l = [], []
        score = lambda hd: _dot_nt(k_ref[k0:k0 + tq, hd * MLA_QK:(hd + 1) * MLA_QK],
                                   q_ref[q0:q0 + tq, hd * MLA_QK:(hd + 1) * MLA_QK])
        scores = {0: score(0), 1: score(1)}
        for hd in range(HEADS):
            if hd + 2 < HEADS:
                scores[hd + 2] = score(hd + 2)
            s_t = scores.pop(hd)
            if kj == qi:
                s_t = jnp.where(causal, s_t, MASK_NEG)
            pv_in = v_ref[k0:k0 + tq, hd * MLA_V:(hd + 1) * MLA_V]
            if kj == 0:
                m_new = jnp.max(s_t, axis=0, keepdims=True)
                p = jnp.exp2(s_t - m_new)
                new_l.append(jnp.sum(p, axis=0, keepdims=True))
                acc_scr[hd] = _dot_tn(pv_in, p.astype(BF16))
            else:
                m_new = jnp.maximum(ms[hd], jnp.max(s_t, axis=0, keepdims=True))
                alpha = jnp.exp2(ms[hd] - m_new)
                p = jnp.exp2(s_t - m_new)
                new_l.append(ls[hd] * alpha + jnp.sum(p, axis=0, keepdims=True))
                acc_scr[hd] = acc_scr[hd] * alpha + _dot_tn(pv_in, p.astype(BF16))
            new_m.append(m_new)
        return tuple(new_m), tuple(new_l)

    for qi in range(nq):
        carry = (None, None)
        for kj in range(qi + 1):
            carry = visit(qi, kj, carry)
        for hd in range(HEADS):
            o_ref[qi * tq:(qi + 1) * tq, hd * MLA_V:(hd + 1) * MLA_V] = jnp.transpose(
                acc_scr[hd] / carry[1][hd]).astype(o_ref.dtype)


def _mla_attn(q, k, v, b, s, tq):
    nq = s // tq
    return pl.pallas_call(
        functools.partial(_mla_attn_kernel, tq=tq, nq=nq),
        grid=(b,),
        in_specs=[pl.BlockSpec((s, HEADS * MLA_QK), lambda i: (i, 0)),
                  pl.BlockSpec((s, HEADS * MLA_QK), lambda i: (i, 0)),
                  pl.BlockSpec((s, HEADS * MLA_V), lambda i: (i, 0))],
        out_specs=pl.BlockSpec((s, HEADS * MLA_V), lambda i: (i, 0)),
        out_shape=jax.ShapeDtypeStruct((b * s, HEADS * MLA_V), BF16),
        scratch_shapes=[pltpu.VMEM((HEADS, MLA_V, tq), F32)],
        compiler_params=_params(("parallel",)),
        name="mla_attention",
    )(q, k, v)


def _merge_kernel(x_ref, h_ref, wg_ref, ya_ref, yb_ref, yc_ref, wa_ref, wb_ref, wc_ref, wo_ref, o_ref):
    for r0 in range(0, x_ref.shape[0], ROW_GROUP):
        rows = slice(r0, r0 + ROW_GROUP)
        h = h_ref[rows, :]
        merged = (_sigmoid(_dot(h, wg_ref[:, 0:D_MODEL])) * _dot(ya_ref[rows, :], wa_ref[...])
                  + _sigmoid(_dot(h, wg_ref[:, D_MODEL:2 * D_MODEL])) * _dot(yb_ref[rows, :], wb_ref[...])
                  + _sigmoid(_dot(h, wg_ref[:, 2 * D_MODEL:3 * D_MODEL])) * _dot(yc_ref[rows, :], wc_ref[...]))
        o_ref[rows, :] = x_ref[rows, :] + _dot(merged.astype(BF16), wo_ref[...])


def _merge(x2, h, wg, ya, yb, yc, wa, wb, wc, wo, tm):
    t = x2.shape[0]
    const = lambda a: pl.BlockSpec(a.shape, lambda i: (0,) * a.ndim)
    rows = lambda a: pl.BlockSpec((tm, a.shape[1]), lambda i: (i, 0))
    return pl.pallas_call(
        _merge_kernel,
        grid=(t // tm,),
        in_specs=[rows(x2), rows(h), const(wg), rows(ya), rows(yb), rows(yc),
                  const(wa), const(wb), const(wc), const(wo)],
        out_specs=rows(x2),
        out_shape=jax.ShapeDtypeStruct(x2.shape, F32),
        compiler_params=_params(("parallel",)),
        name="gated_merge",
    )(x2, h, wg, ya, yb, yc, wa, wb, wc, wo)


def _ffn_kernel(x_ref, nw_ref, wi_ref, wo_ref, fw_ref, o_ref, *, final_norm):
    for r0 in range(0, x_ref.shape[0], ROW_GROUP):
        rows = slice(r0, r0 + ROW_GROUP)
        x = x_ref[rows, :]
        h = _rms(x, nw_ref[...]).astype(BF16)
        g = _dot(h, wi_ref[:, 0:D_FF])
        u = _dot(h, wi_ref[:, D_FF:2 * D_FF])
        y = x + _dot((_silu(g) * u).astype(BF16), wo_ref[...])
        if final_norm:
            y = _rms(y, fw_ref[...])
        o_ref[rows, :] = y


def _ffn(x2, nw, wi, wo, fw, tm, final_norm):
    t = x2.shape[0]
    const = lambda a: pl.BlockSpec(a.shape, lambda i: (0,) * a.ndim)
    rows = lambda a: pl.BlockSpec((tm, a.shape[1]), lambda i: (i, 0))
    return pl.pallas_call(
        functools.partial(_ffn_kernel, final_norm=final_norm),
        grid=(t // tm,),
        in_specs=[rows(x2), const(nw), const(wi), const(wo), const(fw)],
        out_specs=rows(x2),
        out_shape=jax.ShapeDtypeStruct(x2.shape, F32),
        compiler_params=_params(("parallel",)),
        name="swiglu_ffn",
    )(x2, nw, wi, wo, fw)


def _in_proj_slices(w_in_l):
    widths = (512, 512, 512, 512, 512, 512, 1024, 1024, MLA_Q_RANK, MLA_KV_RANK, MLA_ROPE, 3 * D_MODEL)
    offs = np.concatenate([[0], np.cumsum(widths)])
    cols = [w_in_l[:, int(offs[i]):int(offs[i + 1])] for i in range(len(widths))]
    w_hg = jnp.concatenate(cols[0:4], axis=1)

    def freq_order(w):
        return w.reshape(D_MODEL, HEADS, 2, RET_K // 2)[..., RET_FREQ_ORDER].reshape(D_MODEL, HEADS * RET_K)

    w_ret = jnp.concatenate([freq_order(cols[4]), freq_order(cols[5]), cols[6], cols[7]], axis=1)
    kr = cols[10]
    kr1, kr2 = kr[:, :32], kr[:, 32:]
    w_c = jnp.concatenate([cols[8], cols[9], kr1, kr1, kr2, kr2], axis=1)
    return w_hg.astype(BF16), w_ret.astype(BF16), w_c.astype(BF16), cols[11].astype(BF16)


def _uq_layout(w_uq_l):
    per = MLA_NOPE + MLA_ROPE
    heads = [w_uq_l[:, hd * per:(hd + 1) * per] for hd in range(HEADS)]
    nope = [hh[:, :MLA_NOPE] for hh in heads]
    pe = [hh[:, MLA_NOPE:] for hh in heads]
    pairs = []
    for p in range(HEADS // 2):
        a, b = pe[2 * p], pe[2 * p + 1]
        pairs += [a[:, :32], b[:, :32], a[:, 32:], b[:, 32:]]
    return jnp.concatenate(nope + pairs, axis=1).astype(BF16)


def _ukv_layout(w_ukv_l):
    per = MLA_NOPE + MLA_V
    heads = [w_ukv_l[:, hd * per:(hd + 1) * per] for hd in range(HEADS)]
    return jnp.concatenate([hh[:, :MLA_NOPE] for hh in heads] + [hh[:, MLA_NOPE:] for hh in heads],
                           axis=1).astype(BF16)


def _block(n, want):
    return want if n % want == 0 else n


def kernel(x, positions, norm_mix_w, w_in, hg_lower_bounds, hg_norm_w, ret_norm_w, mla_q_norm_w, mla_w_uq, mla_kv_norm_w, mla_w_ukv, w_br_a, w_br_b, w_br_c, w_out, norm_ffn_w, w_ffn_in, w_ffn_out, final_norm_w):
    b, s, d = x.shape
    depth = w_in.shape[0]
    assert d == D_MODEL and s % (CHUNKS_PER_ITER * CHUNK) == 0
    t = b * s
    ts = _block(s, 2 * ROW_GROUP)
    assert ts % ROW_GROUP == 0
    tm = _block(t, 2 * ROW_GROUP)
    tq = _block(s, 512)
    assert tm % ROW_GROUP == 0

    lb_p = jax.nn.softmax(hg_lower_bounds.astype(F32), axis=0)
    lb_all = jnp.cumsum(lb_p, axis=0) - lb_p[0]

    rc, rs, mc, ms = _rope_tables(positions, _block(t, 512))
    row = lambda a: a.reshape(1, -1).astype(F32)

    x2 = x.reshape(t, d)
    for l in range(depth):
        w_hg, w_ret, w_c, w_g = _in_proj_slices(w_in[l])
        nw = row(norm_mix_w[l])
        ya, h = _hg_mixer(x2, nw, w_hg, row(lb_all[l]), row(hg_norm_w[l]), b, s, ts)
        yb, q, k, v = _ret_mixer(h, w_ret, row(ret_norm_w[l]), rc, rs, w_c, row(mla_q_norm_w[l]),
                                 _uq_layout(mla_w_uq[l]), row(mla_kv_norm_w[l]), _ukv_layout(mla_w_ukv[l]),
                                 mc, ms, b, s, ts)
        yc = _mla_attn(q, k, v, b, s, tq)
        x2 = _merge(x2, h, w_g, ya, yb, yc, w_br_a[l].astype(BF16), w_br_b[l].astype(BF16),
                    w_br_c[l].astype(BF16), w_out[l].astype(BF16), tm)
        x2 = _ffn(x2, row(norm_ffn_w[l]), w_ffn_in[l].astype(BF16), w_ffn_out[l].astype(BF16),
                  row(final_norm_w), tm, final_norm=(l == depth - 1))
    return x2.reshape(b, s, d)
```

```python
import functools

import numpy as np
import jax
import jax.numpy as jnp
from jax import lax
from jax.experimental import pallas as pl
from jax.experimental.pallas import tpu as pltpu

F32 = jnp.float32
BF16 = jnp.bfloat16

D_MODEL = 1024
HEADS = 4
HG_K = 128
HG_V = 128
RET_K = 128
RET_V = 256
MLA_Q_RANK = 256
MLA_KV_RANK = 128
MLA_NOPE = 128
MLA_ROPE = 64
MLA_V = 128
MLA_QK = 256
D_FF = 2816
CHUNK = 64
CHUNKS_PER_ITER = 4
RET_CHUNK = 128
ROPE_BASE = 10000.0
EPS = 1e-6
EXP_CLIP = 60.0
MASK_NEG = -1e30
LOG2E = 1.4426950408889634

ROW_GROUP = 512
LANES = 128
VMEM_LIMIT = 56 * 1024 * 1024

LEVELS = (32, 16, 8, 4, 2, 1)


def _rms(x, w):
    return x * lax.rsqrt(jnp.mean(x * x, axis=-1, keepdims=True) + EPS) * w


def _sigmoid(x):
    return 0.5 + 0.5 * jnp.tanh(0.5 * x)


def _silu(x):
    return x * _sigmoid(x)


def _dot(a, b):
    return jnp.dot(a, b, preferred_element_type=F32)


def _dot_nt(a, b):
    return lax.dot_general(a, b, (((1,), (1,)), ((), ())), preferred_element_type=F32)


def _dot_tn(a, b):
    return lax.dot_general(a, b, (((0,), (0,)), ((), ())), preferred_element_type=F32)


def _params(sem):
    return pltpu.CompilerParams(dimension_semantics=sem, vmem_limit_bytes=VMEM_LIMIT)


RET_FREQ_ORDER = np.concatenate([np.arange(0, 64, 2), np.arange(1, 64, 2)])


def _rope_table_kernel(pos_ref, inv_ref, rc_ref, rs_ref, mc_ref, ms_ref):
    pos = pos_ref[...].astype(F32)
    ang = inv_ref[...] * pos
    c = jnp.cos(ang)
    s = jnp.sin(ang)
    c32, s32 = c[:32], s[:32]
    rc_ref[...] = jnp.transpose(jnp.concatenate([c, c], axis=0))
    rs_ref[...] = jnp.transpose(jnp.concatenate([-s, s], axis=0))
    mc_ref[...] = jnp.transpose(jnp.concatenate([c32, c32, c32, c32], axis=0))
    ms_ref[...] = jnp.transpose(jnp.concatenate([-s32, -s32, s32, s32], axis=0))


def _rope_tables(positions, tt):
    t = positions.size
    pos = positions.reshape(1, t)
    inv64 = ROPE_BASE ** (-jnp.arange(64, dtype=F32) / 64)
    inv = inv64[RET_FREQ_ORDER].reshape(64, 1)
    tab = jax.ShapeDtypeStruct((t, LANES), F32)
    return pl.pallas_call(
        _rope_table_kernel,
        grid=(t // tt,),
        in_specs=[pl.BlockSpec((1, tt), lambda i: (0, i)),
                  pl.BlockSpec((64, 1), lambda i: (0, 0))],
        out_specs=[pl.BlockSpec((tt, LANES), lambda i: (i, 0))] * 4,
        out_shape=[tab] * 4,
        compiler_params=_params(("parallel",)),
        name="rope_tables",
    )(pos, inv)


def _rope(t, c, s):
    return t * c + pltpu.roll(t, 64, 1) * s


def _split3(x):
    a = x.astype(BF16)
    r = x - a.astype(F32)
    b = r.astype(BF16)
    c = (r - b.astype(F32)).astype(BF16)
    return a, b, c


def _level_ref(cum, m):
    w = cum.shape[1]
    if m >= 8:
        g = CHUNK // (2 * m)
        c3 = cum.reshape(g, 2 * m, w)
        return jnp.broadcast_to(c3[:, m - 1:m, :], (g, 2 * m, w)).reshape(CHUNK, w)
    c3 = cum.reshape(8, 8, w)

    def row(r):
        return jnp.broadcast_to(c3[:, r:r + 1, :], (8, 8, w))

    if m == 4:
        out = row(3)
    else:
        sub = lax.broadcasted_iota(jnp.int32, (8, 8, w), 1)
        out = jnp.where(sub < 4, row(1), row(5))
    return out.reshape(CHUNK, w)


def _upper_q_lower_k(q, k, m, upper):
    if m < 8:
        return jnp.where(upper[m], q, k)
    return jnp.concatenate([(q if blk % 2 else k)[blk * m:(blk + 1) * m] for blk in range(CHUNK // m)], axis=0)


def _hg_kernel(x_ref, nw_ref, w_ref, lb_ref, gw_ref, tril_ref, masks_ref, o_ref, h_ref, p_scr, st_scr, *, ts):
    @pl.when(pl.program_id(1) == 0)
    def _():
        st_scr[...] = jnp.zeros_like(st_scr)

    def project(r0):
        rows = slice(r0, r0 + ROW_GROUP)
        h = _rms(x_ref[rows, :], nw_ref[...]).astype(BF16)
        h_ref[rows, :] = h
        p_scr[rows, :] = _dot(h, w_ref[...])

    width = HEADS * HG_K
    pw = 2 * HG_K
    tril = tril_ref[...]
    rows = lax.broadcasted_iota(jnp.int32, (CHUNK, pw), 0)
    upper = {m: (rows % (2 * m)) >= m for m in LEVELS}
    zk = jnp.zeros((CHUNK, HG_K), BF16)

    def blockdiag(t):
        return jnp.concatenate([jnp.concatenate([t[:, :HG_K], zk], axis=1),
                                jnp.concatenate([zk, t[:, HG_K:]], axis=1)], axis=0)


    def gates(r0, pair):
        c0 = pair * pw
        lb = lb_ref[:, c0:c0 + pw]
        q = _silu(p_scr[pl.ds(r0, CHUNK), c0:c0 + pw])
        hf = p_scr[pl.ds(r0, CHUNK), width + c0:width + c0 + pw]
        e = jnp.exp2(jnp.abs(hf) * -LOG2E)
        u = jnp.exp2(jnp.minimum(hf * -LOG2E, EXP_CLIP * LOG2E))
        r = 1.0 / (1.0 + e)
        logf = (jnp.minimum(hf, 0.0) + jnp.log((1.0 + lb * u) * r)) * LOG2E
        k = (1.0 - lb) * (jnp.where(hf >= 0.0, e, 1.0) * r)
        cums = _dot(tril, jnp.concatenate(_split3(logf), axis=1))
        cum = cums[:, :pw] + cums[:, pw:2 * pw] + cums[:, 2 * pw:]
        return q, k, logf, cum

    def level_dots(q, k, logf, cum):
        c_last = cum[CHUNK - 1:CHUNK, :]
        qd = (q * jnp.exp2(cum)).astype(BF16)
        kd = (k * jnp.exp2(c_last - cum)).astype(BF16)
        dec = jnp.exp2(c_last)
        prods = [_dot_nt(q.astype(BF16), blockdiag(k.astype(BF16)))]
        for m in LEVELS:
            if m == 1:
                xl = jnp.where(upper[m], q * jnp.exp2(logf), k)
            else:
                xl = _upper_q_lower_k(q, k, m, upper) * jnp.exp2(-jnp.abs(cum - _level_ref(cum, m)))
            xl = xl.astype(BF16)
            prods.append(_dot_nt(xl, blockdiag(xl)))
        return qd, kd, dec, prods

    def finish(r0, pair, qd, kd, dec, prods):
        c0 = pair * pw
        v = p_scr[pl.ds(r0, CHUNK), 2 * width + c0:2 * width + c0 + pw].astype(BF16)
        gate = _silu(p_scr[pl.ds(r0, CHUNK), 3 * width + c0:3 * width + c0 + pw])
        att = masks_ref[0] * prods[0]
        for li in range(1, len(prods)):
            att = att + masks_ref[li] * prods[li]
        o = _dot(att.astype(BF16), blockdiag(v))
        outs = []
        for sub in range(2):
            hs = slice(sub * HG_K, (sub + 1) * HG_K)
            st = st_scr[2 * pair + sub]
            oh = o[:, sub * HG_V:(sub + 1) * HG_V] + _dot_nt(qd[:, hs], st.astype(BF16))
            st_scr[2 * pair + sub] = st * dec[:, hs] + _dot_tn(v[:, hs], kd[:, hs])
            outs.append(oh * lax.rsqrt(jnp.mean(oh * oh, axis=-1, keepdims=True) + EPS))
        y = jnp.concatenate(outs, axis=1) * gw_ref[:, c0:c0 + pw] * gate
        o_ref[pl.ds(r0, CHUNK), c0:c0 + pw] = y.astype(o_ref.dtype)

    for c0 in range(0, ts // CHUNK, CHUNKS_PER_ITER):
        if (c0 * CHUNK) % ROW_GROUP == 0:
            project(c0 * CHUNK)
        units = [((c0 + ci) * CHUNK, pair) for ci in range(CHUNKS_PER_ITER) for pair in range(HEADS // 2)]
        stage1 = [gates(r0, pair) for r0, pair in units]
        stage2 = [level_dots(*vals) for vals in stage1]
        for (r0, pair), vals in zip(units, stage2):
            finish(r0, pair, *vals)


def _level_masks():
    t = np.arange(CHUNK)[:, None]
    s = np.arange(CHUNK)[None, :]
    masks = [(t == s)]
    for m in LEVELS:
        masks.append((t // (2 * m) == s // (2 * m)) & (t % (2 * m) >= m) & (s % (2 * m) < m))
    return jnp.asarray(np.tile(np.stack(masks).astype(np.float32), (1, 1, 2)))


def _hg_mixer(x2, nw, w, lb, gw, b, s, ts):
    width = HEADS * HG_K
    nblk = s // ts
    tril = jnp.asarray(np.tril(np.ones((CHUNK, CHUNK), np.float32)), BF16)
    masks = _level_masks()
    const = lambda shape: pl.BlockSpec(shape, lambda i, j: (0,) * len(shape))
    return pl.pallas_call(
        functools.partial(_hg_kernel, ts=ts),
        grid=(b, nblk),
        in_specs=[pl.BlockSpec((ts, D_MODEL), lambda i, j: (i * nblk + j, 0)),
                  const((1, D_MODEL)), const((D_MODEL, 4 * width)), const((1, width)), const((1, width)),
                  const((CHUNK, CHUNK)), const(masks.shape)],
        out_specs=[pl.BlockSpec((ts, HEADS * HG_V), lambda i, j: (i * nblk + j, 0)),
                   pl.BlockSpec((ts, D_MODEL), lambda i, j: (i * nblk + j, 0))],
        out_shape=[jax.ShapeDtypeStruct((b * s, HEADS * HG_V), BF16), jax.ShapeDtypeStruct((b * s, D_MODEL), BF16)],
        scratch_shapes=[pltpu.VMEM((ts, 4 * width), F32), pltpu.VMEM((HEADS, HG_V, HG_K), F32)],
        compiler_params=_params(("parallel", "arbitrary")),
        name="hgrn2_mixer",
    )(x2, nw, w, lb, gw, tril, masks)


def _ret_kernel(h_ref, w_ref, gw_ref, rc_ref, rs_ref, intra_ref, qdec_ref, kdec_ref, cdec_ref,
                wc_ref, qn_ref, wuq_ref, kvn_ref, wukv_ref, mc_ref, ms_ref,
                o_ref, q_ref, k_ref, v_ref, p_scr, st_scr, kc_scr, ks_scr, *, ts):
    @pl.when(pl.program_id(1) == 0)
    def _():
        st_scr[...] = jnp.zeros_like(st_scr)

    def project(rows):
        p_scr[rows, :] = _dot(h_ref[rows, :], w_ref[...])
        kc_scr[rows, :] = rc_ref[rows, :] * (RET_K ** -0.5)
        ks_scr[rows, :] = rs_ref[rows, :] * (RET_K ** -0.5)

    qk_w = HEADS * RET_K
    v_w = HEADS * RET_V


    def scores(r0, hd):
        rows = pl.ds(r0, RET_CHUNK)
        q = _rope(p_scr[rows, hd * RET_K:(hd + 1) * RET_K], rc_ref[rows, :], rs_ref[rows, :])
        k = _rope(p_scr[rows, qk_w + hd * RET_K:qk_w + (hd + 1) * RET_K], kc_scr[rows, :], ks_scr[rows, :])
        return q, k, _dot_nt(q.astype(BF16), k.astype(BF16))

    def finish(r0, hd, q, k, qk):
        rows = pl.ds(r0, RET_CHUNK)
        v = p_scr[rows, 2 * qk_w + hd * RET_V:2 * qk_w + (hd + 1) * RET_V].astype(BF16)
        gate = _silu(p_scr[rows, 2 * qk_w + v_w + hd * RET_V:2 * qk_w + v_w + (hd + 1) * RET_V])
        att = (qk * intra_ref[hd]).astype(BF16)
        st = st_scr[hd]
        o = _dot(jnp.concatenate([att, (q * qdec_ref[hd]).astype(BF16)], axis=1),
                 jnp.concatenate([v, st.astype(BF16)], axis=0))
        st_scr[hd] = st * cdec_ref[hd] + _dot_tn((k * kdec_ref[hd]).astype(BF16), v)
        o = o * lax.rsqrt(jnp.mean(o * o, axis=-1, keepdims=True) + EPS)
        y = o * gw_ref[:, hd * RET_V:(hd + 1) * RET_V] * gate
        o_ref[rows, hd * RET_V:(hd + 1) * RET_V] = y.astype(o_ref.dtype)

    for g0 in range(0, ts, ROW_GROUP):
        rows = slice(g0, g0 + ROW_GROUP)
        project(rows)
        units = [(g0 + ci * RET_CHUNK, hd) for ci in range(ROW_GROUP // RET_CHUNK) for hd in range(HEADS)]
        stage1 = [scores(r0, hd) for r0, hd in units]
        for (r0, hd), vals in zip(units, stage1):
            finish(r0, hd, *vals)
        _mla_project(h_ref[rows, :], wc_ref, qn_ref, wuq_ref, kvn_ref, wukv_ref, mc_ref[rows, :], ms_ref[rows, :],
                     q_ref, k_ref, v_ref, rows)


def _ret_tables():
    log_gamma = jnp.log1p(-jnp.exp2(-5.0 - jnp.arange(HEADS, dtype=F32)))
    idx = jnp.arange(RET_CHUNK, dtype=F32)
    causal = jnp.tril(jnp.ones((RET_CHUNK, RET_CHUNK), dtype=bool))[None]
    lg = log_gamma[:, None, None]
    dist = jnp.where(causal, (idx[:, None] - idx[None, :])[None], 0.0)
    intra = jnp.where(causal, jnp.exp(dist * lg), 0.0)
    q_decay = jnp.exp((idx + 1.0)[None, :] * log_gamma[:, None])[..., None]
    k_decay = jnp.exp((RET_CHUNK - 1.0 - idx)[None, :] * log_gamma[:, None])[..., None]
    chunk_decay = jnp.exp(RET_CHUNK * log_gamma)[:, None, None]
    return (intra,
            jnp.broadcast_to(q_decay, (HEADS, RET_CHUNK, RET_K)),
            jnp.broadcast_to(k_decay, (HEADS, RET_CHUNK, RET_K)),
            jnp.broadcast_to(chunk_decay, (HEADS, 1, RET_V)))


def _ret_mixer(h, w, gw, rc, rs, wc, qn, wuq, kvn, wukv, mc, ms, b, s, ts):
    nblk = s // ts
    t = b * s
    intra, qdec, kdec, cdec = _ret_tables()
    n_proj = 2 * HEADS * RET_K + 2 * HEADS * RET_V
    const = lambda a: pl.BlockSpec(a.shape, lambda i, j: (0,) * a.ndim)
    rows = lambda width: pl.BlockSpec((ts, width), lambda i, j: (i * nblk + j, 0))
    out_widths = (HEADS * RET_V, HEADS * MLA_QK, HEADS * MLA_QK, HEADS * MLA_V)
    return pl.pallas_call(
        functools.partial(_ret_kernel, ts=ts),
        grid=(b, nblk),
        in_specs=[rows(D_MODEL), const(w), const(gw), rows(LANES), rows(LANES),
                  const(intra), const(qdec), const(kdec), const(cdec),
                  const(wc), const(qn), const(wuq), const(kvn), const(wukv), rows(LANES), rows(LANES)],
        out_specs=[rows(wd) for wd in out_widths],
        out_shape=[jax.ShapeDtypeStruct((t, wd), BF16) for wd in out_widths],
        scratch_shapes=[pltpu.VMEM((ts, n_proj), F32), pltpu.VMEM((HEADS, RET_K, RET_V), F32),
                        pltpu.VMEM((ts, LANES), F32), pltpu.VMEM((ts, LANES), F32)],
        compiler_params=_params(("parallel", "arbitrary")),
        name="retention_mla_mixer",
    )(h, w, gw, rc, rs, intra, qdec, kdec, cdec, wc, qn, wuq, kvn, wukv, mc, ms)


def _mla_project(h, wc_ref, qn_ref, wuq_ref, kvn_ref, wukv_ref, cos, sin, q_ref, k_ref, v_ref, rows):
    c = _dot(h, wc_ref[...])
    cq = _rms(c[:, 0:MLA_Q_RANK], qn_ref[...]).astype(BF16)
    qm = _dot(cq, wuq_ref[...]) * ((MLA_NOPE + MLA_ROPE) ** -0.5 * LOG2E)
    ckv = _rms(c[:, MLA_Q_RANK:MLA_Q_RANK + MLA_KV_RANK], kvn_ref[...]).astype(BF16)
    kv = _dot(ckv, wukv_ref[...])
    k_pe = _rope(c[:, MLA_Q_RANK + MLA_KV_RANK:], cos, sin).astype(BF16)
    lane = lax.broadcasted_iota(jnp.int32, k_pe.shape, 1)
    first = (lane % 64) < 32
    nope_w = HEADS * MLA_NOPE
    for pair in range(HEADS // 2):
        q_pe = _rope(qm[:, nope_w + pair * LANES:nope_w + (pair + 1) * LANES], cos, sin)
        for sub in range(2):
            hd = 2 * pair + sub
            keep = first if sub == 0 else jnp.logical_not(first)
            q_ref[rows, hd * MLA_QK:hd * MLA_QK + MLA_NOPE] = qm[:, hd * MLA_NOPE:(hd + 1) * MLA_NOPE].astype(BF16)
            q_ref[rows, hd * MLA_QK + MLA_NOPE:(hd + 1) * MLA_QK] = jnp.where(keep, q_pe, 0.0).astype(BF16)
    for hd in range(HEADS):
        k_ref[rows, hd * MLA_QK:hd * MLA_QK + MLA_NOPE] = kv[:, hd * MLA_NOPE:(hd + 1) * MLA_NOPE].astype(BF16)
        k_ref[rows, hd * MLA_QK + MLA_NOPE:(hd + 1) * MLA_QK] = k_pe
    v_ref[rows, :] = kv[:, nope_w:].astype(BF16)


def _mla_attn_kernel(q_ref, k_ref, v_ref, o_ref, acc_scr, *, tq, nq):
    key = lax.broadcasted_iota(jnp.int32, (tq, tq), 0)
    qry = lax.broadcasted_iota(jnp.int32, (tq, tq), 1)
    causal = key <= qry

    def visit(qi, kj, carry):
        ms, ls = carry
        q0, k0 = qi * tq, kj * tq
        new_m, new_l = [], []
        score = lambda hd: _dot_nt(k_ref[k0:k0 + tq, hd * MLA_QK:(hd + 1) * MLA_QK],
                                   q_ref[q0:q0 + tq, hd * MLA_QK:(hd + 1) * MLA_QK])
        scores = {0: score(0), 1: score(1)}
        for hd in range(HEADS):
            if hd + 2 < HEADS:
                scores[hd + 2] = score(hd + 2)
            s_t = scores.pop(hd)
            if kj == qi:
                s_t = jnp.where(causal, s_t, MASK_NEG)
            pv_in = v_ref[k0:k0 + tq, hd * MLA_V:(hd + 1) * MLA_V]
            if kj == 0:
                m_new = jnp.max(s_t, axis=0, keepdims=True)
                p = jnp.exp2(s_t - m_new)
                new_l.append(jnp.sum(p, axis=0, keepdims=True))
                acc_scr[hd] = _dot_tn(pv_in, p.astype(BF16))
            else:
                m_new = jnp.maximum(ms[hd], jnp.max(s_t, axis=0, keepdims=True))
                alpha = jnp.exp2(ms[hd] - m_new)
                p = jnp.exp2(s_t - m_new)
                new_l.append(ls[hd] * alpha + jnp.sum(p, axis=0, keepdims=True))
                acc_scr[hd] = acc_scr[hd] * alpha + _dot_tn(pv_in, p.astype(BF16))
            new_m.append(m_new)
        return tuple(new_m), tuple(new_l)

    for qi in range(nq):
        carry = (None, None)
        for kj in range(qi + 1):
            carry = visit(qi, kj, carry)
        for hd in range(HEADS):
            o_ref[qi * tq:(qi + 1) * tq, hd * MLA_V:(hd + 1) * MLA_V] = jnp.transpose(
                acc_scr[hd] / carry[1][hd]).astype(o_ref.dtype)


def _mla_attn(q, k, v, b, s, tq):
    nq = s // tq
    return pl.pallas_call(
        functools.partial(_mla_attn_kernel, tq=tq, nq=nq),
        grid=(b,),
        in_specs=[pl.BlockSpec((s, HEADS * MLA_QK), lambda i: (i, 0)),
                  pl.BlockSpec((s, HEADS * MLA_QK), lambda i: (i, 0)),
                  pl.BlockSpec((s, HEADS * MLA_V), lambda i: (i, 0))],
        out_specs=pl.BlockSpec((s, HEADS * MLA_V), lambda i: (i, 0)),
        out_shape=jax.ShapeDtypeStruct((b * s, HEADS * MLA_V), BF16),
        scratch_shapes=[pltpu.VMEM((HEADS, MLA_V, tq), F32)],
        compiler_params=_params(("parallel",)),
        name="mla_attention",
    )(q, k, v)


def _merge_kernel(x_ref, h_ref, wg_ref, ya_ref, yb_ref, yc_ref, wa_ref, wb_ref, wc_ref, wo_ref, o_ref):
    for r0 in range(0, x_ref.shape[0], ROW_GROUP):
        rows = slice(r0, r0 + ROW_GROUP)
        h = h_ref[rows, :]
        gate = lambda c0: _sigmoid(_dot(h, wg_ref[:, c0:c0 + D_MODEL]).astype(BF16)).astype(F32)
        merged = (gate(0) * _dot(ya_ref[rows, :], wa_ref[...])
                  + gate(D_MODEL) * _dot(yb_ref[rows, :], wb_ref[...])
                  + gate(2 * D_MODEL) * _dot(yc_ref[rows, :], wc_ref[...]))
        o_ref[rows, :] = x_ref[rows, :] + _dot(merged.astype(BF16), wo_ref[...])


def _merge(x2, h, wg, ya, yb, yc, wa, wb, wc, wo, tm):
    t = x2.shape[0]
    const = lambda a: pl.BlockSpec(a.shape, lambda i: (0,) * a.ndim)
    rows = lambda a: pl.BlockSpec((tm, a.shape[1]), lambda i: (i, 0))
    return pl.pallas_call(
        _merge_kernel,
        grid=(t // tm,),
        in_specs=[rows(x2), rows(h), const(wg), rows(ya), rows(yb), rows(yc),
                  const(wa), const(wb), const(wc), const(wo)],
        out_specs=rows(x2),
        out_shape=jax.ShapeDtypeStruct(x2.shape, F32),
        compiler_params=_params(("parallel",)),
        name="gated_merge",
    )(x2, h, wg, ya, yb, yc, wa, wb, wc, wo)


def _ffn_kernel(x_ref, nw_ref, wi_ref, wo_ref, fw_ref, o_ref, *, final_norm):
    for r0 in range(0, x_ref.shape[0], ROW_GROUP):
        rows = slice(r0, r0 + ROW_GROUP)
        x = x_ref[rows, :]
        h = _rms(x, nw_ref[...]).astype(BF16)
        g = _dot(h, wi_ref[:, 0:D_FF])
        u = _dot(h, wi_ref[:, D_FF:2 * D_FF])
        y = x + _dot((g * _sigmoid(g.astype(BF16)).astype(F32) * u).astype(BF16), wo_ref[...])
        if final_norm:
            y = _rms(y, fw_ref[...])
        o_ref[rows, :] = y


def _ffn(x2, nw, wi, wo, fw, tm, final_norm):
    t = x2.shape[0]
    const = lambda a: pl.BlockSpec(a.shape, lambda i: (0,) * a.ndim)
    rows = lambda a: pl.BlockSpec((tm, a.shape[1]), lambda i: (i, 0))
    return pl.pallas_call(
        functools.partial(_ffn_kernel, final_norm=final_norm),
        grid=(t // tm,),
        in_specs=[rows(x2), const(nw), const(wi), const(wo), const(fw)],
        out_specs=rows(x2),
        out_shape=jax.ShapeDtypeStruct(x2.shape, F32),
        compiler_params=_params(("parallel",)),
        name="swiglu_ffn",
    )(x2, nw, wi, wo, fw)


def _in_proj_slices(w_in_l):
    widths = (512, 512, 512, 512, 512, 512, 1024, 1024, MLA_Q_RANK, MLA_KV_RANK, MLA_ROPE, 3 * D_MODEL)
    offs = np.concatenate([[0], np.cumsum(widths)])
    cols = [w_in_l[:, int(offs[i]):int(offs[i + 1])] for i in range(len(widths))]
    w_hg = jnp.concatenate(cols[0:4], axis=1)

    def freq_order(w):
        return w.reshape(D_MODEL, HEADS, 2, RET_K // 2)[..., RET_FREQ_ORDER].reshape(D_MODEL, HEADS * RET_K)

    w_ret = jnp.concatenate([freq_order(cols[4]), freq_order(cols[5]), cols[6], cols[7]], axis=1)
    kr = cols[10]
    kr1, kr2 = kr[:, :32], kr[:, 32:]
    w_c = jnp.concatenate([cols[8], cols[9], kr1, kr1, kr2, kr2], axis=1)
    return w_hg.astype(BF16), w_ret.astype(BF16), w_c.astype(BF16), cols[11].astype(BF16)


def _uq_layout(w_uq_l):
    per = MLA_NOPE + MLA_ROPE
    heads = [w_uq_l[:, hd * per:(hd + 1) * per] for hd in range(HEADS)]
    nope = [hh[:, :MLA_NOPE] for hh in heads]
    pe = [hh[:, MLA_NOPE:] for hh in heads]
    pairs = []
    for p in range(HEADS // 2):
        a, b = pe[2 * p], pe[2 * p + 1]
        pairs += [a[:, :32], b[:, :32], a[:, 32:], b[:, 32:]]
    return jnp.concatenate(nope + pairs, axis=1).astype(BF16)


def _ukv_layout(w_ukv_l):
    per = MLA_NOPE + MLA_V
    heads = [w_ukv_l[:, hd * per:(hd + 1) * per] for hd in range(HEADS)]
    return jnp.concatenate([hh[:, :MLA_NOPE] for hh in heads] + [hh[:, MLA_NOPE:] for hh in heads],
                           axis=1).astype(BF16)


def _block(n, want):
    return want if n % want == 0 else n


def kernel(x, positions, norm_mix_w, w_in, hg_lower_bounds, hg_norm_w, ret_norm_w, mla_q_norm_w, mla_w_uq, mla_kv_norm_w, mla_w_ukv, w_br_a, w_br_b, w_br_c, w_out, norm_ffn_w, w_ffn_in, w_ffn_out, final_norm_w):
    b, s, d = x.shape
    depth = w_in.shape[0]
    assert d == D_MODEL and s % (CHUNKS_PER_ITER * CHUNK) == 0
    t = b * s
    ts = _block(s, 2 * ROW_GROUP)
    assert ts % ROW_GROUP == 0
    tm = _block(t, 2 * ROW_GROUP)
    tq = _block(s, 512)
    assert tm % ROW_GROUP == 0

    lb_p = jax.nn.softmax(hg_lower_bounds.astype(F32), axis=0)
    lb_all = jnp.cumsum(lb_p, axis=0) - lb_p[0]

    rc, rs, mc, ms = _rope_tables(positions, _block(t, 512))
    row = lambda a: a.reshape(1, -1).astype(F32)

    x2 = x.reshape(t, d)
    for l in range(depth):
        w_hg, w_ret, w_c, w_g = _in_proj_slices(w_in[l])
        nw = row(norm_mix_w[l])
        ya, h = _hg_mixer(x2, nw, w_hg, row(lb_all[l]), row(hg_norm_w[l]), b, s, ts)
        yb, q, k, v = _ret_mixer(h, w_ret, row(ret_norm_w[l]), rc, rs, w_c, row(mla_q_norm_w[l]),
                                 _uq_layout(mla_w_uq[l]), row(mla_kv_norm_w[l]), _ukv_layout(mla_w_ukv[l]),
                                 mc, ms, b, s, ts)
        yc = _mla_attn(q, k, v, b, s, tq)
        x2 = _merge(x2, h, w_g, ya, yb, yc, w_br_a[l].astype(BF16), w_br_b[l].astype(BF16),
                    w_br_c[l].astype(BF16), w_out[l].astype(BF16), tm)
        x2 = _ffn(x2, row(norm_ffn_w[l]), w_ffn_in[l].astype(BF16), w_ffn_out[l].astype(BF16),
                  row(final_norm_w), tm, final_norm=(l == depth - 1))
    return x2.reshape(b, s, d)
```

```python
import functools

import numpy as np
import jax
import jax.numpy as jnp
from jax import lax
from jax.experimental import pallas as pl
from jax.experimental.pallas import tpu as pltpu

F32 = jnp.float32
BF16 = jnp.bfloat16

D_MODEL = 1024
HEADS = 4
HG_K = 128
HG_V = 128
RET_K = 128
RET_V = 256
MLA_Q_RANK = 256
MLA_KV_RANK = 128
MLA_NOPE = 128
MLA_ROPE = 64
MLA_V = 128
MLA_QK = 256
D_FF = 2816
FF_SLABS = ((0, 1536), (1536, D_FF))
CHUNK = 64
CHUNKS_PER_ITER = 4
RET_CHUNK = 128
ROPE_BASE = 10000.0
EPS = 1e-6
EXP_CLIP = 60.0
MASK_NEG = -1e30
LOG2E = 1.4426950408889634

ROW_GROUP = 512
LANES = 128
VMEM_LIMIT = 56 * 1024 * 1024

LEVELS = (32, 16, 8, 4, 2, 1)


def _rms(x, w):
    return x * lax.rsqrt(jnp.mean(x * x, axis=-1, keepdims=True) + EPS) * w


def _sigmoid(x):
    return 0.5 + 0.5 * jnp.tanh(0.5 * x)


def _silu(x):
    return x * _sigmoid(x)


def _dot(a, b):
    return jnp.dot(a, b, preferred_element_type=F32)


def _dot_nt(a, b):
    return lax.dot_general(a, b, (((1,), (1,)), ((), ())), preferred_element_type=F32)


def _dot_tn(a, b):
    return lax.dot_general(a, b, (((0,), (0,)), ((), ())), preferred_element_type=F32)


def _params(sem):
    return pltpu.CompilerParams(dimension_semantics=sem, vmem_limit_bytes=VMEM_LIMIT)


RET_FREQ_ORDER = np.concatenate([np.arange(0, 64, 2), np.arange(1, 64, 2)])


def _rope_table_kernel(pos_ref, inv_ref, rc_ref, rs_ref, mc_ref, ms_ref):
    pos = pos_ref[...].astype(F32)
    ang = inv_ref[...] * pos
    c = jnp.cos(ang)
    s = jnp.sin(ang)
    c32, s32 = c[:32], s[:32]
    rc_ref[...] = jnp.transpose(jnp.concatenate([c, c], axis=0))
    rs_ref[...] = jnp.transpose(jnp.concatenate([-s, s], axis=0))
    mc_ref[...] = jnp.transpose(jnp.concatenate([c32, c32, c32, c32], axis=0))
    ms_ref[...] = jnp.transpose(jnp.concatenate([-s32, -s32, s32, s32], axis=0))


def _rope_tables(positions, tt):
    t = positions.size
    pos = positions.reshape(1, t)
    inv64 = ROPE_BASE ** (-jnp.arange(64, dtype=F32) / 64)
    inv = inv64[RET_FREQ_ORDER].reshape(64, 1)
    tab = jax.ShapeDtypeStruct((t, LANES), F32)
    return pl.pallas_call(
        _rope_table_kernel,
        grid=(t // tt,),
        in_specs=[pl.BlockSpec((1, tt), lambda i: (0, i)),
                  pl.BlockSpec((64, 1), lambda i: (0, 0))],
        out_specs=[pl.BlockSpec((tt, LANES), lambda i: (i, 0))] * 4,
        out_shape=[tab] * 4,
        compiler_params=_params(("parallel",)),
        name="rope_tables",
    )(pos, inv)


def _rope(t, c, s):
    return t * c + pltpu.roll(t, 64, 1) * s


def _split3(x):
    a = x.astype(BF16)
    r = x - a.astype(F32)
    b = r.astype(BF16)
    c = (r - b.astype(F32)).astype(BF16)
    return a, b, c


def _level_ref(cum, m):
    w = cum.shape[1]
    if m >= 8:
        g = CHUNK // (2 * m)
        c3 = cum.reshape(g, 2 * m, w)
        return jnp.broadcast_to(c3[:, m - 1:m, :], (g, 2 * m, w)).reshape(CHUNK, w)
    c3 = cum.reshape(8, 8, w)

    def row(r):
        return jnp.broadcast_to(c3[:, r:r + 1, :], (8, 8, w))

    if m == 4:
        out = row(3)
    else:
        sub = lax.broadcasted_iota(jnp.int32, (8, 8, w), 1)
        out = jnp.where(sub < 4, row(1), row(5))
    return out.reshape(CHUNK, w)


def _upper_q_lower_k(q, k, m, upper):
    if m < 8:
        return jnp.where(upper[m], q, k)
    return jnp.concatenate([(q if blk % 2 else k)[blk * m:(blk + 1) * m] for blk in range(CHUNK // m)], axis=0)


def _hg_kernel(x_ref, nw_ref, w_ref, lb_ref, gw_ref, tril_ref, masks_ref, o_ref, h_ref, p_scr, st_scr, *, ts):
    @pl.when(pl.program_id(1) == 0)
    def _():
        st_scr[...] = jnp.zeros_like(st_scr)

    def project(r0):
        rows = slice(r0, r0 + ROW_GROUP)
        h = _rms(x_ref[rows, :], nw_ref[...]).astype(BF16)
        h_ref[rows, :] = h
        p_scr[rows, :] = _dot(h, w_ref[...])

    width = HEADS * HG_K
    pw = 2 * HG_K
    tril = tril_ref[...]
    rows = lax.broadcasted_iota(jnp.int32, (CHUNK, pw), 0)
    upper = {m: (rows % (2 * m)) >= m for m in LEVELS}
    zk = jnp.zeros((CHUNK, HG_K), BF16)

    def blockdiag(t):
        return jnp.concatenate([jnp.concatenate([t[:, :HG_K], zk], axis=1),
                                jnp.concatenate([zk, t[:, HG_K:]], axis=1)], axis=0)


    def gates(r0, pair):
        c0 = pair * pw
        lb = lb_ref[:, c0:c0 + pw]
        q = _silu(p_scr[pl.ds(r0, CHUNK), c0:c0 + pw])
        hf = p_scr[pl.ds(r0, CHUNK), width + c0:width + c0 + pw]
        e = jnp.exp2(jnp.abs(hf) * -LOG2E)
        u = jnp.exp2(jnp.minimum(hf * -LOG2E, EXP_CLIP * LOG2E))
        r = 1.0 / (1.0 + e)
        logf = (jnp.minimum(hf, 0.0) + jnp.log((1.0 + lb * u) * r)) * LOG2E
        k = (1.0 - lb) * (jnp.where(hf >= 0.0, e, 1.0) * r)
        cums = _dot(tril, jnp.concatenate(_split3(logf), axis=1))
        cum = cums[:, :pw] + cums[:, pw:2 * pw] + cums[:, 2 * pw:]
        return q, k, logf, cum

    def level_dots(q, k, logf, cum):
        c_last = cum[CHUNK - 1:CHUNK, :]
        qd = (q * jnp.exp2(cum)).astype(BF16)
        kd = (k * jnp.exp2(c_last - cum)).astype(BF16)
        dec = jnp.exp2(c_last)
        prods = [_dot_nt(q.astype(BF16), blockdiag(k.astype(BF16)))]
        for m in LEVELS:
            if m == 1:
                xl = jnp.where(upper[m], q * jnp.exp2(logf), k)
            else:
                xl = _upper_q_lower_k(q, k, m, upper) * jnp.exp2(-jnp.abs(cum - _level_ref(cum, m)))
            xl = xl.astype(BF16)
            prods.append(_dot_nt(xl, blockdiag(xl)))
        return qd, kd, dec, prods

    def finish(r0, pair, qd, kd, dec, prods):
        c0 = pair * pw
        v = p_scr[pl.ds(r0, CHUNK), 2 * width + c0:2 * width + c0 + pw].astype(BF16)
        gate = _silu(p_scr[pl.ds(r0, CHUNK), 3 * width + c0:3 * width + c0 + pw])
        att = masks_ref[0] * prods[0]
        for li in range(1, len(prods)):
            att = att + masks_ref[li] * prods[li]
        o = _dot(att.astype(BF16), blockdiag(v))
        outs = []
        for sub in range(2):
            hs = slice(sub * HG_K, (sub + 1) * HG_K)
            st = st_scr[2 * pair + sub]
            oh = o[:, sub * HG_V:(sub + 1) * HG_V] + _dot_nt(qd[:, hs], st.astype(BF16))
            st_scr[2 * pair + sub] = st * dec[:, hs] + _dot_tn(v[:, hs], kd[:, hs])
            outs.append(oh * lax.rsqrt(jnp.mean(oh * oh, axis=-1, keepdims=True) + EPS))
        y = jnp.concatenate(outs, axis=1) * gw_ref[:, c0:c0 + pw] * gate
        o_ref[pl.ds(r0, CHUNK), c0:c0 + pw] = y.astype(o_ref.dtype)

    for c0 in range(0, ts // CHUNK, CHUNKS_PER_ITER):
        if (c0 * CHUNK) % ROW_GROUP == 0:
            project(c0 * CHUNK)
        units = [((c0 + ci) * CHUNK, pair) for ci in range(CHUNKS_PER_ITER) for pair in range(HEADS // 2)]
        stage1 = [gates(r0, pair) for r0, pair in units]
        stage2 = [level_dots(*vals) for vals in stage1]
        for (r0, pair), vals in zip(units, stage2):
            finish(r0, pair, *vals)


def _level_masks():
    t = np.arange(CHUNK)[:, None]
    s = np.arange(CHUNK)[None, :]
    masks = [(t == s)]
    for m in LEVELS:
        masks.append((t // (2 * m) == s // (2 * m)) & (t % (2 * m) >= m) & (s % (2 * m) < m))
    return jnp.asarray(np.tile(np.stack(masks).astype(np.float32), (1, 1, 2)))


def _hg_mixer(x2, nw, w, lb, gw, b, s, ts):
    width = HEADS * HG_K
    nblk = s // ts
    tril = jnp.asarray(np.tril(np.ones((CHUNK, CHUNK), np.float32)), BF16)
    masks = _level_masks()
    const = lambda shape: pl.BlockSpec(shape, lambda i, j: (0,) * len(shape))
    return pl.pallas_call(
        functools.partial(_hg_kernel, ts=ts),
        grid=(b, nblk),
        in_specs=[pl.BlockSpec((ts, D_MODEL), lambda i, j: (i * nblk + j, 0)),
                  const((1, D_MODEL)), const((D_MODEL, 4 * width)), const((1, width)), const((1, width)),
                  const((CHUNK, CHUNK)), const(masks.shape)],
        out_specs=[pl.BlockSpec((ts, HEADS * HG_V), lambda i, j: (i * nblk + j, 0)),
                   pl.BlockSpec((ts, D_MODEL), lambda i, j: (i * nblk + j, 0))],
        out_shape=[jax.ShapeDtypeStruct((b * s, HEADS * HG_V), BF16), jax.ShapeDtypeStruct((b * s, D_MODEL), BF16)],
        scratch_shapes=[pltpu.VMEM((ts, 4 * width), F32), pltpu.VMEM((HEADS, HG_V, HG_K), F32)],
        compiler_params=_params(("parallel", "arbitrary")),
        name="hgrn2_mixer",
    )(x2, nw, w, lb, gw, tril, masks)


def _ret_kernel(h_ref, w_ref, gw_ref, rc_ref, rs_ref, intra_ref, qdec_ref, kdec_ref, cdec_ref,
                wc_ref, qn_ref, wuq_ref, kvn_ref, wukv_ref, mc_ref, ms_ref,
                o_ref, q_ref, k_ref, v_ref, p_scr, st_scr, kc_scr, ks_scr, *, ts):
    @pl.when(pl.program_id(1) == 0)
    def _():
        st_scr[...] = jnp.zeros_like(st_scr)

    def project(rows):
        p_scr[rows, :] = _dot(h_ref[rows, :], w_ref[...])
        kc_scr[rows, :] = rc_ref[rows, :] * (RET_K ** -0.5)
        ks_scr[rows, :] = rs_ref[rows, :] * (RET_K ** -0.5)

    qk_w = HEADS * RET_K
    v_w = HEADS * RET_V


    def scores(r0, hd):
        rows = pl.ds(r0, RET_CHUNK)
        q = _rope(p_scr[rows, hd * RET_K:(hd + 1) * RET_K], rc_ref[rows, :], rs_ref[rows, :])
        k = _rope(p_scr[rows, qk_w + hd * RET_K:qk_w + (hd + 1) * RET_K], kc_scr[rows, :], ks_scr[rows, :])
        return q, k, _dot_nt(q.astype(BF16), k.astype(BF16))

    def finish(r0, hd, q, k, qk):
        rows = pl.ds(r0, RET_CHUNK)
        v = p_scr[rows, 2 * qk_w + hd * RET_V:2 * qk_w + (hd + 1) * RET_V].astype(BF16)
        gate = _silu(p_scr[rows, 2 * qk_w + v_w + hd * RET_V:2 * qk_w + v_w + (hd + 1) * RET_V])
        att = (qk * intra_ref[hd]).astype(BF16)
        st = st_scr[hd]
        o = _dot(jnp.concatenate([att, (q * qdec_ref[hd]).astype(BF16)], axis=1),
                 jnp.concatenate([v, st.astype(BF16)], axis=0))
        st_scr[hd] = st * cdec_ref[hd] + _dot_tn((k * kdec_ref[hd]).astype(BF16), v)
        o = o * lax.rsqrt(jnp.mean(o * o, axis=-1, keepdims=True) + EPS)
        y = o * gw_ref[:, hd * RET_V:(hd + 1) * RET_V] * gate
        o_ref[rows, hd * RET_V:(hd + 1) * RET_V] = y.astype(o_ref.dtype)

    for g0 in range(0, ts, ROW_GROUP):
        rows = slice(g0, g0 + ROW_GROUP)
        project(rows)
        units = [(g0 + ci * RET_CHUNK, hd) for ci in range(ROW_GROUP // RET_CHUNK) for hd in range(HEADS)]
        stage1 = [scores(r0, hd) for r0, hd in units]
        for (r0, hd), vals in zip(units, stage1):
            finish(r0, hd, *vals)
        _mla_project(h_ref[rows, :], wc_ref, qn_ref, wuq_ref, kvn_ref, wukv_ref, mc_ref[rows, :], ms_ref[rows, :],
                     q_ref, k_ref, v_ref, rows)


def _ret_tables():
    log_gamma = jnp.log1p(-jnp.exp2(-5.0 - jnp.arange(HEADS, dtype=F32)))
    idx = jnp.arange(RET_CHUNK, dtype=F32)
    causal = jnp.tril(jnp.ones((RET_CHUNK, RET_CHUNK), dtype=bool))[None]
    lg = log_gamma[:, None, None]
    dist = jnp.where(causal, (idx[:, None] - idx[None, :])[None], 0.0)
    intra = jnp.where(causal, jnp.exp(dist * lg), 0.0)
    q_decay = jnp.exp((idx + 1.0)[None, :] * log_gamma[:, None])[..., None]
    k_decay = jnp.exp((RET_CHUNK - 1.0 - idx)[None, :] * log_gamma[:, None])[..., None]
    chunk_decay = jnp.exp(RET_CHUNK * log_gamma)[:, None, None]
    return (intra,
            jnp.broadcast_to(q_decay, (HEADS, RET_CHUNK, RET_K)),
            jnp.broadcast_to(k_decay, (HEADS, RET_CHUNK, RET_K)),
            jnp.broadcast_to(chunk_decay, (HEADS, 1, RET_V)))


def _ret_mixer(h, w, gw, rc, rs, wc, qn, wuq, kvn, wukv, mc, ms, b, s, ts):
    nblk = s // ts
    t = b * s
    intra, qdec, kdec, cdec = _ret_tables()
    n_proj = 2 * HEADS * RET_K + 2 * HEADS * RET_V
    const = lambda a: pl.BlockSpec(a.shape, lambda i, j: (0,) * a.ndim)
    rows = lambda width: pl.BlockSpec((ts, width), lambda i, j: (i * nblk + j, 0))
    out_widths = (HEADS * RET_V, HEADS * MLA_QK, HEADS * MLA_QK, HEADS * MLA_V)
    return pl.pallas_call(
        functools.partial(_ret_kernel, ts=ts),
        grid=(b, nblk),
        in_specs=[rows(D_MODEL), const(w), const(gw), rows(LANES), rows(LANES),
                  const(intra), const(qdec), const(kdec), const(cdec),
                  const(wc), const(qn), const(wuq), const(kvn), const(wukv), rows(LANES), rows(LANES)],
        out_specs=[rows(wd) for wd in out_widths],
        out_shape=[jax.ShapeDtypeStruct((t, wd), BF16) for wd in out_widths],
        scratch_shapes=[pltpu.VMEM((ts, n_proj), F32), pltpu.VMEM((HEADS, RET_K, RET_V), F32),
                        pltpu.VMEM((ts, LANES), F32), pltpu.VMEM((ts, LANES), F32)],
        compiler_params=_params(("parallel", "arbitrary")),
        name="retention_mla_mixer",
    )(h, w, gw, rc, rs, intra, qdec, kdec, cdec, wc, qn, wuq, kvn, wukv, mc, ms)


def _mla_project(h, wc_ref, qn_ref, wuq_ref, kvn_ref, wukv_ref, cos, sin, q_ref, k_ref, v_ref, rows):
    c = _dot(h, wc_ref[...])
    cq = _rms(c[:, 0:MLA_Q_RANK], qn_ref[...]).astype(BF16)
    qm = _dot(cq, wuq_ref[...]) * ((MLA_NOPE + MLA_ROPE) ** -0.5 * LOG2E)
    ckv = _rms(c[:, MLA_Q_RANK:MLA_Q_RANK + MLA_KV_RANK], kvn_ref[...]).astype(BF16)
    kv = _dot(ckv, wukv_ref[...])
    k_pe = _rope(c[:, MLA_Q_RANK + MLA_KV_RANK:], cos, sin).astype(BF16)
    lane = lax.broadcasted_iota(jnp.int32, k_pe.shape, 1)
    first = (lane % 64) < 32
    nope_w = HEADS * MLA_NOPE
    for pair in range(HEADS // 2):
        q_pe = _rope(qm[:, nope_w + pair * LANES:nope_w + (pair + 1) * LANES], cos, sin)
        for sub in range(2):
            hd = 2 * pair + sub
            keep = first if sub == 0 else jnp.logical_not(first)
            q_ref[rows, hd * MLA_QK:hd * MLA_QK + MLA_NOPE] = qm[:, hd * MLA_NOPE:(hd + 1) * MLA_NOPE].astype(BF16)
            q_ref[rows, hd * MLA_QK + MLA_NOPE:(hd + 1) * MLA_QK] = jnp.where(keep, q_pe, 0.0).astype(BF16)
    for hd in range(HEADS):
        k_ref[rows, hd * MLA_QK:hd * MLA_QK + MLA_NOPE] = kv[:, hd * MLA_NOPE:(hd + 1) * MLA_NOPE].astype(BF16)
        k_ref[rows, hd * MLA_QK + MLA_NOPE:(hd + 1) * MLA_QK] = k_pe
    v_ref[rows, :] = kv[:, nope_w:].astype(BF16)


def _mla_attn_kernel(q_ref, k_ref, v_ref, o_ref, acc_scr, *, tq, nq):
    key = lax.broadcasted_iota(jnp.int32, (tq, tq), 0)
    qry = lax.broadcasted_iota(jnp.int32, (tq, tq), 1)
    causal = key <= qry

    def visit(qi, kj, carry):
        ms, ls = carry
        q0, k0 = qi * tq, kj * tq
        new_m, new_l = [], []
        score = lambda hd: _dot_nt(k_ref[k0:k0 + tq, hd * MLA_QK:(hd + 1) * MLA_QK],
                                   q_ref[q0:q0 + tq, hd * MLA_QK:(hd + 1) * MLA_QK])
        scores = {0: score(0), 1: score(1)}
        for hd in range(HEADS):
            if hd + 2 < HEADS:
                scores[hd + 2] = score(hd + 2)
            s_t = scores.pop(hd)
            if kj == qi:
                s_t = jnp.where(causal, s_t, MASK_NEG)
            pv_in = v_ref[k0:k0 + tq, hd * MLA_V:(hd + 1) * MLA_V]
            if kj == 0:
                m_new = jnp.max(s_t, axis=0, keepdims=True)
                p = jnp.exp2(s_t - m_new)
                new_l.append(jnp.sum(p, axis=0, keepdims=True))
                acc_scr[hd] = _dot_tn(pv_in, p.astype(BF16))
            else:
                m_new = jnp.maximum(ms[hd], jnp.max(s_t, axis=0, keepdims=True))
                alpha = jnp.exp2(ms[hd] - m_new)
                p = jnp.exp2(s_t - m_new)
                new_l.append(ls[hd] * alpha + jnp.sum(p, axis=0, keepdims=True))
                acc_scr[hd] = acc_scr[hd] * alpha + _dot_tn(pv_in, p.astype(BF16))
            new_m.append(m_new)
        return tuple(new_m), tuple(new_l)

    for qi in range(nq):
        carry = (None, None)
        for kj in range(qi + 1):
            carry = visit(qi, kj, carry)
        for hd in range(HEADS):
            o_ref[qi * tq:(qi + 1) * tq, hd * MLA_V:(hd + 1) * MLA_V] = jnp.transpose(
                acc_scr[hd] / carry[1][hd]).astype(o_ref.dtype)


def _mla_attn(q, k, v, b, s, tq):
    nq = s // tq
    return pl.pallas_call(
        functools.partial(_mla_attn_kernel, tq=tq, nq=nq),
        grid=(b,),
        in_specs=[pl.BlockSpec((s, HEADS * MLA_QK), lambda i: (i, 0)),
                  pl.BlockSpec((s, HEADS * MLA_QK), lambda i: (i, 0)),
                  pl.BlockSpec((s, HEADS * MLA_V), lambda i: (i, 0))],
        out_specs=pl.BlockSpec((s, HEADS * MLA_V), lambda i: (i, 0)),
        out_shape=jax.ShapeDtypeStruct((b * s, HEADS * MLA_V), BF16),
        scratch_shapes=[pltpu.VMEM((HEADS, MLA_V, tq), F32)],
        compiler_params=_params(("parallel",)),
        name="mla_attention",
    )(q, k, v)


def _merge_kernel(x_ref, h_ref, wg_ref, ya_ref, yb_ref, yc_ref, wa_ref, wb_ref, wc_ref, wo_ref, o_ref):
    for r0 in range(0, x_ref.shape[0], ROW_GROUP):
        rows = slice(r0, r0 + ROW_GROUP)
        h = h_ref[rows, :]
        merged = (_sigmoid(_dot(h, wg_ref[:, 0:D_MODEL])) * _dot(ya_ref[rows, :], wa_ref[...])
                  + _sigmoid(_dot(h, wg_ref[:, D_MODEL:2 * D_MODEL])) * _dot(yb_ref[rows, :], wb_ref[...])
                  + _sigmoid(_dot(h, wg_ref[:, 2 * D_MODEL:3 * D_MODEL])) * _dot(yc_ref[rows, :], wc_ref[...]))
        o_ref[rows, :] = x_ref[rows, :] + _dot(merged.astype(BF16), wo_ref[...])


def _merge(x2, h, wg, ya, yb, yc, wa, wb, wc, wo, tm):
    t = x2.shape[0]
    const = lambda a: pl.BlockSpec(a.shape, lambda i: (0,) * a.ndim)
    rows = lambda a: pl.BlockSpec((tm, a.shape[1]), lambda i: (i, 0))
    return pl.pallas_call(
        _merge_kernel,
        grid=(t // tm,),
        in_specs=[rows(x2), rows(h), const(wg), rows(ya), rows(yb), rows(yc),
                  const(wa), const(wb), const(wc), const(wo)],
        out_specs=rows(x2),
        out_shape=jax.ShapeDtypeStruct(x2.shape, F32),
        compiler_params=_params(("parallel",)),
        name="gated_merge",
    )(x2, h, wg, ya, yb, yc, wa, wb, wc, wo)


def _ffn_kernel(x_ref, nw_ref, wi_ref, wo_ref, fw_ref, o_ref, *, final_norm):
    for r0 in range(0, x_ref.shape[0], ROW_GROUP):
        rows = slice(r0, r0 + ROW_GROUP)
        x = x_ref[rows, :]
        h = _rms(x, nw_ref[...]).astype(BF16)
        y = x
        for f0, f1 in FF_SLABS:
            g = _dot(h, wi_ref[:, f0:f1])
            u = _dot(h, wi_ref[:, D_FF + f0:D_FF + f1])
            y = y + _dot((_silu(g) * u).astype(BF16), wo_ref[f0:f1, :])
        if final_norm:
            y = _rms(y, fw_ref[...])
        o_ref[rows, :] = y


def _ffn(x2, nw, wi, wo, fw, tm, final_norm):
    t = x2.shape[0]
    const = lambda a: pl.BlockSpec(a.shape, lambda i: (0,) * a.ndim)
    rows = lambda a: pl.BlockSpec((tm, a.shape[1]), lambda i: (i, 0))
    return pl.pallas_call(
        functools.partial(_ffn_kernel, final_norm=final_norm),
        grid=(t // tm,),
        in_specs=[rows(x2), const(nw), const(wi), const(wo), const(fw)],
        out_specs=rows(x2),
        out_shape=jax.ShapeDtypeStruct(x2.shape, F32),
        compiler_params=_params(("parallel",)),
        name="swiglu_ffn",
    )(x2, nw, wi, wo, fw)


def _in_proj_slices(w_in_l):
    widths = (512, 512, 512, 512, 512, 512, 1024, 1024, MLA_Q_RANK, MLA_KV_RANK, MLA_ROPE, 3 * D_MODEL)
    offs = np.concatenate([[0], np.cumsum(widths)])
    cols = [w_in_l[:, int(offs[i]):int(offs[i + 1])] for i in range(len(widths))]
    w_hg = jnp.concatenate(cols[0:4], axis=1)

    def freq_order(w):
        return w.reshape(D_MODEL, HEADS, 2, RET_K // 2)[..., RET_FREQ_ORDER].reshape(D_MODEL, HEADS * RET_K)

    w_ret = jnp.concatenate([freq_order(cols[4]), freq_order(cols[5]), cols[6], cols[7]], axis=1)
    kr = cols[10]
    kr1, kr2 = kr[:, :32], kr[:, 32:]
    w_c = jnp.concatenate([cols[8], cols[9], kr1, kr1, kr2, kr2], axis=1)
    return w_hg.astype(BF16), w_ret.astype(BF16), w_c.astype(BF16), cols[11].astype(BF16)


def _uq_layout(w_uq_l):
    per = MLA_NOPE + MLA_ROPE
    heads = [w_uq_l[:, hd * per:(hd + 1) * per] for hd in range(HEADS)]
    nope = [hh[:, :MLA_NOPE] for hh in heads]
    pe = [hh[:, MLA_NOPE:] for hh in heads]
    pairs = []
    for p in range(HEADS // 2):
        a, b = pe[2 * p], pe[2 * p + 1]
        pairs += [a[:, :32], b[:, :32], a[:, 32:], b[:, 32:]]
    return jnp.concatenate(nope + pairs, axis=1).astype(BF16)


def _ukv_layout(w_ukv_l):
    per = MLA_NOPE + MLA_V
    heads = [w_ukv_l[:, hd * per:(hd + 1) * per] for hd in range(HEADS)]
    return jnp.concatenate([hh[:, :MLA_NOPE] for hh in heads] + [hh[:, MLA_NOPE:] for hh in heads],
                           axis=1).astype(BF16)


def _block(n, want):
    return want if n % want == 0 else n


def kernel(x, positions, norm_mix_w, w_in, hg_lower_bounds, hg_norm_w, ret_norm_w, mla_q_norm_w, mla_w_uq, mla_kv_norm_w, mla_w_ukv, w_br_a, w_br_b, w_br_c, w_out, norm_ffn_w, w_ffn_in, w_ffn_out, final_norm_w):
    b, s, d = x.shape
    depth = w_in.shape[0]
    assert d == D_MODEL and s % (CHUNKS_PER_ITER * CHUNK) == 0
    t = b * s
    ts = _block(s, 2 * ROW_GROUP)
    assert ts % ROW_GROUP == 0
    tm = _block(t, 2 * ROW_GROUP)
    tq = _block(s, 512)
    assert tm % ROW_GROUP == 0

    lb_p = jax.nn.softmax(hg_lower_bounds.astype(F32), axis=0)
    lb_all = jnp.cumsum(lb_p, axis=0) - lb_p[0]

    rc, rs, mc, ms = _rope_tables(positions, _block(t, 512))
    row = lambda a: a.reshape(1, -1).astype(F32)

    x2 = x.reshape(t, d)
    for l in range(depth):
        w_hg, w_ret, w_c, w_g = _in_proj_slices(w_in[l])
        nw = row(norm_mix_w[l])
        ya, h = _hg_mixer(x2, nw, w_hg, row(lb_all[l]), row(hg_norm_w[l]), b, s, ts)
        yb, q, k, v = _ret_mixer(h, w_ret, row(ret_norm_w[l]), rc, rs, w_c, row(mla_q_norm_w[l]),
                                 _uq_layout(mla_w_uq[l]), row(mla_kv_norm_w[l]), _ukv_layout(mla_w_ukv[l]),
                                 mc, ms, b, s, ts)
        yc = _mla_attn(q, k, v, b, s, tq)
        x2 = _merge(x2, h, w_g, ya, yb, yc, w_br_a[l].astype(BF16), w_br_b[l].astype(BF16),
                    w_br_c[l].astype(BF16), w_out[l].astype(BF16), tm)
        x2 = _ffn(x2, row(norm_ffn_w[l]), w_ffn_in[l].astype(BF16), w_ffn_out[l].astype(BF16),
                  row(final_norm_w), tm, final_norm=(l == depth - 1))
    return x2.reshape(b, s, d)
```
